```python
import jax, jax.numpy as jnp
from jax import lax
import numpy as np

D_MODEL = 2048
BATCH = 2
SEQ = 8192
DEPTH = 2

D_MIX = D_MODEL
CONV_WIDTH = D_MIX // 2
LRU_WIDTH = D_MIX - CONV_WIDTH
LRU_HEADS = 8
LRU_HEAD_DIM = LRU_WIDTH // LRU_HEADS
D_IN = 2 * CONV_WIDTH + 2 * LRU_WIDTH
CONV_K = 31
LRU_CONV_K = 4
LRU_C = 8.0
D_FF = 3 * D_MODEL
N_EXPERTS = 8
TOP_K = 2
D_EXPERT = 3 * D_MODEL
MOE_BLOCK = 512
N_DENSE = (DEPTH + 1) // 2
N_MOE = DEPTH // 2
EPS = 1e-6

kernel_name = "hybrid_conformer_rglru_moe_trunk"


def rms_norm(x, g):
    xf = x.astype(jnp.float32)
    y = xf * lax.rsqrt(jnp.mean(xf * xf, axis=-1, keepdims=True) + EPS)
    return (y * g.astype(jnp.float32)).astype(x.dtype)


def causal_depthwise_conv(x, w, b):
    width, ch = w.shape
    y = lax.conv_general_dilated(
        x, w[:, None, :].astype(x.dtype), window_strides=(1,), padding=[(width - 1, 0)],
        dimension_numbers=("NWC", "WIO", "NWC"), feature_group_count=ch)
    return y + b.astype(x.dtype)


def conformer_conv(val, gate, conv_w, conv_b, ln_g, ln_b):
    c = val * jax.nn.sigmoid(gate)
    c = causal_depthwise_conv(c, conv_w, conv_b)
    cf = c.astype(jnp.float32)
    mu = jnp.mean(cf, axis=-1, keepdims=True)
    var = jnp.mean(jnp.square(cf - mu), axis=-1, keepdims=True)
    cn = (cf - mu) * lax.rsqrt(var + EPS) * ln_g.astype(jnp.float32) + ln_b.astype(jnp.float32)
    return jax.nn.silu(cn).astype(val.dtype)


def _linear_recurrence_combine(c1, c2):
    a1, b1 = c1
    a2, b2 = c2
    return a1 * a2, a2 * b1 + b2


def rg_lru(x, wa, ba, wx, bx, lam):
    bsz, seq, width = x.shape
    xh = x.reshape(bsz, seq, LRU_HEADS, LRU_HEAD_DIM)
    gate_a = jax.nn.sigmoid(jnp.einsum("bshi,hij->bshj", xh, wa.astype(x.dtype)).reshape(bsz, seq, width) + ba.astype(x.dtype))
    gate_x = jax.nn.sigmoid(jnp.einsum("bshi,hij->bshj", xh, wx.astype(x.dtype)).reshape(bsz, seq, width) + bx.astype(x.dtype))
    log_a = -LRU_C * gate_a.astype(jnp.float32) * jax.nn.softplus(-lam.astype(jnp.float32))
    a = jnp.exp(log_a)
    mult = jnp.sqrt(-jnp.expm1(2.0 * log_a))
    is_first = (jnp.arange(seq) == 0)[None, :, None]
    mult = jnp.where(is_first, jnp.ones_like(mult), mult)
    b = mult * (gate_x * x).astype(jnp.float32)
    _, h = lax.associative_scan(_linear_recurrence_combine, (a, b), axis=1)
    return h.astype(x.dtype)


def hybrid_mixer(h, w_in, conv_w, conv_b, conv_ln_g, conv_ln_b,
                 lru_conv_w, lru_conv_b, lru_wa, lru_ba, lru_wx, lru_bx, lru_lambda, w_out):
    u = jnp.einsum("bsd,de->bse", h, w_in.astype(h.dtype))
    splits = [CONV_WIDTH, 2 * CONV_WIDTH, 2 * CONV_WIDTH + LRU_WIDTH]
    c_val, c_gate, r_x, r_gate = jnp.split(u, splits, axis=-1)
    y_conv = conformer_conv(c_val, c_gate, conv_w, conv_b, conv_ln_g, conv_ln_b)
    r_x = causal_depthwise_conv(r_x, lru_conv_w, lru_conv_b)
    y_lru = rg_lru(r_x, lru_wa, lru_ba, lru_wx, lru_bx, lru_lambda) * jax.nn.gelu(r_gate)
    y = jnp.concatenate([y_conv, y_lru], axis=-1)
    return jnp.einsum("bse,ed->bsd", y, w_out.astype(h.dtype))


def swiglu(h, wg, wu, wd):
    g = jnp.einsum("bsd,df->bsf", h, wg.astype(h.dtype))
    v = jnp.einsum("bsd,df->bsf", h, wu.astype(h.dtype))
    return jnp.einsum("bsf,fd->bsd", jax.nn.silu(g) * v, wd.astype(h.dtype))


def moe_swiglu(h, w_router, wg, wu, wd):
    bsz, seq, d = h.shape
    n_tok = bsz * seq
    n_asg = n_tok * TOP_K
    xf = h.reshape(n_tok, d)
    logits = jnp.einsum("td,de->te", xf, w_router.astype(xf.dtype)).astype(jnp.float32)
    probs = jax.nn.softmax(logits, axis=-1)
    top_p, top_e = lax.top_k(probs, TOP_K)
    top_p = top_p / jnp.sum(top_p, axis=-1, keepdims=True)
    flat_e = top_e.reshape(-1).astype(jnp.int32)
    flat_tok = jnp.arange(n_asg, dtype=jnp.int32) // TOP_K
    flat_w = top_p.reshape(-1)
    order = jnp.argsort(flat_e)
    s_e, s_tok, s_w = flat_e[order], flat_tok[order], flat_w[order]
    counts = jnp.bincount(flat_e, length=N_EXPERTS).astype(jnp.int32)
    starts = jnp.cumsum(counts) - counts
    padded = ((counts + MOE_BLOCK - 1) // MOE_BLOCK) * MOE_BLOCK
    p_ends = jnp.cumsum(padded)
    p_starts = p_ends - padded
    pos = p_starts[s_e] + (jnp.arange(n_asg, dtype=jnp.int32) - starts[s_e])
    n_blocks = -(-n_asg // MOE_BLOCK) + N_EXPERTS
    n_rows = n_blocks * MOE_BLOCK
    x_buf = jnp.zeros((n_rows, d), xf.dtype).at[pos].set(xf[s_tok])
    block_start = jnp.arange(n_blocks, dtype=jnp.int32) * MOE_BLOCK
    block_e = jnp.minimum(jnp.searchsorted(p_ends, block_start, side="right"), N_EXPERTS - 1)

    def expert_block(args):
        xb, e = args
        g = xb @ wg[e].astype(xb.dtype)
        v = xb @ wu[e].astype(xb.dtype)
        return (jax.nn.silu(g) * v) @ wd[e].astype(xb.dtype)

    y_buf = lax.map(expert_block, (x_buf.reshape(n_blocks, MOE_BLOCK, d), block_e)).reshape(n_rows, d)
    y = jnp.zeros((n_tok, d), xf.dtype).at[s_tok].add(y_buf[pos] * s_w[:, None].astype(xf.dtype))
    return y.reshape(bsz, seq, d)


def setup_inputs(seed: int = 0) -> dict:
    key = jax.random.key(seed)
    ks = jax.random.split(key, 24)
    f32 = jnp.float32

    def nrm(k, shape, scale):
        return jax.random.normal(k, shape, f32) * scale

    lam_v = jax.random.uniform(ks[13], (DEPTH, LRU_WIDTH), f32, minval=0.9, maxval=0.999)
    lam_s = lam_v ** (1.0 / LRU_C)
    return {
        "x": nrm(ks[0], (BATCH, SEQ, D_MODEL), 1.0),
        "mix_norm": 1.0 + nrm(ks[1], (DEPTH, D_MODEL), 0.02),
        "w_in": nrm(ks[2], (DEPTH, D_MODEL, D_IN), D_MODEL ** -0.5),
        "conv_w": nrm(ks[3], (DEPTH, CONV_K, CONV_WIDTH), CONV_K ** -0.5),
        "conv_b": nrm(ks[4], (DEPTH, CONV_WIDTH), 0.02),
        "conv_ln_g": 1.0 + nrm(ks[5], (DEPTH, CONV_WIDTH), 0.02),
        "conv_ln_b": nrm(ks[6], (DEPTH, CONV_WIDTH), 0.02),
        "lru_conv_w": nrm(ks[7], (DEPTH, LRU_CONV_K, LRU_WIDTH), LRU_CONV_K ** -0.5),
        "lru_conv_b": nrm(ks[8], (DEPTH, LRU_WIDTH), 0.02),
        "lru_wa": nrm(ks[9], (DEPTH, LRU_HEADS, LRU_HEAD_DIM, LRU_HEAD_DIM), LRU_HEAD_DIM ** -0.5),
        "lru_ba": nrm(ks[10], (DEPTH, LRU_WIDTH), 0.02),
        "lru_wx": nrm(ks[11], (DEPTH, LRU_HEADS, LRU_HEAD_DIM, LRU_HEAD_DIM), LRU_HEAD_DIM ** -0.5),
        "lru_bx": nrm(ks[12], (DEPTH, LRU_WIDTH), 0.02),
        "lru_lambda": jnp.log(lam_s) - jnp.log1p(-lam_s),
        "w_out": nrm(ks[14], (DEPTH, D_MIX, D_MODEL), D_MIX ** -0.5),
        "ffn_norm": 1.0 + nrm(ks[15], (DEPTH, D_MODEL), 0.02),
        "dense_wg": nrm(ks[16], (N_DENSE, D_MODEL, D_FF), D_MODEL ** -0.5),
        "dense_wu": nrm(ks[17], (N_DENSE, D_MODEL, D_FF), D_MODEL ** -0.5),
        "dense_wd": nrm(ks[18], (N_DENSE, D_FF, D_MODEL), D_FF ** -0.5),
        "w_router": nrm(ks[19], (N_MOE, D_MODEL, N_EXPERTS), D_MODEL ** -0.5),
        "moe_wg": nrm(ks[20], (N_MOE, N_EXPERTS, D_MODEL, D_EXPERT), D_MODEL ** -0.5),
        "moe_wu": nrm(ks[21], (N_MOE, N_EXPERTS, D_MODEL, D_EXPERT), D_MODEL ** -0.5),
        "moe_wd": nrm(ks[22], (N_MOE, N_EXPERTS, D_EXPERT, D_MODEL), D_EXPERT ** -0.5),
        "final_norm": 1.0 + nrm(ks[23], (D_MODEL,), 0.02),
    }


def reference(x, mix_norm, w_in, conv_w, conv_b, conv_ln_g, conv_ln_b,
              lru_conv_w, lru_conv_b, lru_wa, lru_ba, lru_wx, lru_bx, lru_lambda,
              w_out, ffn_norm, dense_wg, dense_wu, dense_wd,
              w_router, moe_wg, moe_wu, moe_wd, final_norm):
    for layer in range(DEPTH):
        h = rms_norm(x, mix_norm[layer])
        x = x + hybrid_mixer(h, w_in[layer], conv_w[layer], conv_b[layer], conv_ln_g[layer], conv_ln_b[layer],
                             lru_conv_w[layer], lru_conv_b[layer], lru_wa[layer], lru_ba[layer],
                             lru_wx[layer], lru_bx[layer], lru_lambda[layer], w_out[layer])
        h = rms_norm(x, ffn_norm[layer])
        j = layer // 2
        if layer % 2 == 0:
            x = x + swiglu(h, dense_wg[j], dense_wu[j], dense_wd[j])
        else:
            x = x + moe_swiglu(h, w_router[j], moe_wg[j], moe_wu[j], moe_wd[j])
    return rms_norm(x, final_norm)
```

```python
import functools

import jax
import jax.numpy as jnp
from jax import lax
from jax.experimental import pallas as pl
from jax.experimental.pallas import tpu as pltpu

F32 = jnp.float32
BF16 = jnp.bfloat16

D_MODEL = 2048
CONV_WIDTH = 1024
LRU_WIDTH = 1024
LRU_HEADS = 8
LRU_HEAD_DIM = LRU_WIDTH // LRU_HEADS
D_IN = 2 * CONV_WIDTH + 2 * LRU_WIDTH
CONV_K = 31
LRU_CONV_K = 4
LRU_C = 8.0
N_EXPERTS = 8
TOP_K = 2
EPS = 1e-6

LANES = 128
SUBLANES = 8
MIB = 1024 * 1024

TM_PROJ = 512
TN_IN = 1024
TS_MIX = 256
CONV_HALO = 32
LRU_HALO = 8
CONV_ROWS = 64
TM_FFN = 512
TF_FFN = 512
MOE_BM = 512
TM_ROUTER = 512
TM_COMBINE = 256


def _cparams(semantics, vmem_mib):
    return pltpu.CompilerParams(dimension_semantics=semantics, vmem_limit_bytes=vmem_mib * MIB)


def _sigmoid(x):
    return 1.0 / (1.0 + jnp.exp(-x))


def _rms_norm(x, g):
    ms = jnp.mean(x * x, axis=-1, keepdims=True)
    return x * lax.rsqrt(ms + EPS) * g


def _in_proj_kernel(x_ref, g_ref, w_ref, o_ref, h_ref):
    @pl.when(pl.program_id(1) == 0)
    def _():
        h_ref[...] = _rms_norm(x_ref[...], g_ref[...]).astype(BF16)

    o_ref[...] = jnp.dot(h_ref[...], w_ref[...], preferred_element_type=F32)


def _in_proj(x, g, w):
    t, d = x.shape
    n = w.shape[1]
    return pl.pallas_call(
        _in_proj_kernel,
        grid=(t // TM_PROJ, n // TN_IN),
        in_specs=[
            pl.BlockSpec((TM_PROJ, d), lambda i, j: (i, 0)),
            pl.BlockSpec((1, d), lambda i, j: (0, 0)),
            pl.BlockSpec((d, TN_IN), lambda i, j: (0, j)),
        ],
        out_specs=pl.BlockSpec((TM_PROJ, TN_IN), lambda i, j: (i, j)),
        out_shape=jax.ShapeDtypeStruct((t, n), F32),
        scratch_shapes=[pltpu.VMEM((TM_PROJ, d), BF16)],
        compiler_params=_cparams(("parallel", "arbitrary"), 40),
        name="in_proj",
    )(x, g, w)


def _mixer_kernel(u_ref, cw_ref, cb_ref, lg_ref, lb_ref, rw_ref, rb_ref, wa_ref, ba_ref,
                  wx_ref, bx_ref, lam_ref, o_ref, cbuf, rbuf, abuf, bbuf, hc_ref):
    s = pl.program_id(1)
    ts = TS_MIX

    @pl.when(s == 0)
    def _():
        cbuf[0:CONV_HALO, :] = jnp.zeros((CONV_HALO, CONV_WIDTH), F32)
        rbuf[0:LRU_HALO, :] = jnp.zeros((LRU_HALO, LRU_WIDTH), F32)
        hc_ref[...] = jnp.zeros((SUBLANES, LRU_WIDTH), F32)

    val = u_ref[:, 0:CONV_WIDTH]
    gate = u_ref[:, CONV_WIDTH:2 * CONV_WIDTH]
    cbuf[CONV_HALO:CONV_HALO + ts, :] = val * _sigmoid(gate)
    base = CONV_HALO - (CONV_K - 1)
    for rc in range(ts // CONV_ROWS):
        r0 = rc * CONV_ROWS
        for lc in range(CONV_WIDTH // LANES):
            cols = slice(lc * LANES, (lc + 1) * LANES)
            acc = jnp.broadcast_to(cb_ref[:, cols], (CONV_ROWS, LANES))
            for k in range(CONV_K):
                acc = acc + cw_ref[k:k + 1, cols] * cbuf[r0 + base + k:r0 + base + k + CONV_ROWS, cols]
            abuf[r0:r0 + CONV_ROWS, cols] = acc
    cbuf[0:CONV_HALO, :] = cbuf[ts:ts + CONV_HALO, :]
    c = abuf[...]
    mu = jnp.mean(c, axis=-1, keepdims=True)
    cc = c - mu
    var = jnp.mean(cc * cc, axis=-1, keepdims=True)
    cn = cc * lax.rsqrt(var + EPS) * lg_ref[...] + lb_ref[...]
    o_ref[:, 0:CONV_WIDTH] = (cn * _sigmoid(cn)).astype(o_ref.dtype)

    rbuf[LRU_HALO:LRU_HALO + ts, :] = u_ref[:, 2 * CONV_WIDTH:2 * CONV_WIDTH + LRU_WIDTH]
    rbase = LRU_HALO - (LRU_CONV_K - 1)
    xc = jnp.broadcast_to(rb_ref[...], (ts, LRU_WIDTH))
    for k in range(LRU_CONV_K):
        xc = xc + rw_ref[k:k + 1, :] * rbuf[rbase + k:rbase + k + ts, :]
    rbuf[0:LRU_HALO, :] = rbuf[ts:ts + LRU_HALO, :]
    bbuf[...] = xc
    for h in range(LRU_HEADS):
        cols = slice(h * LRU_HEAD_DIM, (h + 1) * LRU_HEAD_DIM)
        xh = bbuf[:, cols]
        xh_b = xh.astype(BF16)
        ga = jnp.dot(xh_b, wa_ref[h], preferred_element_type=F32) + ba_ref[:, cols]
        gx = jnp.dot(xh_b, wx_ref[h], preferred_element_type=F32) + bx_ref[:, cols]
        lam = lam_ref[:, cols]
        e = jnp.exp(-jnp.abs(lam))
        e1 = 1.0 + e
        log1p_e = jnp.where(e1 == 1.0, e, jnp.log(e1) * (e / (e1 - 1.0)))
        sp = jnp.maximum(-lam, 0.0) + log1p_e
        log_a = (-LRU_C) * _sigmoid(ga) * sp
        a = jnp.exp(log_a)
        mult = jnp.sqrt(-jnp.tanh(log_a) * (1.0 + a * a))
        row = lax.broadcasted_iota(jnp.int32, (ts, LRU_HEAD_DIM), 0)
        mult = jnp.where((row == 0) & (s == 0), 1.0, mult)
        abuf[:, cols] = a
        bbuf[:, cols] = mult * (_sigmoid(gx) * xh)

    row8 = lax.broadcasted_iota(jnp.int32, (SUBLANES, LRU_WIDTH), 0)
    h_prev = hc_ref[...]
    for g in range(ts // SUBLANES):
        rows = slice(g * SUBLANES, (g + 1) * SUBLANES)
        a = abuf[rows, :]
        b = bbuf[rows, :]
        for dsh in (1, 2, 4):
            a_s = pltpu.roll(a, dsh, axis=0)
            b_s = pltpu.roll(b, dsh, axis=0)
            m = row8 >= dsh
            b = jnp.where(m, a * b_s + b, b)
            a = jnp.where(m, a * a_s, a)
        hg = a * h_prev + b
        abuf[rows, :] = hg
        h_prev = jnp.broadcast_to(hg[SUBLANES - 1:SUBLANES, :], (SUBLANES, LRU_WIDTH))
    hc_ref[...] = h_prev

    rg = u_ref[:, 2 * CONV_WIDTH + LRU_WIDTH:D_IN]
    gelu = 0.5 * rg * (1.0 + jnp.tanh(0.7978845608028654 * (rg + 0.044715 * (rg * rg * rg))))
    o_ref[:, CONV_WIDTH:CONV_WIDTH + LRU_WIDTH] = (abuf[...] * gelu).astype(o_ref.dtype)


def _mixer_core(u, bsz, seq, cw, cb, lg, lb, rw, rb, wa, ba, wx, bx, lam):
    t = u.shape[0]
    nsb = seq // TS_MIX
    row = lambda n: pl.BlockSpec((1, n), lambda b, s: (0, 0))
    return pl.pallas_call(
        _mixer_kernel,
        grid=(bsz, nsb),
        in_specs=[
            pl.BlockSpec((TS_MIX, D_IN), lambda b, s: (b * nsb + s, 0)),
            pl.BlockSpec((CONV_K, CONV_WIDTH), lambda b, s: (0, 0)),
            row(CONV_WIDTH), row(CONV_WIDTH), row(CONV_WIDTH),
            pl.BlockSpec((LRU_CONV_K, LRU_WIDTH), lambda b, s: (0, 0)),
            row(LRU_WIDTH),
            pl.BlockSpec((LRU_HEADS, LRU_HEAD_DIM, LRU_HEAD_DIM), lambda b, s: (0, 0, 0)),
            row(LRU_WIDTH),
            pl.BlockSpec((LRU_HEADS, LRU_HEAD_DIM, LRU_HEAD_DIM), lambda b, s: (0, 0, 0)),
            row(LRU_WIDTH), row(LRU_WIDTH),
        ],
        out_specs=pl.BlockSpec((TS_MIX, CONV_WIDTH + LRU_WIDTH), lambda b, s: (b * nsb + s, 0)),
        out_shape=jax.ShapeDtypeStruct((t, CONV_WIDTH + LRU_WIDTH), BF16),
        scratch_shapes=[
            pltpu.VMEM((CONV_HALO + TS_MIX, CONV_WIDTH), F32),
            pltpu.VMEM((LRU_HALO + TS_MIX, LRU_WIDTH), F32),
            pltpu.VMEM((TS_MIX, LRU_WIDTH), F32),
            pltpu.VMEM((TS_MIX, LRU_WIDTH), F32),
            pltpu.VMEM((SUBLANES, LRU_WIDTH), F32),
        ],
        compiler_params=_cparams(("parallel", "arbitrary"), 40),
        name="mixer_core",
    )(u, cw, cb, lg, lb, rw, rb, wa, ba, wx, bx, lam)


def _out_proj_kernel(x_ref, y_ref, w_ref, o_ref):
    o_ref[...] = x_ref[...] + jnp.dot(y_ref[...], w_ref[...], preferred_element_type=F32)


def _out_proj(x, y, w):
    t, d = x.shape
    k = y.shape[1]
    return pl.pallas_call(
        _out_proj_kernel,
        grid=(t // TM_PROJ,),
        in_specs=[
            pl.BlockSpec((TM_PROJ, d), lambda i: (i, 0)),
            pl.BlockSpec((TM_PROJ, k), lambda i: (i, 0)),
            pl.BlockSpec((k, d), lambda i: (0, 0)),
        ],
        out_specs=pl.BlockSpec((TM_PROJ, d), lambda i: (i, 0)),
        out_shape=jax.ShapeDtypeStruct((t, d), F32),
        compiler_params=_cparams(("parallel",), 48),
        name="out_proj",
    )(x, y, w)


def _swiglu_step(h, wg, wu, wd):
    g = jnp.dot(h, wg, preferred_element_type=F32)
    v = jnp.dot(h, wu, preferred_element_type=F32)
    act = (g * _sigmoid(g) * v).astype(BF16)
    return jnp.dot(act, wd, preferred_element_type=F32)


def _dense_ffn_kernel(x_ref, g_ref, wg_ref, wu_ref, wd_ref, o_ref, h_ref, acc_ref):
    f = pl.program_id(1)

    @pl.when(f == 0)
    def _():
        h_ref[...] = _rms_norm(x_ref[...], g_ref[...]).astype(BF16)
        acc_ref[...] = x_ref[...]

    acc_ref[...] += _swiglu_step(h_ref[...], wg_ref[...], wu_ref[...], wd_ref[...])

    @pl.when(f == pl.num_programs(1) - 1)
    def _():
        o_ref[...] = acc_ref[...]


def _dense_ffn(x, g, wg, wu, wd):
    t, d = x.shape
    ff = wg.shape[1]
    return pl.pallas_call(
        _dense_ffn_kernel,
        grid=(t // TM_FFN, ff // TF_FFN),
        in_specs=[
            pl.BlockSpec((TM_FFN, d), lambda i, f: (i, 0)),
            pl.BlockSpec((1, d), lambda i, f: (0, 0)),
            pl.BlockSpec((d, TF_FFN), lambda i, f: (0, f)),
            pl.BlockSpec((d, TF_FFN), lambda i, f: (0, f)),
            pl.BlockSpec((TF_FFN, d), lambda i, f: (f, 0)),
        ],
        out_specs=pl.BlockSpec((TM_FFN, d), lambda i, f: (i, 0)),
        out_shape=jax.ShapeDtypeStruct((t, d), F32),
        scratch_shapes=[pltpu.VMEM((TM_FFN, d), BF16), pltpu.VMEM((TM_FFN, d), F32)],
        compiler_params=_cparams(("parallel", "arbitrary"), 48),
        name="dense_ffn",
    )(x, g, wg, wu, wd)


def _router_kernel(x_ref, g_ref, wr_ref, o_ref):
    h = _rms_norm(x_ref[...], g_ref[...])
    logits = jnp.dot(h, wr_ref[...], preferred_element_type=F32, precision=lax.Precision.HIGHEST)
    lane = lax.broadcasted_iota(jnp.int32, logits.shape, 1)
    neg = jnp.float32(-jnp.inf)
    l1 = jnp.where(lane < N_EXPERTS, logits, neg)
    m1 = jnp.max(l1, axis=-1, keepdims=True)
    i1 = jnp.min(jnp.where(l1 == m1, lane, LANES), axis=-1, keepdims=True)
    l2 = jnp.where(lane == i1, neg, l1)
    m2 = jnp.max(l2, axis=-1, keepdims=True)
    i2 = jnp.min(jnp.where(l2 == m2, lane, LANES), axis=-1, keepdims=True)
    dlt = jnp.exp(m2 - m1)
    w1 = 1.0 / (1.0 + dlt)
    w2 = dlt / (1.0 + dlt)
    out = jnp.where(lane == 0, i1.astype(F32),
                    jnp.where(lane == 1, i2.astype(F32),
                              jnp.where(lane == 2, w1, jnp.where(lane == 3, w2, 0.0))))
    o_ref[...] = out[:, 0:o_ref.shape[1]]


def _router(x, g, wr_pad):
    t, d = x.shape
    return pl.pallas_call(
        _router_kernel,
        grid=(t // TM_ROUTER,),
        in_specs=[
            pl.BlockSpec((TM_ROUTER, d), lambda i: (i, 0)),
            pl.BlockSpec((1, d), lambda i: (0, 0)),
            pl.BlockSpec((d, LANES), lambda i: (0, 0)),
        ],
        out_specs=pl.BlockSpec((TM_ROUTER, SUBLANES), lambda i: (i, 0)),
        out_shape=jax.ShapeDtypeStruct((t, SUBLANES), F32),
        compiler_params=_cparams(("parallel",), 32),
        name="router",
    )(x, g, wr_pad)


def _moe_ffn_kernel(src_ref, be_ref, nr_ref, x_hbm, g_ref, wg_ref, wu_ref, wd_ref, o_ref,
                    xg_ref, h_ref, acc_ref, sem):
    b = pl.program_id(0)
    f = pl.program_id(1)
    real = b < nr_ref[0]

    @pl.when(real & (f == 0))
    def _():
        def issue(r, carry):
            tok = src_ref[b * MOE_BM + r]
            pltpu.make_async_copy(x_hbm.at[pl.ds(tok, 1), :], xg_ref.at[pl.ds(r, 1), :], sem).start()
            return carry

        lax.fori_loop(0, MOE_BM, issue, 0)
        pltpu.make_async_copy(x_hbm.at[pl.ds(0, MOE_BM), :], xg_ref, sem).wait()
        h_ref[...] = _rms_norm(xg_ref[...], g_ref[...]).astype(BF16)

    @pl.when(real)
    def _():
        y = _swiglu_step(h_ref[...], wg_ref[0], wu_ref[0], wd_ref[0])

        @pl.when(f == 0)
        def _():
            acc_ref[...] = y

        @pl.when(f > 0)
        def _():
            acc_ref[...] += y

        @pl.when(f == pl.num_programs(1) - 1)
        def _():
            o_ref[...] = acc_ref[...]

    @pl.when(jnp.logical_not(real) & (f == pl.num_programs(1) - 1))
    def _():
        o_ref[...] = jnp.zeros(o_ref.shape, o_ref.dtype)


def _moe_ffn(src, block_e, n_real, x, g, wg, wu, wd, n_blocks):
    t, d = x.shape
    ff = wg.shape[2]
    nf = ff // TF_FFN

    def f_eff(b, f, nr):
        return jnp.where(b < nr[0], f, nf - 1)

    grid_spec = pltpu.PrefetchScalarGridSpec(
        num_scalar_prefetch=3,
        grid=(n_blocks, nf),
        in_specs=[
            pl.BlockSpec(memory_space=pl.ANY),
            pl.BlockSpec((1, d), lambda b, f, src, be, nr: (0, 0)),
            pl.BlockSpec((1, d, TF_FFN), lambda b, f, src, be, nr: (be[b], 0, f_eff(b, f, nr))),
            pl.BlockSpec((1, d, TF_FFN), lambda b, f, src, be, nr: (be[b], 0, f_eff(b, f, nr))),
            pl.BlockSpec((1, TF_FFN, d), lambda b, f, src, be, nr: (be[b], f_eff(b, f, nr), 0)),
        ],
        out_specs=pl.BlockSpec((MOE_BM, d), lambda b, f, src, be, nr: (b, 0)),
        scratch_shapes=[
            pltpu.VMEM((MOE_BM, d), F32),
            pltpu.VMEM((MOE_BM, d), BF16),
            pltpu.VMEM((MOE_BM, d), F32),
            pltpu.SemaphoreType.DMA(()),
        ],
    )
    return pl.pallas_call(
        _moe_ffn_kernel,
        grid_spec=grid_spec,
        out_shape=jax.ShapeDtypeStruct((n_blocks * MOE_BM, d), F32),
        compiler_params=_cparams(("arbitrary", "arbitrary"), 48),
        name="moe_ffn",
    )(src, block_e, n_real, x, g, wg, wu, wd)


def _combine_kernel(pos_ref, x_ref, info_ref, g_ref, y_hbm, o_ref, ya_ref, yb_ref, sem, *, final_norm):
    i = pl.program_id(0)

    def issue(r, carry):
        tok = i * TM_COMBINE + r
        pa = pos_ref[2 * tok]
        pb = pos_ref[2 * tok + 1]
        pltpu.make_async_copy(y_hbm.at[pl.ds(pa, 1), :], ya_ref.at[pl.ds(r, 1), :], sem.at[0]).start()
        pltpu.make_async_copy(y_hbm.at[pl.ds(pb, 1), :], yb_ref.at[pl.ds(r, 1), :], sem.at[1]).start()
        return carry

    lax.fori_loop(0, TM_COMBINE, issue, 0)
    pltpu.make_async_copy(y_hbm.at[pl.ds(0, TM_COMBINE), :], ya_ref, sem.at[0]).wait()
    pltpu.make_async_copy(y_hbm.at[pl.ds(0, TM_COMBINE), :], yb_ref, sem.at[1]).wait()
    w1 = info_ref[:, 2:3]
    w2 = info_ref[:, 3:4]
    z = x_ref[...] + (ya_ref[...] * w1 + yb_ref[...] * w2)
    o_ref[...] = _rms_norm(z, g_ref[...]) if final_norm else z


def _combine(pos, x, info, g, y_buf, final_norm):
    t, d = x.shape
    grid_spec = pltpu.PrefetchScalarGridSpec(
        num_scalar_prefetch=1,
        grid=(t // TM_COMBINE,),
        in_specs=[
            pl.BlockSpec((TM_COMBINE, d), lambda i, pos: (i, 0)),
            pl.BlockSpec((TM_COMBINE, SUBLANES), lambda i, pos: (i, 0)),
            pl.BlockSpec((1, d), lambda i, pos: (0, 0)),
            pl.BlockSpec(memory_space=pl.ANY),
        ],
        out_specs=pl.BlockSpec((TM_COMBINE, d), lambda i, pos: (i, 0)),
        scratch_shapes=[
            pltpu.VMEM((TM_COMBINE, d), F32),
            pltpu.VMEM((TM_COMBINE, d), F32),
            pltpu.SemaphoreType.DMA((2,)),
        ],
    )
    return pl.pallas_call(
        functools.partial(_combine_kernel, final_norm=final_norm),
        grid_spec=grid_spec,
        out_shape=jax.ShapeDtypeStruct((t, d), F32),
        compiler_params=_cparams(("arbitrary",), 32),
        name="moe_combine",
    )(pos, x, info, g, y_buf)


def _routing_plan(info, n_tok):
    n_asg = n_tok * TOP_K
    n_blocks = n_asg // MOE_BM + N_EXPERTS
    flat_e = info[:, 0:TOP_K].astype(jnp.int32).reshape(n_asg)
    onehot = (flat_e[:, None] == jnp.arange(N_EXPERTS, dtype=jnp.int32)[None, :]).astype(jnp.int32)
    csum = jnp.cumsum(onehot, axis=0)
    counts = csum[-1]
    padded = ((counts + MOE_BM - 1) // MOE_BM) * MOE_BM
    p_ends = jnp.cumsum(padded)
    p_starts = p_ends - padded
    pos = jnp.sum(onehot * (csum - 1 + p_starts[None, :]), axis=1).astype(jnp.int32)
    flat_tok = jnp.arange(n_asg, dtype=jnp.int32) // TOP_K
    src = jnp.zeros((n_blocks * MOE_BM,), jnp.int32).at[pos].set(flat_tok)
    block_start = jnp.arange(n_blocks, dtype=jnp.int32) * MOE_BM
    block_e = jnp.minimum(jnp.searchsorted(p_ends, block_start, side="right"), N_EXPERTS - 1).astype(jnp.int32)
    n_real = (p_ends[-1] // MOE_BM).astype(jnp.int32).reshape(1)
    return pos, src, block_e, n_real, n_blocks


def kernel(x, mix_norm, w_in, conv_w, conv_b, conv_ln_g, conv_ln_b, lru_conv_w, lru_conv_b, lru_wa, lru_ba, lru_wx, lru_bx, lru_lambda, w_out, ffn_norm, dense_wg, dense_wu, dense_wd, w_router, moe_wg, moe_wu, moe_wd, final_norm):
    bsz, seq, d = x.shape
    depth = w_in.shape[0]
    t = bsz * seq
    xt = x.reshape(t, d)
    row = lambda v: v.reshape(1, -1)
    assert depth % 2 == 0, "the final RMSNorm is fused into the last routed layer's combine"
    for layer in range(depth):
        u = _in_proj(xt, row(mix_norm[layer]), w_in[layer].astype(BF16))
        y = _mixer_core(u, bsz, seq, conv_w[layer], row(conv_b[layer]), row(conv_ln_g[layer]),
                        row(conv_ln_b[layer]), lru_conv_w[layer], row(lru_conv_b[layer]),
                        lru_wa[layer].astype(BF16), row(lru_ba[layer]), lru_wx[layer].astype(BF16),
                        row(lru_bx[layer]), row(lru_lambda[layer]))
        xt = _out_proj(xt, y, w_out[layer].astype(BF16))
        j = layer // 2
        g = row(ffn_norm[layer])
        if layer % 2 == 0:
            xt = _dense_ffn(xt, g, dense_wg[j].astype(BF16), dense_wu[j].astype(BF16), dense_wd[j].astype(BF16))
        else:
            wr_pad = jnp.zeros((d, LANES), F32).at[:, 0:N_EXPERTS].set(w_router[j])
            info = _router(xt, g, wr_pad)
            pos, src, block_e, n_real, n_blocks = _routing_plan(info, t)
            y_buf = _moe_ffn(src, block_e, n_real, xt, g, moe_wg[j].astype(BF16), moe_wu[j].astype(BF16),
                             moe_wd[j].astype(BF16), n_blocks)
            last = layer == depth - 1
            xt = _combine(pos, xt, info, row(final_norm), y_buf, last)
    return xt.reshape(bsz, seq, d)
```

```python
import functools

import jax
import jax.numpy as jnp
from jax import lax
from jax.experimental import pallas as pl
from jax.experimental.pallas import tpu as pltpu

F32 = jnp.float32
BF16 = jnp.bfloat16

D_MODEL = 2048
CONV_WIDTH = 1024
LRU_WIDTH = 1024
LRU_HEADS = 8
LRU_HEAD_DIM = LRU_WIDTH // LRU_HEADS
D_IN = 2 * CONV_WIDTH + 2 * LRU_WIDTH
CONV_K = 31
LRU_CONV_K = 4
LRU_C = 8.0
N_EXPERTS = 8
TOP_K = 2
EPS = 1e-6

LANES = 128
SUBLANES = 8
MIB = 1024 * 1024

TM_PROJ = 512
TN_IN = 1024
TS_MIX = 256
CONV_HALO = 32
LRU_HALO = 8
CONV_ROWS = 64
TM_FFN = 512
TF_FFN = 1024
MOE_BM = 512
TM_ROUTER = 512
TM_COMBINE = 512


def _cparams(semantics, vmem_mib):
    return pltpu.CompilerParams(dimension_semantics=semantics, vmem_limit_bytes=vmem_mib * MIB)


def _sigmoid(x):
    return 1.0 / (1.0 + jnp.exp(-x))


def _rms_norm(x, g):
    ms = jnp.mean(x * x, axis=-1, keepdims=True)
    return x * lax.rsqrt(ms + EPS) * g


def _in_proj_kernel(x_ref, g_ref, w_ref, o_ref):
    h = _rms_norm(x_ref[...], g_ref[...]).astype(BF16)
    for j in range(o_ref.shape[1] // TN_IN):
        cols = slice(j * TN_IN, (j + 1) * TN_IN)
        o_ref[:, cols] = jnp.dot(h, w_ref[:, cols], preferred_element_type=F32)


def _in_proj(x, g, w):
    t, d = x.shape
    n = w.shape[1]
    return pl.pallas_call(
        _in_proj_kernel,
        grid=(t // TM_PROJ,),
        in_specs=[
            pl.BlockSpec((TM_PROJ, d), lambda i: (i, 0)),
            pl.BlockSpec((1, d), lambda i: (0, 0)),
            pl.BlockSpec((d, n), lambda i: (0, 0), pipeline_mode=pl.Buffered(1)),
        ],
        out_specs=pl.BlockSpec((TM_PROJ, n), lambda i: (i, 0)),
        out_shape=jax.ShapeDtypeStruct((t, n), F32),
        compiler_params=_cparams(("parallel",), 56),
        name="in_proj",
    )(x, g, w)


def _mixer_kernel(u_ref, cw_ref, cb_ref, lg_ref, lb_ref, rw_ref, rb_ref, wa_ref, ba_ref,
                  wx_ref, bx_ref, lam_ref, o_ref, cbuf, rbuf, abuf, bbuf, hc_ref):
    s = pl.program_id(1)
    ts = TS_MIX

    @pl.when(s == 0)
    def _():
        cbuf[0:CONV_HALO, :] = jnp.zeros((CONV_HALO, CONV_WIDTH), F32)
        rbuf[0:LRU_HALO, :] = jnp.zeros((LRU_HALO, LRU_WIDTH), F32)
        hc_ref[...] = jnp.zeros((SUBLANES, LRU_WIDTH), F32)

    val = u_ref[:, 0:CONV_WIDTH]
    gate = u_ref[:, CONV_WIDTH:2 * CONV_WIDTH]
    cbuf[CONV_HALO:CONV_HALO + ts, :] = val * _sigmoid(gate)
    base = CONV_HALO - (CONV_K - 1)
    for rc in range(ts // CONV_ROWS):
        r0 = rc * CONV_ROWS
        for lc in range(CONV_WIDTH // LANES):
            cols = slice(lc * LANES, (lc + 1) * LANES)
            acc = jnp.broadcast_to(cb_ref[:, cols], (CONV_ROWS, LANES))
            for k in range(CONV_K):
                acc = acc + cw_ref[k:k + 1, cols] * cbuf[r0 + base + k:r0 + base + k + CONV_ROWS, cols]
            abuf[r0:r0 + CONV_ROWS, cols] = acc
    cbuf[0:CONV_HALO, :] = cbuf[ts:ts + CONV_HALO, :]
    c = abuf[...]
    mu = jnp.mean(c, axis=-1, keepdims=True)
    cc = c - mu
    var = jnp.mean(cc * cc, axis=-1, keepdims=True)
    cn = cc * lax.rsqrt(var + EPS) * lg_ref[...] + lb_ref[...]
    o_ref[:, 0:CONV_WIDTH] = (cn * _sigmoid(cn)).astype(o_ref.dtype)

    rbuf[LRU_HALO:LRU_HALO + ts, :] = u_ref[:, 2 * CONV_WIDTH:2 * CONV_WIDTH + LRU_WIDTH]
    rbase = LRU_HALO - (LRU_CONV_K - 1)
    xc = jnp.broadcast_to(rb_ref[...], (ts, LRU_WIDTH))
    for k in range(LRU_CONV_K):
        xc = xc + rw_ref[k:k + 1, :] * rbuf[rbase + k:rbase + k + ts, :]
    rbuf[0:LRU_HALO, :] = rbuf[ts:ts + LRU_HALO, :]
    bbuf[...] = xc
    for h in range(LRU_HEADS):
        cols = slice(h * LRU_HEAD_DIM, (h + 1) * LRU_HEAD_DIM)
        xh = bbuf[:, cols]
        xh_b = xh.astype(BF16)
        ga = jnp.dot(xh_b, wa_ref[h], preferred_element_type=F32) + ba_ref[:, cols]
        gx = jnp.dot(xh_b, wx_ref[h], preferred_element_type=F32) + bx_ref[:, cols]
        lam = lam_ref[:, cols]
        e = jnp.exp(-jnp.abs(lam))
        e1 = 1.0 + e
        log1p_e = jnp.where(e1 == 1.0, e, jnp.log(e1) * (e / (e1 - 1.0)))
        sp = jnp.maximum(-lam, 0.0) + log1p_e
        log_a = (-LRU_C) * _sigmoid(ga) * sp
        a = jnp.exp(log_a)
        mult = jnp.sqrt(-jnp.tanh(log_a) * (1.0 + a * a))
        row = lax.broadcasted_iota(jnp.int32, (ts, LRU_HEAD_DIM), 0)
        mult = jnp.where((row == 0) & (s == 0), 1.0, mult)
        abuf[:, cols] = a
        bbuf[:, cols] = mult * (_sigmoid(gx) * xh)

    row8 = lax.broadcasted_iota(jnp.int32, (SUBLANES, LRU_WIDTH), 0)
    h_prev = hc_ref[...]
    for g in range(ts // SUBLANES):
        rows = slice(g * SUBLANES, (g + 1) * SUBLANES)
        a = abuf[rows, :]
        b = bbuf[rows, :]
        for dsh in (1, 2, 4):
            a_s = pltpu.roll(a, dsh, axis=0)
            b_s = pltpu.roll(b, dsh, axis=0)
            m = row8 >= dsh
            b = jnp.where(m, a * b_s + b, b)
            a = jnp.where(m, a * a_s, a)
        hg = a * h_prev + b
        abuf[rows, :] = hg
        h_prev = jnp.broadcast_to(hg[SUBLANES - 1:SUBLANES, :], (SUBLANES, LRU_WIDTH))
    hc_ref[...] = h_prev

    rg = u_ref[:, 2 * CONV_WIDTH + LRU_WIDTH:D_IN]
    gelu = 0.5 * rg * (1.0 + jnp.tanh(0.7978845608028654 * (rg + 0.044715 * (rg * rg * rg))))
    o_ref[:, CONV_WIDTH:CONV_WIDTH + LRU_WIDTH] = (abuf[...] * gelu).astype(o_ref.dtype)


def _mixer_core(u, bsz, seq, cw, cb, lg, lb, rw, rb, wa, ba, wx, bx, lam):
    t = u.shape[0]
    nsb = seq // TS_MIX
    row = lambda n: pl.BlockSpec((1, n), lambda b, s: (0, 0))
    return pl.pallas_call(
        _mixer_kernel,
        grid=(bsz, nsb),
        in_specs=[
            pl.BlockSpec((TS_MIX, D_IN), lambda b, s: (b * nsb + s, 0)),
            pl.BlockSpec((CONV_K, CONV_WIDTH), lambda b, s: (0, 0)),
            row(CONV_WIDTH), row(CONV_WIDTH), row(CONV_WIDTH),
            pl.BlockSpec((LRU_CONV_K, LRU_WIDTH), lambda b, s: (0, 0)),
            row(LRU_WIDTH),
            pl.BlockSpec((LRU_HEADS, LRU_HEAD_DIM, LRU_HEAD_DIM), lambda b, s: (0, 0, 0)),
            row(LRU_WIDTH),
            pl.BlockSpec((LRU_HEADS, LRU_HEAD_DIM, LRU_HEAD_DIM), lambda b, s: (0, 0, 0)),
            row(LRU_WIDTH), row(LRU_WIDTH),
        ],
        out_specs=pl.BlockSpec((TS_MIX, CONV_WIDTH + LRU_WIDTH), lambda b, s: (b * nsb + s, 0)),
        out_shape=jax.ShapeDtypeStruct((t, CONV_WIDTH + LRU_WIDTH), BF16),
        scratch_shapes=[
            pltpu.VMEM((CONV_HALO + TS_MIX, CONV_WIDTH), F32),
            pltpu.VMEM((LRU_HALO + TS_MIX, LRU_WIDTH), F32),
            pltpu.VMEM((TS_MIX, LRU_WIDTH), F32),
            pltpu.VMEM((TS_MIX, LRU_WIDTH), F32),
            pltpu.VMEM((SUBLANES, LRU_WIDTH), F32),
        ],
        compiler_params=_cparams(("parallel", "arbitrary"), 40),
        name="mixer_core",
    )(u, cw, cb, lg, lb, rw, rb, wa, ba, wx, bx, lam)


def _out_proj_kernel(x_ref, y_ref, w_ref, o_ref):
    o_ref[...] = x_ref[...] + jnp.dot(y_ref[...], w_ref[...], preferred_element_type=F32)


def _out_proj(x, y, w):
    t, d = x.shape
    k = y.shape[1]
    return pl.pallas_call(
        _out_proj_kernel,
        grid=(t // TM_PROJ,),
        in_specs=[
            pl.BlockSpec((TM_PROJ, d), lambda i: (i, 0)),
            pl.BlockSpec((TM_PROJ, k), lambda i: (i, 0)),
            pl.BlockSpec((k, d), lambda i: (0, 0)),
        ],
        out_specs=pl.BlockSpec((TM_PROJ, d), lambda i: (i, 0)),
        out_shape=jax.ShapeDtypeStruct((t, d), F32),
        compiler_params=_cparams(("parallel",), 48),
        name="out_proj",
    )(x, y, w)


def _swiglu_step(h, wg, wu, wd):
    g = jnp.dot(h, wg, preferred_element_type=F32)
    v = jnp.dot(h, wu, preferred_element_type=F32)
    act = (g * _sigmoid(g) * v).astype(BF16)
    return jnp.dot(act, wd, preferred_element_type=F32)


def _dense_ffn_kernel(x_ref, g_ref, wg_ref, wu_ref, wd_ref, o_ref, h_ref):
    @pl.when(pl.program_id(1) == 0)
    def _():
        h_ref[...] = _rms_norm(x_ref[...], g_ref[...]).astype(BF16)
        o_ref[...] = x_ref[...]

    o_ref[...] += _swiglu_step(h_ref[...], wg_ref[...], wu_ref[...], wd_ref[...])


def _dense_ffn(x, g, wg, wu, wd):
    t, d = x.shape
    ff = wg.shape[1]
    return pl.pallas_call(
        _dense_ffn_kernel,
        grid=(t // TM_FFN, ff // TF_FFN),
        in_specs=[
            pl.BlockSpec((TM_FFN, d), lambda i, f: (i, 0)),
            pl.BlockSpec((1, d), lambda i, f: (0, 0)),
            pl.BlockSpec((d, TF_FFN), lambda i, f: (0, f)),
            pl.BlockSpec((d, TF_FFN), lambda i, f: (0, f)),
            pl.BlockSpec((TF_FFN, d), lambda i, f: (f, 0)),
        ],
        out_specs=pl.BlockSpec((TM_FFN, d), lambda i, f: (i, 0)),
        out_shape=jax.ShapeDtypeStruct((t, d), F32),
        scratch_shapes=[pltpu.VMEM((TM_FFN, d), BF16)],
        compiler_params=_cparams(("parallel", "arbitrary"), 56),
        name="dense_ffn",
    )(x, g, wg, wu, wd)


def _router_kernel(x_ref, g_ref, wr_ref, o_ref):
    h = _rms_norm(x_ref[...], g_ref[...])
    logits = jnp.dot(h, wr_ref[...], preferred_element_type=F32, precision=lax.Precision.HIGHEST)
    lane = lax.broadcasted_iota(jnp.int32, logits.shape, 1)
    neg = jnp.float32(-jnp.inf)
    l1 = jnp.where(lane < N_EXPERTS, logits, neg)
    m1 = jnp.max(l1, axis=-1, keepdims=True)
    i1 = jnp.min(jnp.where(l1 == m1, lane, LANES), axis=-1, keepdims=True)
    l2 = jnp.where(lane == i1, neg, l1)
    m2 = jnp.max(l2, axis=-1, keepdims=True)
    i2 = jnp.min(jnp.where(l2 == m2, lane, LANES), axis=-1, keepdims=True)
    dlt = jnp.exp(m2 - m1)
    w1 = 1.0 / (1.0 + dlt)
    w2 = dlt / (1.0 + dlt)
    out = jnp.where(lane == 0, i1.astype(F32),
                    jnp.where(lane == 1, i2.astype(F32),
                              jnp.where(lane == 2, w1, jnp.where(lane == 3, w2, 0.0))))
    o_ref[...] = out[:, 0:o_ref.shape[1]]


def _router(x, g, wr_pad):
    t, d = x.shape
    return pl.pallas_call(
        _router_kernel,
        grid=(t // TM_ROUTER,),
        in_specs=[
            pl.BlockSpec((TM_ROUTER, d), lambda i: (i, 0)),
            pl.BlockSpec((1, d), lambda i: (0, 0)),
            pl.BlockSpec((d, LANES), lambda i: (0, 0)),
        ],
        out_specs=pl.BlockSpec((TM_ROUTER, SUBLANES), lambda i: (i, 0)),
        out_shape=jax.ShapeDtypeStruct((t, SUBLANES), F32),
        compiler_params=_cparams(("parallel",), 32),
        name="router",
    )(x, g, wr_pad)


def _moe_ffn_kernel(asg_ref, be_ref, nr_ref, x_hbm, g_ref, wg_ref, wu_ref, wd_ref, y_hbm,
                    xg_ref, h_ref, acc_ref, gsem, ssem, *, n_tok):
    b = pl.program_id(0)
    f = pl.program_id(1)
    nb = pl.num_programs(0)
    nf = pl.num_programs(1)
    n_real = nr_ref[0]
    real = b < n_real
    slot = b % 2

    def gather_issue(blk, slt):
        def body(r, carry):
            a = asg_ref[blk * MOE_BM + r]
            tok = jnp.where(a >= TOP_K * n_tok, 0, jnp.where(a >= n_tok, a - n_tok, a))
            pltpu.make_async_copy(x_hbm.at[pl.ds(tok, 1), :], xg_ref.at[slt, pl.ds(r, 1), :],
                                  gsem.at[slt]).start()
            return carry

        lax.fori_loop(0, MOE_BM, body, 0, unroll=8)

    def gather_wait(slt):
        pltpu.make_async_copy(x_hbm.at[pl.ds(0, MOE_BM), :], xg_ref.at[slt], gsem.at[slt]).wait()

    def scatter_issue(blk):
        def body(r, carry):
            a = asg_ref[blk * MOE_BM + r]
            pltpu.make_async_copy(acc_ref.at[pl.ds(r, 1), :], y_hbm.at[pl.ds(a, 1), :], ssem).start()
            return carry

        lax.fori_loop(0, MOE_BM, body, 0, unroll=8)

    def scatter_wait():
        pltpu.make_async_copy(acc_ref, y_hbm.at[pl.ds(0, MOE_BM), :], ssem).wait()

    @pl.when(real & (f == 0))
    def _():
        @pl.when(b == 0)
        def _():
            gather_issue(0, 0)
            acc_ref[...] = jnp.zeros(acc_ref.shape, acc_ref.dtype)
            spare = pltpu.make_async_copy(acc_ref, y_hbm.at[pl.ds(TOP_K * n_tok, MOE_BM), :], ssem)
            spare.start()
            spare.wait()

        gather_wait(slot)
        h_ref[...] = _rms_norm(xg_ref[slot], g_ref[...]).astype(BF16)

    @pl.when((b + 1 < n_real) & (f == 1))
    def _():
        gather_issue(b + 1, 1 - slot)

    @pl.when(real)
    def _():
        y = _swiglu_step(h_ref[...], wg_ref[0], wu_ref[0], wd_ref[0])

        @pl.when(f == 0)
        def _():
            @pl.when(b > 0)
            def _():
                scatter_wait()

            acc_ref[...] = y

        @pl.when(f > 0)
        def _():
            acc_ref[...] += y

        @pl.when(f == nf - 1)
        def _():
            scatter_issue(b)

    @pl.when((b == n_real) & (f == 0))
    def _():
        scatter_wait()

    @pl.when(real & (b == nb - 1) & (f == nf - 1))
    def _():
        scatter_wait()


def _moe_ffn(asg, block_e, n_real, x, g, wg, wu, wd):
    t, d = x.shape
    ff = wg.shape[2]
    nf = ff // TF_FFN
    n_blocks = block_e.shape[0]
    assert nf >= 2

    def f_eff(b, f, nr):
        return jnp.where(b < nr[0], f, nf - 1)

    grid_spec = pltpu.PrefetchScalarGridSpec(
        num_scalar_prefetch=3,
        grid=(n_blocks, nf),
        in_specs=[
            pl.BlockSpec(memory_space=pl.ANY),
            pl.BlockSpec((1, d), lambda b, f, asg, be, nr: (0, 0)),
            pl.BlockSpec((1, d, TF_FFN), lambda b, f, asg, be, nr: (be[b], 0, f_eff(b, f, nr))),
            pl.BlockSpec((1, d, TF_FFN), lambda b, f, asg, be, nr: (be[b], 0, f_eff(b, f, nr))),
            pl.BlockSpec((1, TF_FFN, d), lambda b, f, asg, be, nr: (be[b], f_eff(b, f, nr), 0)),
        ],
        out_specs=pl.BlockSpec(memory_space=pl.ANY),
        scratch_shapes=[
            pltpu.VMEM((2, MOE_BM, d), F32),
            pltpu.VMEM((MOE_BM, d), BF16),
            pltpu.VMEM((MOE_BM, d), F32),
            pltpu.SemaphoreType.DMA((2,)),
            pltpu.SemaphoreType.DMA(()),
        ],
    )
    return pl.pallas_call(
        functools.partial(_moe_ffn_kernel, n_tok=t),
        grid_spec=grid_spec,
        out_shape=jax.ShapeDtypeStruct((TOP_K * t + MOE_BM, d), F32),
        compiler_params=_cparams(("arbitrary", "arbitrary"), 48),
        name="moe_ffn",
    )(asg, block_e, n_real, x, g, wg, wu, wd)


def _combine_kernel(x_ref, info_ref, g_ref, ya_ref, yb_ref, o_ref, *, final_norm):
    w1 = info_ref[:, 2:3]
    w2 = info_ref[:, 3:4]
    z = x_ref[...] + (ya_ref[...] * w1 + yb_ref[...] * w2)
    o_ref[...] = _rms_norm(z, g_ref[...]) if final_norm else z


def _combine(x, info, g, y, final_norm):
    t, d = x.shape
    nt = t // TM_COMBINE
    return pl.pallas_call(
        functools.partial(_combine_kernel, final_norm=final_norm),
        grid=(nt,),
        in_specs=[
            pl.BlockSpec((TM_COMBINE, d), lambda i: (i, 0)),
            pl.BlockSpec((TM_COMBINE, SUBLANES), lambda i: (i, 0)),
            pl.BlockSpec((1, d), lambda i: (0, 0)),
            pl.BlockSpec((TM_COMBINE, d), lambda i: (i, 0)),
            pl.BlockSpec((TM_COMBINE, d), lambda i: (nt + i, 0)),
        ],
        out_specs=pl.BlockSpec((TM_COMBINE, d), lambda i: (i, 0)),
        out_shape=jax.ShapeDtypeStruct((t, d), F32),
        compiler_params=_cparams(("parallel",), 48),
        name="moe_combine",
    )(x, info, g, y, y)


def _routing_plan(info, n_tok):
    n_asg = n_tok * TOP_K
    n_blocks = n_asg // MOE_BM + N_EXPERTS
    flat_e = info[:, 0:TOP_K].astype(jnp.int32).reshape(n_asg)
    onehot = (flat_e[:, None] == jnp.arange(N_EXPERTS, dtype=jnp.int32)[None, :]).astype(jnp.int32)
    csum = jnp.cumsum(onehot, axis=0)
    counts = csum[-1]
    padded = ((counts + MOE_BM - 1) // MOE_BM) * MOE_BM
    p_ends = jnp.cumsum(padded)
    p_starts = p_ends - padded
    pos = jnp.sum(onehot * (csum - 1 + p_starts[None, :]), axis=1).astype(jnp.int32)
    flat = jnp.arange(n_asg, dtype=jnp.int32)
    dest = (flat % TOP_K) * n_tok + flat // TOP_K
    spare = n_asg + jnp.arange(n_blocks * MOE_BM, dtype=jnp.int32) % MOE_BM
    asg = spare.at[pos].set(dest)
    block_start = jnp.arange(n_blocks, dtype=jnp.int32) * MOE_BM
    block_e = jnp.minimum(jnp.searchsorted(p_ends, block_start, side="right"), N_EXPERTS - 1).astype(jnp.int32)
    n_real = (p_ends[-1] // MOE_BM).astype(jnp.int32).reshape(1)
    return asg, block_e, n_real


def kernel(x, mix_norm, w_in, conv_w, conv_b, conv_ln_g, conv_ln_b, lru_conv_w, lru_conv_b, lru_wa, lru_ba, lru_wx, lru_bx, lru_lambda, w_out, ffn_norm, dense_wg, dense_wu, dense_wd, w_router, moe_wg, moe_wu, moe_wd, final_norm):
    bsz, seq, d = x.shape
    depth = w_in.shape[0]
    t = bsz * seq
    xt = x.reshape(t, d)
    row = lambda v: v.reshape(1, -1)
    assert depth % 2 == 0, "the final RMSNorm is fused into the last routed layer's combine"
    for layer in range(depth):
        u = _in_proj(xt, row(mix_norm[layer]), w_in[layer].astype(BF16))
        y = _mixer_core(u, bsz, seq, conv_w[layer], row(conv_b[layer]), row(conv_ln_g[layer]),
                        row(conv_ln_b[layer]), lru_conv_w[layer], row(lru_conv_b[layer]),
                        lru_wa[layer].astype(BF16), row(lru_ba[layer]), lru_wx[layer].astype(BF16),
                        row(lru_bx[layer]), row(lru_lambda[layer]))
        xt = _out_proj(xt, y, w_out[layer].astype(BF16))
        j = layer // 2
        g = row(ffn_norm[layer])
        if layer % 2 == 0:
            xt = _dense_ffn(xt, g, dense_wg[j].astype(BF16), dense_wu[j].astype(BF16), dense_wd[j].astype(BF16))
        else:
            wr_pad = jnp.zeros((d, LANES), F32).at[:, 0:N_EXPERTS].set(w_router[j])
            info = _router(xt, g, wr_pad)
            asg, block_e, n_real = _routing_plan(info, t)
            y = _moe_ffn(asg, block_e, n_real, xt, g, moe_wg[j].astype(BF16),
                         moe_wu[j].astype(BF16), moe_wd[j].astype(BF16))
            xt = _combine(xt, info, row(final_norm), y, layer == depth - 1)
    return xt.reshape(bsz, seq, d)
```

```python
import functools

import jax
import jax.numpy as jnp
from jax import lax
from jax.experimental import pallas as pl
from jax.experimental.pallas import tpu as pltpu

F32 = jnp.float32
BF16 = jnp.bfloat16

D_MODEL = 2048
CONV_WIDTH = 1024
LRU_WIDTH = 1024
LRU_HEADS = 8
LRU_HEAD_DIM = LRU_WIDTH // LRU_HEADS
D_IN = 2 * CONV_WIDTH + 2 * LRU_WIDTH
CONV_K = 31
LRU_CONV_K = 4
LRU_C = 8.0
N_EXPERTS = 8
TOP_K = 2
EPS = 1e-6

LANES = 128
SUBLANES = 8
MIB = 1024 * 1024

TM_PROJ = 512
TN_IN = 1024
TS_MIX = 256
CONV_HALO = 32
LRU_HALO = 8
CONV_ROWS = 64
TM_FFN = 512
TF_FFN = 1024
TF_MOE = 768
MOE_BM = 512
TM_ROUTER = 512
TM_COMBINE = 512


def _cparams(semantics, vmem_mib):
    return pltpu.CompilerParams(dimension_semantics=semantics, vmem_limit_bytes=vmem_mib * MIB)


def _sigmoid(x):
    return 0.5 * jnp.tanh(0.5 * x) + 0.5


def _rms_norm(x, g):
    ms = jnp.mean(x * x, axis=-1, keepdims=True)
    return x * lax.rsqrt(ms + EPS) * g


def _in_proj_kernel(x_ref, g_ref, w_ref, o_ref):
    h = _rms_norm(x_ref[...], g_ref[...]).astype(BF16)
    for j in range(o_ref.shape[1] // TN_IN):
        cols = slice(j * TN_IN, (j + 1) * TN_IN)
        o_ref[:, cols] = jnp.dot(h, w_ref[:, cols], preferred_element_type=F32)


def _in_proj(x, g, w):
    t, d = x.shape
    n = w.shape[1]
    return pl.pallas_call(
        _in_proj_kernel,
        grid=(t // TM_PROJ,),
        in_specs=[
            pl.BlockSpec((TM_PROJ, d), lambda i: (i, 0)),
            pl.BlockSpec((1, d), lambda i: (0, 0)),
            pl.BlockSpec((d, n), lambda i: (0, 0), pipeline_mode=pl.Buffered(1)),
        ],
        out_specs=pl.BlockSpec((TM_PROJ, n), lambda i: (i, 0)),
        out_shape=jax.ShapeDtypeStruct((t, n), F32),
        compiler_params=_cparams(("parallel",), 56),
        name="in_proj",
    )(x, g, w)


def _causal_depthwise_conv(buf, w_ref, b_ref, out_ref, ts, halo, n_taps):
    base = halo - (n_taps - 1)
    nt = CONV_ROWS // SUBLANES
    nq = (base + n_taps - 1) // SUBLANES + 1
    row8 = lax.broadcasted_iota(jnp.int32, (SUBLANES, LANES), 0)
    for rc in range(ts // CONV_ROWS):
        t0 = rc * CONV_ROWS
        for lc in range(buf.shape[1] // LANES):
            cols = slice(lc * LANES, (lc + 1) * LANES)
            xs = [buf[t0 + SUBLANES * j:t0 + SUBLANES * (j + 1), cols] for j in range(nt + nq - 1)]
            bias = jnp.broadcast_to(b_ref[:, cols], (SUBLANES, LANES))
            out = [bias] * nt
            for p in range(SUBLANES):
                taps = [(q, SUBLANES * q + p - base) for q in range(nq)
                        if 0 <= SUBLANES * q + p - base < n_taps]
                if not taps:
                    continue
                ws = [jnp.broadcast_to(w_ref[k:k + 1, cols], (SUBLANES, LANES)) for _, k in taps]
                zs = []
                for j in range(nt + (1 if p else 0)):
                    z = ws[0] * xs[j + taps[0][0]]
                    for w, (q, _) in zip(ws[1:], taps[1:]):
                        z = z + w * xs[j + q]
                    zs.append(z)
                if p == 0:
                    out = [o + z for o, z in zip(out, zs)]
                else:
                    rs = [pltpu.roll(z, SUBLANES - p, axis=0) for z in zs]
                    keep = row8 < SUBLANES - p
                    out = [o + jnp.where(keep, rs[g], rs[g + 1]) for g, o in enumerate(out)]
            for g in range(nt):
                out_ref[t0 + SUBLANES * g:t0 + SUBLANES * (g + 1), cols] = out[g]


def _mixer_kernel(u_ref, cw_ref, cb_ref, lg_ref, lb_ref, rw_ref, rb_ref, wa_ref, ba_ref,
                  wx_ref, bx_ref, lam_ref, o_ref, cbuf, rbuf, abuf, bbuf, hc_ref):
    s = pl.program_id(1)
    ts = TS_MIX

    @pl.when(s == 0)
    def _():
        cbuf[0:CONV_HALO, :] = jnp.zeros((CONV_HALO, CONV_WIDTH), F32)
        rbuf[0:LRU_HALO, :] = jnp.zeros((LRU_HALO, LRU_WIDTH), F32)
        hc_ref[...] = jnp.zeros((SUBLANES, LRU_WIDTH), F32)

    val = u_ref[:, 0:CONV_WIDTH]
    gate = u_ref[:, CONV_WIDTH:2 * CONV_WIDTH]
    cbuf[CONV_HALO:CONV_HALO + ts, :] = val * _sigmoid(gate)
    _causal_depthwise_conv(cbuf, cw_ref, cb_ref, abuf, ts, CONV_HALO, CONV_K)
    cbuf[0:CONV_HALO, :] = cbuf[ts:ts + CONV_HALO, :]
    c = abuf[...]
    mu = jnp.mean(c, axis=-1, keepdims=True)
    cc = c - mu
    var = jnp.mean(cc * cc, axis=-1, keepdims=True)
    cn = cc * lax.rsqrt(var + EPS) * lg_ref[...] + lb_ref[...]
    o_ref[:, 0:CONV_WIDTH] = (cn * _sigmoid(cn)).astype(o_ref.dtype)

    rbuf[LRU_HALO:LRU_HALO + ts, :] = u_ref[:, 2 * CONV_WIDTH:2 * CONV_WIDTH + LRU_WIDTH]
    _causal_depthwise_conv(rbuf, rw_ref, rb_ref, bbuf, ts, LRU_HALO, LRU_CONV_K)
    rbuf[0:LRU_HALO, :] = rbuf[ts:ts + LRU_HALO, :]
    for h in range(LRU_HEADS):
        cols = slice(h * LRU_HEAD_DIM, (h + 1) * LRU_HEAD_DIM)
        xh = bbuf[:, cols]
        xh_b = xh.astype(BF16)
        ga = jnp.dot(xh_b, wa_ref[h], preferred_element_type=F32) + ba_ref[:, cols]
        gx = jnp.dot(xh_b, wx_ref[h], preferred_element_type=F32) + bx_ref[:, cols]
        lam = lam_ref[:, cols]
        e = jnp.exp(-jnp.abs(lam))
        e1 = 1.0 + e
        log1p_e = jnp.where(e1 == 1.0, e, jnp.log(e1) * (e / (e1 - 1.0)))
        sp = jnp.maximum(-lam, 0.0) + log1p_e
        log_a = (-LRU_C) * _sigmoid(ga) * sp
        a = jnp.exp(log_a)
        mult = jnp.sqrt(-jnp.tanh(log_a) * (1.0 + a * a))
        row = lax.broadcasted_iota(jnp.int32, (ts, LRU_HEAD_DIM), 0)
        mult = jnp.where((row == 0) & (s == 0), 1.0, mult)
        abuf[:, cols] = a
        bbuf[:, cols] = mult * (_sigmoid(gx) * xh)

    row8 = lax.broadcasted_iota(jnp.int32, (SUBLANES, LRU_WIDTH), 0)
    h_prev = hc_ref[...]
    for g in range(ts // SUBLANES):
        rows = slice(g * SUBLANES, (g + 1) * SUBLANES)
        a = abuf[rows, :]
        b = bbuf[rows, :]
        for dsh in (1, 2, 4):
            a_s = pltpu.roll(a, dsh, axis=0)
            b_s = pltpu.roll(b, dsh, axis=0)
            m = row8 >= dsh
            b = jnp.where(m, a * b_s + b, b)
            a = jnp.where(m, a * a_s, a)
        hg = a * h_prev + b
        abuf[rows, :] = hg
        h_prev = jnp.broadcast_to(hg[SUBLANES - 1:SUBLANES, :], (SUBLANES, LRU_WIDTH))
    hc_ref[...] = h_prev

    rg = u_ref[:, 2 * CONV_WIDTH + LRU_WIDTH:D_IN]
    gelu = 0.5 * rg * (1.0 + jnp.tanh(0.7978845608028654 * (rg + 0.044715 * (rg * rg * rg))))
    o_ref[:, CONV_WIDTH:CONV_WIDTH + LRU_WIDTH] = (abuf[...] * gelu).astype(o_ref.dtype)


def _mixer_core(u, bsz, seq, cw, cb, lg, lb, rw, rb, wa, ba, wx, bx, lam):
    t = u.shape[0]
    nsb = seq // TS_MIX
    row = lambda n: pl.BlockSpec((1, n), lambda b, s: (0, 0))
    return pl.pallas_call(
        _mixer_kernel,
        grid=(bsz, nsb),
        in_specs=[
            pl.BlockSpec((TS_MIX, D_IN), lambda b, s: (b * nsb + s, 0)),
            pl.BlockSpec((CONV_K, CONV_WIDTH), lambda b, s: (0, 0)),
            row(CONV_WIDTH), row(CONV_WIDTH), row(CONV_WIDTH),
            pl.BlockSpec((LRU_CONV_K, LRU_WIDTH), lambda b, s: (0, 0)),
            row(LRU_WIDTH),
            pl.BlockSpec((LRU_HEADS, LRU_HEAD_DIM, LRU_HEAD_DIM), lambda b, s: (0, 0, 0)),
            row(LRU_WIDTH),
            pl.BlockSpec((LRU_HEADS, LRU_HEAD_DIM, LRU_HEAD_DIM), lambda b, s: (0, 0, 0)),
            row(LRU_WIDTH), row(LRU_WIDTH),
        ],
        out_specs=pl.BlockSpec((TS_MIX, CONV_WIDTH + LRU_WIDTH), lambda b, s: (b * nsb + s, 0)),
        out_shape=jax.ShapeDtypeStruct((t, CONV_WIDTH + LRU_WIDTH), BF16),
        scratch_shapes=[
            pltpu.VMEM((CONV_HALO + TS_MIX, CONV_WIDTH), F32),
            pltpu.VMEM((LRU_HALO + TS_MIX, LRU_WIDTH), F32),
            pltpu.VMEM((TS_MIX, LRU_WIDTH), F32),
            pltpu.VMEM((TS_MIX, LRU_WIDTH), F32),
            pltpu.VMEM((SUBLANES, LRU_WIDTH), F32),
        ],
        compiler_params=_cparams(("parallel", "arbitrary"), 40),
        name="mixer_core",
    )(u, cw, cb, lg, lb, rw, rb, wa, ba, wx, bx, lam)


def _out_proj_kernel(x_ref, y_ref, w_ref, o_ref):
    o_ref[...] = x_ref[...] + jnp.dot(y_ref[...], w_ref[...], preferred_element_type=F32)


def _out_proj(x, y, w):
    t, d = x.shape
    k = y.shape[1]
    return pl.pallas_call(
        _out_proj_kernel,
        grid=(t // TM_PROJ,),
        in_specs=[
            pl.BlockSpec((TM_PROJ, d), lambda i: (i, 0)),
            pl.BlockSpec((TM_PROJ, k), lambda i: (i, 0)),
            pl.BlockSpec((k, d), lambda i: (0, 0)),
        ],
        out_specs=pl.BlockSpec((TM_PROJ, d), lambda i: (i, 0)),
        out_shape=jax.ShapeDtypeStruct((t, d), F32),
        compiler_params=_cparams(("parallel",), 48),
        name="out_proj",
    )(x, y, w)


def _swiglu_step(h, wg, wu, wd):
    g = jnp.dot(h, wg, preferred_element_type=F32)
    v = jnp.dot(h, wu, preferred_element_type=F32)
    act = (g * _sigmoid(g) * v).astype(BF16)
    return jnp.dot(act, wd, preferred_element_type=F32)


def _dense_ffn_kernel(x_ref, g_ref, wg_ref, wu_ref, wd_ref, o_ref, h_ref):
    @pl.when(pl.program_id(1) == 0)
    def _():
        h_ref[...] = _rms_norm(x_ref[...], g_ref[...]).astype(BF16)
        o_ref[...] = x_ref[...]

    o_ref[...] += _swiglu_step(h_ref[...], wg_ref[...], wu_ref[...], wd_ref[...])


def _dense_ffn(x, g, wg, wu, wd):
    t, d = x.shape
    ff = wg.shape[1]
    return pl.pallas_call(
        _dense_ffn_kernel,
        grid=(t // TM_FFN, ff // TF_FFN),
        in_specs=[
            pl.BlockSpec((TM_FFN, d), lambda i, f: (i, 0)),
            pl.BlockSpec((1, d), lambda i, f: (0, 0)),
            pl.BlockSpec((d, TF_FFN), lambda i, f: (0, f)),
            pl.BlockSpec((d, TF_FFN), lambda i, f: (0, f)),
            pl.BlockSpec((TF_FFN, d), lambda i, f: (f, 0)),
        ],
        out_specs=pl.BlockSpec((TM_FFN, d), lambda i, f: (i, 0)),
        out_shape=jax.ShapeDtypeStruct((t, d), F32),
        scratch_shapes=[pltpu.VMEM((TM_FFN, d), BF16)],
        compiler_params=_cparams(("parallel", "arbitrary"), 56),
        name="dense_ffn",
    )(x, g, wg, wu, wd)


def _router_kernel(x_ref, g_ref, wr_ref, o_ref):
    h = _rms_norm(x_ref[...], g_ref[...])
    logits = jnp.dot(h, wr_ref[...], preferred_element_type=F32, precision=lax.Precision.HIGHEST)
    lane = lax.broadcasted_iota(jnp.int32, logits.shape, 1)
    neg = jnp.float32(-jnp.inf)
    l1 = jnp.where(lane < N_EXPERTS, logits, neg)
    m1 = jnp.max(l1, axis=-1, keepdims=True)
    i1 = jnp.min(jnp.where(l1 == m1, lane, LANES), axis=-1, keepdims=True)
    l2 = jnp.where(lane == i1, neg, l1)
    m2 = jnp.max(l2, axis=-1, keepdims=True)
    i2 = jnp.min(jnp.where(l2 == m2, lane, LANES), axis=-1, keepdims=True)
    dlt = jnp.exp(m2 - m1)
    w1 = 1.0 / (1.0 + dlt)
    w2 = dlt / (1.0 + dlt)
    out = jnp.where(lane == 0, i1.astype(F32),
                    jnp.where(lane == 1, i2.astype(F32),
                              jnp.where(lane == 2, w1, jnp.where(lane == 3, w2, 0.0))))
    o_ref[...] = out[:, 0:o_ref.shape[1]]


def _router(x, g, wr_pad):
    t, d = x.shape
    return pl.pallas_call(
        _router_kernel,
        grid=(t // TM_ROUTER,),
        in_specs=[
            pl.BlockSpec((TM_ROUTER, d), lambda i: (i, 0)),
            pl.BlockSpec((1, d), lambda i: (0, 0)),
            pl.BlockSpec((d, LANES), lambda i: (0, 0)),
        ],
        out_specs=pl.BlockSpec((TM_ROUTER, SUBLANES), lambda i: (i, 0)),
        out_shape=jax.ShapeDtypeStruct((t, SUBLANES), F32),
        compiler_params=_cparams(("parallel",), 32),
        name="router",
    )(x, g, wr_pad)


def _moe_ffn_kernel(asg_ref, be_ref, nr_ref, x_hbm, g_ref, wg_ref, wu_ref, wd_ref, y_hbm,
                    xg_ref, h_ref, acc_ref, gsem, ssem, *, n_tok):
    b = pl.program_id(0)
    f = pl.program_id(1)
    nb = pl.num_programs(0)
    nf = pl.num_programs(1)
    n_real = nr_ref[0]
    real = b < n_real
    slot = b % 2
    rows_per_step = MOE_BM // nf

    def token_of(a):
        return jnp.where(a >= TOP_K * n_tok, 0, jnp.where(a >= n_tok, a - n_tok, a))

    def gather_start(a, r, slt):
        pltpu.make_async_copy(x_hbm.at[pl.ds(token_of(a), 1), :], xg_ref.at[slt, pl.ds(r, 1), :],
                              gsem.at[slt]).start()

    def scatter_start(a, r, slt):
        pltpu.make_async_copy(acc_ref.at[slt, pl.ds(r, 1), :], y_hbm.at[pl.ds(a, 1), :], ssem).start()

    def gather_wait(slt):
        pltpu.make_async_copy(x_hbm.at[pl.ds(0, MOE_BM), :], xg_ref.at[slt], gsem.at[slt]).wait()

    def scatter_wait():
        pltpu.make_async_copy(acc_ref.at[0], y_hbm.at[pl.ds(0, MOE_BM), :], ssem).wait()

    def drain(blk):
        gather_wait((blk + 1) % 2)

        def body(r, carry):
            scatter_start(asg_ref[blk * MOE_BM + r], r, blk % 2)
            return carry

        lax.fori_loop(0, MOE_BM, body, 0, unroll=8)
        scatter_wait()

    @pl.when(real & (f == 0))
    def _():
        @pl.when(b == 0)
        def _():
            def body(r, carry):
                gather_start(asg_ref[r], r, 0)
                return carry

            lax.fori_loop(0, MOE_BM, body, 0, unroll=8)
            acc_ref[1] = jnp.zeros(acc_ref.shape[1:], acc_ref.dtype)

        gather_wait(slot)
        h_ref[...] = _rms_norm(xg_ref[slot], g_ref[...]).astype(BF16)

    @pl.when(real)
    def _():
        prev_base = jnp.maximum(b - 1, 0) * MOE_BM
        for i in range(rows_per_step):
            r = f * rows_per_step + i
            gather_start(asg_ref[(b + 1) * MOE_BM + r], r, 1 - slot)
            a_prev = jnp.where(b == 0, TOP_K * n_tok + r, asg_ref[prev_base + r])
            scatter_start(a_prev, r, 1 - slot)
        y = _swiglu_step(h_ref[...], wg_ref[0], wu_ref[0], wd_ref[0])

        @pl.when(f == 0)
        def _():
            acc_ref[slot] = y

        @pl.when(f > 0)
        def _():
            acc_ref[slot] += y

        @pl.when(f == nf - 1)
        def _():
            scatter_wait()

    @pl.when((b == n_real) & (f == 0))
    def _():
        drain(b - 1)

    @pl.when(real & (b == nb - 1) & (f == nf - 1))
    def _():
        drain(b)


def _moe_ffn(asg, block_e, n_real, x, g, wg, wu, wd):
    t, d = x.shape
    ff = wg.shape[2]
    nf = ff // TF_MOE
    n_blocks = block_e.shape[0]
    assert MOE_BM % nf == 0 and asg.shape[0] == (n_blocks + 1) * MOE_BM

    def f_eff(b, f, nr):
        return jnp.where(b < nr[0], f, nf - 1)

    grid_spec = pltpu.PrefetchScalarGridSpec(
        num_scalar_prefetch=3,
        grid=(n_blocks, nf),
        in_specs=[
            pl.BlockSpec(memory_space=pl.ANY),
            pl.BlockSpec((1, d), lambda b, f, asg, be, nr: (0, 0)),
            pl.BlockSpec((1, d, TF_MOE), lambda b, f, asg, be, nr: (be[b], 0, f_eff(b, f, nr))),
            pl.BlockSpec((1, d, TF_MOE), lambda b, f, asg, be, nr: (be[b], 0, f_eff(b, f, nr))),
            pl.BlockSpec((1, TF_MOE, d), lambda b, f, asg, be, nr: (be[b], f_eff(b, f, nr), 0)),
        ],
        out_specs=pl.BlockSpec(memory_space=pl.ANY),
        scratch_shapes=[
            pltpu.VMEM((2, MOE_BM, d), F32),
            pltpu.VMEM((MOE_BM, d), BF16),
            pltpu.VMEM((2, MOE_BM, d), F32),
            pltpu.SemaphoreType.DMA((2,)),
            pltpu.SemaphoreType.DMA(()),
        ],
    )
    return pl.pallas_call(
        functools.partial(_moe_ffn_kernel, n_tok=t),
        grid_spec=grid_spec,
        out_shape=jax.ShapeDtypeStruct((TOP_K * t + MOE_BM, d), F32),
        compiler_params=_cparams(("arbitrary", "arbitrary"), 56),
        name="moe_ffn",
    )(asg, block_e, n_real, x, g, wg, wu, wd)


def _combine_kernel(x_ref, info_ref, g_ref, ya_ref, yb_ref, o_ref, *, final_norm):
    w1 = info_ref[:, 2:3]
    w2 = info_ref[:, 3:4]
    z = x_ref[...] + (ya_ref[...] * w1 + yb_ref[...] * w2)
    o_ref[...] = _rms_norm(z, g_ref[...]) if final_norm else z


def _combine(x, info, g, y, final_norm):
    t, d = x.shape
    nt = t // TM_COMBINE
    return pl.pallas_call(
        functools.partial(_combine_kernel, final_norm=final_norm),
        grid=(nt,),
        in_specs=[
            pl.BlockSpec((TM_COMBINE, d), lambda i: (i, 0)),
            pl.BlockSpec((TM_COMBINE, SUBLANES), lambda i: (i, 0)),
            pl.BlockSpec((1, d), lambda i: (0, 0)),
            pl.BlockSpec((TM_COMBINE, d), lambda i: (i, 0)),
            pl.BlockSpec((TM_COMBINE, d), lambda i: (nt + i, 0)),
        ],
        out_specs=pl.BlockSpec((TM_COMBINE, d), lambda i: (i, 0)),
        out_shape=jax.ShapeDtypeStruct((t, d), F32),
        compiler_params=_cparams(("parallel",), 48),
        name="moe_combine",
    )(x, info, g, y, y)


def _routing_plan(info, n_tok):
    n_asg = n_tok * TOP_K
    n_blocks = n_asg // MOE_BM + N_EXPERTS
    flat_e = info[:, 0:TOP_K].astype(jnp.int32).reshape(n_asg)
    onehot = (flat_e[:, None] == jnp.arange(N_EXPERTS, dtype=jnp.int32)[None, :]).astype(jnp.int32)
    csum = jnp.cumsum(onehot, axis=0)
    counts = csum[-1]
    padded = ((counts + MOE_BM - 1) // MOE_BM) * MOE_BM
    p_ends = jnp.cumsum(padded)
    p_starts = p_ends - padded
    pos = jnp.sum(onehot * (csum - 1 + p_starts[None, :]), axis=1).astype(jnp.int32)
    flat = jnp.arange(n_asg, dtype=jnp.int32)
    dest = (flat % TOP_K) * n_tok + flat // TOP_K
    spare = n_asg + jnp.arange((n_blocks + 1) * MOE_BM, dtype=jnp.int32) % MOE_BM
    asg = spare.at[pos].set(dest)
    block_start = jnp.arange(n_blocks, dtype=jnp.int32) * MOE_BM
    block_e = jnp.minimum(jnp.searchsorted(p_ends, block_start, side="right"), N_EXPERTS - 1).astype(jnp.int32)
    n_real = (p_ends[-1] // MOE_BM).astype(jnp.int32).reshape(1)
    return asg, block_e, n_real


def kernel(x, mix_norm, w_in, conv_w, conv_b, conv_ln_g, conv_ln_b, lru_conv_w, lru_conv_b, lru_wa, lru_ba, lru_wx, lru_bx, lru_lambda, w_out, ffn_norm, dense_wg, dense_wu, dense_wd, w_router, moe_wg, moe_wu, moe_wd, final_norm):
    bsz, seq, d = x.shape
    depth = w_in.shape[0]
    t = bsz * seq
    xt = x.reshape(t, d)
    row = lambda v: v.reshape(1, -1)
    assert depth % 2 == 0, "the final RMSNorm is fused into the last routed layer's combine"
    for layer in range(depth):
        u = _in_proj(xt, row(mix_norm[layer]), w_in[layer].astype(BF16))
        y = _mixer_core(u, bsz, seq, conv_w[layer], row(conv_b[layer]), row(conv_ln_g[layer]),
                        row(conv_ln_b[layer]), lru_conv_w[layer], row(lru_conv_b[layer]),
                        lru_wa[layer].astype(BF16), row(lru_ba[layer]), lru_wx[layer].astype(BF16),
                        row(lru_bx[layer]), row(lru_lambda[layer]))
        xt = _out_proj(xt, y, w_out[layer].astype(BF16))
        j = layer // 2
        g = row(ffn_norm[layer])
        if layer % 2 == 0:
            xt = _dense_ffn(xt, g, dense_wg[j].astype(BF16), dense_wu[j].astype(BF16), dense_wd[j].astype(BF16))
        else:
            wr_pad = jnp.zeros((d, LANES), F32).at[:, 0:N_EXPERTS].set(w_router[j])
            info = _router(xt, g, wr_pad)
            asg, block_e, n_real = _routing_plan(info, t)
            y = _moe_ffn(asg, block_e, n_real, xt, g, moe_wg[j].astype(BF16),
                         moe_wu[j].astype(BF16), moe_wd[j].astype(BF16))
            xt = _combine(xt, info, row(final_norm), y, layer == depth - 1)
    return xt.reshape(bsz, seq, d)
```

```python
import functools

import jax
import jax.numpy as jnp
from jax import lax
from jax.experimental import pallas as pl
from jax.experimental.pallas import tpu as pltpu

F32 = jnp.float32
BF16 = jnp.bfloat16

D_MODEL = 2048
CONV_WIDTH = 1024
LRU_WIDTH = 1024
LRU_HEADS = 8
LRU_HEAD_DIM = LRU_WIDTH // LRU_HEADS
D_IN = 2 * CONV_WIDTH + 2 * LRU_WIDTH
CONV_K = 31
LRU_CONV_K = 4
LRU_C = 8.0
N_EXPERTS = 8
TOP_K = 2
EPS = 1e-6

LANES = 128
SUBLANES = 8
MIB = 1024 * 1024

TM_PROJ = 512
TN_IN = 1024
TS_MIX = 256
CONV_HALO = 32
LRU_HALO = 8
CONV_ROWS = 64
TM_FFN = 512
TF_FFN = 1024
TF_MOE = 768
MOE_BM = 512
TM_ROUTER = 512
TM_COMBINE = 512


def _cparams(semantics, vmem_mib):
    return pltpu.CompilerParams(dimension_semantics=semantics, vmem_limit_bytes=vmem_mib * MIB)


def _sigmoid(x):
    return 0.5 * jnp.tanh(0.5 * x) + 0.5


def _rms_norm(x, g):
    ms = jnp.mean(x * x, axis=-1, keepdims=True)
    return x * lax.rsqrt(ms + EPS) * g


def _in_proj_kernel(x_ref, g_ref, w_ref, o_ref):
    h = _rms_norm(x_ref[...], g_ref[...]).astype(BF16)
    for j in range(o_ref.shape[1] // TN_IN):
        cols = slice(j * TN_IN, (j + 1) * TN_IN)
        o_ref[:, cols] = jnp.dot(h, w_ref[:, cols], preferred_element_type=F32)


def _in_proj(x, g, w):
    t, d = x.shape
    n = w.shape[1]
    return pl.pallas_call(
        _in_proj_kernel,
        grid=(t // TM_PROJ,),
        in_specs=[
            pl.BlockSpec((TM_PROJ, d), lambda i: (i, 0)),
            pl.BlockSpec((1, d), lambda i: (0, 0)),
            pl.BlockSpec((d, n), lambda i: (0, 0), pipeline_mode=pl.Buffered(1)),
        ],
        out_specs=pl.BlockSpec((TM_PROJ, n), lambda i: (i, 0)),
        out_shape=jax.ShapeDtypeStruct((t, n), F32),
        compiler_params=_cparams(("parallel",), 56),
        name="in_proj",
    )(x, g, w)


def _causal_depthwise_conv(buf, w_ref, b_ref, out_ref, ts, halo, n_taps):
    base = halo - (n_taps - 1)
    nt = CONV_ROWS // SUBLANES
    nq = (base + n_taps - 1) // SUBLANES + 1
    row8 = lax.broadcasted_iota(jnp.int32, (SUBLANES, LANES), 0)
    for rc in range(ts // CONV_ROWS):
        t0 = rc * CONV_ROWS
        for lc in range(buf.shape[1] // LANES):
            cols = slice(lc * LANES, (lc + 1) * LANES)
            xs = [buf[t0 + SUBLANES * j:t0 + SUBLANES * (j + 1), cols] for j in range(nt + nq - 1)]
            bias = jnp.broadcast_to(b_ref[:, cols], (SUBLANES, LANES))
            out = [bias] * nt
            for p in range(SUBLANES):
                taps = [(q, SUBLANES * q + p - base) for q in range(nq)
                        if 0 <= SUBLANES * q + p - base < n_taps]
                if not taps:
                    continue
                ws = [jnp.broadcast_to(w_ref[k:k + 1, cols], (SUBLANES, LANES)) for _, k in taps]
                zs = []
                for j in range(nt + (1 if p else 0)):
                    z = ws[0] * xs[j + taps[0][0]]
                    for w, (q, _) in zip(ws[1:], taps[1:]):
                        z = z + w * xs[j + q]
                    zs.append(z)
                if p == 0:
                    out = [o + z for o, z in zip(out, zs)]
                else:
                    rs = [pltpu.roll(z, SUBLANES - p, axis=0) for z in zs]
                    keep = row8 < SUBLANES - p
                    out = [o + jnp.where(keep, rs[g], rs[g + 1]) for g, o in enumerate(out)]
            for g in range(nt):
                out_ref[t0 + SUBLANES * g:t0 + SUBLANES * (g + 1), cols] = out[g]


def _mixer_kernel(u_ref, cw_ref, cb_ref, lg_ref, lb_ref, rw_ref, rb_ref, wa_ref, ba_ref,
                  wx_ref, bx_ref, lam_ref, cast_ref, o_ref, cast_o_ref, cbuf, rbuf, abuf, bbuf, hc_ref):
    s = pl.program_id(1)
    ts = TS_MIX
    cast_o_ref[...] = cast_ref[...].astype(cast_o_ref.dtype)

    @pl.when(s == 0)
    def _():
        cbuf[0:CONV_HALO, :] = jnp.zeros((CONV_HALO, CONV_WIDTH), F32)
        rbuf[0:LRU_HALO, :] = jnp.zeros((LRU_HALO, LRU_WIDTH), F32)
        hc_ref[...] = jnp.zeros((SUBLANES, LRU_WIDTH), F32)

    val = u_ref[:, 0:CONV_WIDTH]
    gate = u_ref[:, CONV_WIDTH:2 * CONV_WIDTH]
    cbuf[CONV_HALO:CONV_HALO + ts, :] = val * _sigmoid(gate)
    _causal_depthwise_conv(cbuf, cw_ref, cb_ref, abuf, ts, CONV_HALO, CONV_K)
    cbuf[0:CONV_HALO, :] = cbuf[ts:ts + CONV_HALO, :]
    c = abuf[...]
    mu = jnp.mean(c, axis=-1, keepdims=True)
    cc = c - mu
    var = jnp.mean(cc * cc, axis=-1, keepdims=True)
    cn = cc * lax.rsqrt(var + EPS) * lg_ref[...] + lb_ref[...]
    o_ref[:, 0:CONV_WIDTH] = (cn * _sigmoid(cn)).astype(o_ref.dtype)

    rbuf[LRU_HALO:LRU_HALO + ts, :] = u_ref[:, 2 * CONV_WIDTH:2 * CONV_WIDTH + LRU_WIDTH]
    _causal_depthwise_conv(rbuf, rw_ref, rb_ref, bbuf, ts, LRU_HALO, LRU_CONV_K)
    rbuf[0:LRU_HALO, :] = rbuf[ts:ts + LRU_HALO, :]
    for h in range(LRU_HEADS):
        cols = slice(h * LRU_HEAD_DIM, (h + 1) * LRU_HEAD_DIM)
        xh = bbuf[:, cols]
        xh_b = xh.astype(BF16)
        ga = jnp.dot(xh_b, wa_ref[h], preferred_element_type=F32) + ba_ref[:, cols]
        gx = jnp.dot(xh_b, wx_ref[h], preferred_element_type=F32) + bx_ref[:, cols]
        lam = lam_ref[:, cols]
        e = jnp.exp(-jnp.abs(lam))
        e1 = 1.0 + e
        log1p_e = jnp.where(e1 == 1.0, e, jnp.log(e1) * (e / (e1 - 1.0)))
        sp = jnp.maximum(-lam, 0.0) + log1p_e
        log_a = (-LRU_C) * _sigmoid(ga) * sp
        a = jnp.exp(log_a)
        mult = jnp.sqrt(-jnp.tanh(log_a) * (1.0 + a * a))
        row = lax.broadcasted_iota(jnp.int32, (ts, LRU_HEAD_DIM), 0)
        mult = jnp.where((row == 0) & (s == 0), 1.0, mult)
        abuf[:, cols] = a
        bbuf[:, cols] = mult * (_sigmoid(gx) * xh)

    row8 = lax.broadcasted_iota(jnp.int32, (SUBLANES, LRU_WIDTH), 0)
    h_prev = hc_ref[...]
    for g in range(ts // SUBLANES):
        rows = slice(g * SUBLANES, (g + 1) * SUBLANES)
        a = abuf[rows, :]
        b = bbuf[rows, :]
        for dsh in (1, 2, 4):
            a_s = pltpu.roll(a, dsh, axis=0)
            b_s = pltpu.roll(b, dsh, axis=0)
            m = row8 >= dsh
            b = jnp.where(m, a * b_s + b, b)
            a = jnp.where(m, a * a_s, a)
        hg = a * h_prev + b
        abuf[rows, :] = hg
        h_prev = jnp.broadcast_to(hg[SUBLANES - 1:SUBLANES, :], (SUBLANES, LRU_WIDTH))
    hc_ref[...] = h_prev

    rg = u_ref[:, 2 * CONV_WIDTH + LRU_WIDTH:D_IN]
    gelu = 0.5 * rg * (1.0 + jnp.tanh(0.7978845608028654 * (rg + 0.044715 * (rg * rg * rg))))
    o_ref[:, CONV_WIDTH:CONV_WIDTH + LRU_WIDTH] = (abuf[...] * gelu).astype(o_ref.dtype)


def _mixer_core(u, bsz, seq, cw, cb, lg, lb, rw, rb, wa, ba, wx, bx, lam, w_cast):
    t = u.shape[0]
    nsb = seq // TS_MIX
    row = lambda n: pl.BlockSpec((1, n), lambda b, s: (0, 0))
    cast_rows, cast_cols = w_cast.shape
    cast_spec = pl.BlockSpec((cast_rows // (bsz * nsb), cast_cols), lambda b, s: (b * nsb + s, 0))
    return pl.pallas_call(
        _mixer_kernel,
        grid=(bsz, nsb),
        in_specs=[
            pl.BlockSpec((TS_MIX, D_IN), lambda b, s: (b * nsb + s, 0)),
            pl.BlockSpec((CONV_K, CONV_WIDTH), lambda b, s: (0, 0)),
            row(CONV_WIDTH), row(CONV_WIDTH), row(CONV_WIDTH),
            pl.BlockSpec((LRU_CONV_K, LRU_WIDTH), lambda b, s: (0, 0)),
            row(LRU_WIDTH),
            pl.BlockSpec((LRU_HEADS, LRU_HEAD_DIM, LRU_HEAD_DIM), lambda b, s: (0, 0, 0)),
            row(LRU_WIDTH),
            pl.BlockSpec((LRU_HEADS, LRU_HEAD_DIM, LRU_HEAD_DIM), lambda b, s: (0, 0, 0)),
            row(LRU_WIDTH), row(LRU_WIDTH),
            cast_spec,
        ],
        out_specs=[pl.BlockSpec((TS_MIX, CONV_WIDTH + LRU_WIDTH), lambda b, s: (b * nsb + s, 0)), cast_spec],
        out_shape=[jax.ShapeDtypeStruct((t, CONV_WIDTH + LRU_WIDTH), BF16),
                   jax.ShapeDtypeStruct(w_cast.shape, BF16)],
        scratch_shapes=[
            pltpu.VMEM((CONV_HALO + TS_MIX, CONV_WIDTH), F32),
            pltpu.VMEM((LRU_HALO + TS_MIX, LRU_WIDTH), F32),
            pltpu.VMEM((TS_MIX, LRU_WIDTH), F32),
            pltpu.VMEM((TS_MIX, LRU_WIDTH), F32),
            pltpu.VMEM((SUBLANES, LRU_WIDTH), F32),
        ],
        compiler_params=_cparams(("parallel", "arbitrary"), 56),
        name="mixer_core",
    )(u, cw, cb, lg, lb, rw, rb, wa, ba, wx, bx, lam, w_cast)


def _out_proj_kernel(x_ref, y_ref, w_ref, o_ref):
    o_ref[...] = x_ref[...] + jnp.dot(y_ref[...], w_ref[...], preferred_element_type=F32)


def _out_proj(x, y, w):
    t, d = x.shape
    k = y.shape[1]
    return pl.pallas_call(
        _out_proj_kernel,
        grid=(t // TM_PROJ,),
        in_specs=[
            pl.BlockSpec((TM_PROJ, d), lambda i: (i, 0)),
            pl.BlockSpec((TM_PROJ, k), lambda i: (i, 0)),
            pl.BlockSpec((k, d), lambda i: (0, 0)),
        ],
        out_specs=pl.BlockSpec((TM_PROJ, d), lambda i: (i, 0)),
        out_shape=jax.ShapeDtypeStruct((t, d), F32),
        compiler_params=_cparams(("parallel",), 48),
        name="out_proj",
    )(x, y, w)


def _swiglu_step(h, wg, wu, wd):
    g = jnp.dot(h, wg, preferred_element_type=F32)
    v = jnp.dot(h, wu, preferred_element_type=F32)
    act = (g * _sigmoid(g) * v).astype(BF16)
    return jnp.dot(act, wd, preferred_element_type=F32)


def _dense_ffn_kernel(x_ref, g_ref, wg_ref, wu_ref, wd_ref, cast_ref, o_ref, cast_o_ref, h_ref):
    cast_o_ref[...] = cast_ref[...].astype(cast_o_ref.dtype)

    @pl.when(pl.program_id(1) == 0)
    def _():
        h_ref[...] = _rms_norm(x_ref[...], g_ref[...]).astype(BF16)
        o_ref[...] = x_ref[...]

    o_ref[...] += _swiglu_step(h_ref[...], wg_ref[...], wu_ref[...], wd_ref[...])


def _dense_ffn(x, g, wg, wu, wd, w_cast):
    t, d = x.shape
    ff = wg.shape[1]
    nf = ff // TF_FFN
    cast_rows, cast_cols = w_cast.shape
    cast_spec = pl.BlockSpec((cast_rows // (t // TM_FFN * nf), cast_cols), lambda i, f: (i * nf + f, 0))
    return pl.pallas_call(
        _dense_ffn_kernel,
        grid=(t // TM_FFN, ff // TF_FFN),
        in_specs=[
            pl.BlockSpec((TM_FFN, d), lambda i, f: (i, 0)),
            pl.BlockSpec((1, d), lambda i, f: (0, 0)),
            pl.BlockSpec((d, TF_FFN), lambda i, f: (0, f)),
            pl.BlockSpec((d, TF_FFN), lambda i, f: (0, f)),
            pl.BlockSpec((TF_FFN, d), lambda i, f: (f, 0)),
            cast_spec,
        ],
        out_specs=[pl.BlockSpec((TM_FFN, d), lambda i, f: (i, 0)), cast_spec],
        out_shape=[jax.ShapeDtypeStruct((t, d), F32), jax.ShapeDtypeStruct(w_cast.shape, BF16)],
        scratch_shapes=[pltpu.VMEM((TM_FFN, d), BF16)],
        compiler_params=_cparams(("parallel", "arbitrary"), 60),
        name="dense_ffn",
    )(x, g, wg, wu, wd, w_cast)


def _router_kernel(x_ref, g_ref, wr_ref, o_ref):
    h = _rms_norm(x_ref[...], g_ref[...])
    logits = jnp.dot(h, wr_ref[...], preferred_element_type=F32, precision=lax.Precision.HIGHEST)
    lane = lax.broadcasted_iota(jnp.int32, logits.shape, 1)
    neg = jnp.float32(-jnp.inf)
    l1 = jnp.where(lane < N_EXPERTS, logits, neg)
    m1 = jnp.max(l1, axis=-1, keepdims=True)
    i1 = jnp.min(jnp.where(l1 == m1, lane, LANES), axis=-1, keepdims=True)
    l2 = jnp.where(lane == i1, neg, l1)
    m2 = jnp.max(l2, axis=-1, keepdims=True)
    i2 = jnp.min(jnp.where(l2 == m2, lane, LANES), axis=-1, keepdims=True)
    dlt = jnp.exp(m2 - m1)
    w1 = 1.0 / (1.0 + dlt)
    w2 = dlt / (1.0 + dlt)
    out = jnp.where(lane == 0, i1.astype(F32),
                    jnp.where(lane == 1, i2.astype(F32),
                              jnp.where(lane == 2, w1, jnp.where(lane == 3, w2, 0.0))))
    o_ref[...] = out[:, 0:o_ref.shape[1]]


def _router(x, g, wr_pad):
    t, d = x.shape
    return pl.pallas_call(
        _router_kernel,
        grid=(t // TM_ROUTER,),
        in_specs=[
            pl.BlockSpec((TM_ROUTER, d), lambda i: (i, 0)),
            pl.BlockSpec((1, d), lambda i: (0, 0)),
            pl.BlockSpec((d, LANES), lambda i: (0, 0)),
        ],
        out_specs=pl.BlockSpec((TM_ROUTER, SUBLANES), lambda i: (i, 0)),
        out_shape=jax.ShapeDtypeStruct((t, SUBLANES), F32),
        compiler_params=_cparams(("parallel",), 32),
        name="router",
    )(x, g, wr_pad)


def _moe_ffn_kernel(asg_ref, be_ref, nr_ref, x_hbm, g_ref, wg_ref, wu_ref, wd_ref, y_hbm,
                    xg_ref, h_ref, acc_ref, gsem, ssem, *, n_tok):
    b = pl.program_id(0)
    f = pl.program_id(1)
    nb = pl.num_programs(0)
    nf = pl.num_programs(1)
    n_real = nr_ref[0]
    real = b < n_real
    slot = b % 2
    rows_per_step = MOE_BM // nf

    def token_of(a):
        return jnp.where(a >= TOP_K * n_tok, 0, jnp.where(a >= n_tok, a - n_tok, a))

    def gather_start(a, r, slt):
        pltpu.make_async_copy(x_hbm.at[pl.ds(token_of(a), 1), :], xg_ref.at[slt, pl.ds(r, 1), :],
                              gsem.at[slt]).start()

    def scatter_start(a, r, slt):
        pltpu.make_async_copy(acc_ref.at[slt, pl.ds(r, 1), :], y_hbm.at[pl.ds(a, 1), :], ssem).start()

    def gather_wait(slt):
        pltpu.make_async_copy(x_hbm.at[pl.ds(0, MOE_BM), :], xg_ref.at[slt], gsem.at[slt]).wait()

    def scatter_wait():
        pltpu.make_async_copy(acc_ref.at[0], y_hbm.at[pl.ds(0, MOE_BM), :], ssem).wait()

    def drain(blk):
        gather_wait((blk + 1) % 2)

        def body(r, carry):
            scatter_start(asg_ref[blk * MOE_BM + r], r, blk % 2)
            return carry

        lax.fori_loop(0, MOE_BM, body, 0, unroll=8)
        scatter_wait()

    @pl.when(real & (f == 0))
    def _():
        @pl.when(b == 0)
        def _():
            def body(r, carry):
                gather_start(asg_ref[r], r, 0)
                return carry

            lax.fori_loop(0, MOE_BM, body, 0, unroll=8)
            acc_ref[1] = jnp.zeros(acc_ref.shape[1:], acc_ref.dtype)

        gather_wait(slot)
        h_ref[...] = _rms_norm(xg_ref[slot], g_ref[...]).astype(BF16)

    @pl.when(real)
    def _():
        prev_base = jnp.maximum(b - 1, 0) * MOE_BM
        for i in range(rows_per_step):
            r = f * rows_per_step + i
            gather_start(asg_ref[(b + 1) * MOE_BM + r], r, 1 - slot)
            a_prev = jnp.where(b == 0, TOP_K * n_tok + r, asg_ref[prev_base + r])
            scatter_start(a_prev, r, 1 - slot)
        y = _swiglu_step(h_ref[...], wg_ref[0], wu_ref[0], wd_ref[0])

        @pl.when(f == 0)
        def _():
            acc_ref[slot] = y

        @pl.when(f > 0)
        def _():
            acc_ref[slot] += y

        @pl.when(f == nf - 1)
        def _():
            scatter_wait()

    @pl.when((b == n_real) & (f == 0))
    def _():
        drain(b - 1)

    @pl.when(real & (b == nb - 1) & (f == nf - 1))
    def _():
        drain(b)


def _moe_ffn(asg, block_e, n_real, x, g, wg, wu, wd):
    t, d = x.shape
    ff = wg.shape[2]
    nf = ff // TF_MOE
    n_blocks = block_e.shape[0]
    assert MOE_BM % nf == 0 and asg.shape[0] == (n_blocks + 1) * MOE_BM

    def f_eff(b, f, nr):
        return jnp.where(b < nr[0], f, nf - 1)

    grid_spec = pltpu.PrefetchScalarGridSpec(
        num_scalar_prefetch=3,
        grid=(n_blocks, nf),
        in_specs=[
            pl.BlockSpec(memory_space=pl.ANY),
            pl.BlockSpec((1, d), lambda b, f, asg, be, nr: (0, 0)),
            pl.BlockSpec((1, d, TF_MOE), lambda b, f, asg, be, nr: (be[b], 0, f_eff(b, f, nr))),
            pl.BlockSpec((1, d, TF_MOE), lambda b, f, asg, be, nr: (be[b], 0, f_eff(b, f, nr))),
            pl.BlockSpec((1, TF_MOE, d), lambda b, f, asg, be, nr: (be[b], f_eff(b, f, nr), 0)),
        ],
        out_specs=pl.BlockSpec(memory_space=pl.ANY),
        scratch_shapes=[
            pltpu.VMEM((2, MOE_BM, d), F32),
            pltpu.VMEM((MOE_BM, d), BF16),
            pltpu.VMEM((2, MOE_BM, d), F32),
            pltpu.SemaphoreType.DMA((2,)),
            pltpu.SemaphoreType.DMA(()),
        ],
    )
    return pl.pallas_call(
        functools.partial(_moe_ffn_kernel, n_tok=t),
        grid_spec=grid_spec,
        out_shape=jax.ShapeDtypeStruct((TOP_K * t + MOE_BM, d), F32),
        compiler_params=_cparams(("arbitrary", "arbitrary"), 56),
        name="moe_ffn",
    )(asg, block_e, n_real, x, g, wg, wu, wd)


def _combine_kernel(x_ref, info_ref, g_ref, ya_ref, yb_ref, o_ref, *, final_norm):
    w1 = info_ref[:, 2:3]
    w2 = info_ref[:, 3:4]
    z = x_ref[...] + (ya_ref[...] * w1 + yb_ref[...] * w2)
    o_ref[...] = _rms_norm(z, g_ref[...]) if final_norm else z


def _combine(x, info, g, y, final_norm):
    t, d = x.shape
    nt = t // TM_COMBINE
    return pl.pallas_call(
        functools.partial(_combine_kernel, final_norm=final_norm),
        grid=(nt,),
        in_specs=[
            pl.BlockSpec((TM_COMBINE, d), lambda i: (i, 0)),
            pl.BlockSpec((TM_COMBINE, SUBLANES), lambda i: (i, 0)),
            pl.BlockSpec((1, d), lambda i: (0, 0)),
            pl.BlockSpec((TM_COMBINE, d), lambda i: (i, 0)),
            pl.BlockSpec((TM_COMBINE, d), lambda i: (nt + i, 0)),
        ],
        out_specs=pl.BlockSpec((TM_COMBINE, d), lambda i: (i, 0)),
        out_shape=jax.ShapeDtypeStruct((t, d), F32),
        compiler_params=_cparams(("parallel",), 48),
        name="moe_combine",
    )(x, info, g, y, y)


def _routing_plan(info, n_tok):
    n_asg = n_tok * TOP_K
    n_blocks = n_asg // MOE_BM + N_EXPERTS
    flat_e = info[:, 0:TOP_K].astype(jnp.int32).reshape(n_asg)
    onehot = (flat_e[:, None] == jnp.arange(N_EXPERTS, dtype=jnp.int32)[None, :]).astype(jnp.int32)
    csum = jnp.cumsum(onehot, axis=0)
    counts = csum[-1]
    padded = ((counts + MOE_BM - 1) // MOE_BM) * MOE_BM
    p_ends = jnp.cumsum(padded)
    p_starts = p_ends - padded
    pos = jnp.sum(onehot * (csum - 1 + p_starts[None, :]), axis=1).astype(jnp.int32)
    flat = jnp.arange(n_asg, dtype=jnp.int32)
    dest = (flat % TOP_K) * n_tok + flat // TOP_K
    spare = n_asg + jnp.arange((n_blocks + 1) * MOE_BM, dtype=jnp.int32) % MOE_BM
    asg = spare.at[pos].set(dest)
    block_start = jnp.arange(n_blocks, dtype=jnp.int32) * MOE_BM
    block_e = jnp.minimum(jnp.searchsorted(p_ends, block_start, side="right"), N_EXPERTS - 1).astype(jnp.int32)
    n_real = (p_ends[-1] // MOE_BM).astype(jnp.int32).reshape(1)
    return asg, block_e, n_real


def kernel(x, mix_norm, w_in, conv_w, conv_b, conv_ln_g, conv_ln_b, lru_conv_w, lru_conv_b, lru_wa, lru_ba, lru_wx, lru_bx, lru_lambda, w_out, ffn_norm, dense_wg, dense_wu, dense_wd, w_router, moe_wg, moe_wu, moe_wd, final_norm):
    bsz, seq, d = x.shape
    depth = w_in.shape[0]
    t = bsz * seq
    xt = x.reshape(t, d)
    row = lambda v: v.reshape(1, -1)
    assert depth % 2 == 0, "the final RMSNorm is fused into the last routed layer's combine"
    flat2d = lambda w: w.reshape(-1, w.shape[-1])
    for layer in range(depth):
        j = layer // 2
        w_cast = flat2d(moe_wg[j]) if layer % 2 == 0 else flat2d(moe_wu[j])
        u = _in_proj(xt, row(mix_norm[layer]), w_in[layer].astype(BF16))
        y, w_cast_b = _mixer_core(u, bsz, seq, conv_w[layer], row(conv_b[layer]), row(conv_ln_g[layer]),
                                  row(conv_ln_b[layer]), lru_conv_w[layer], row(lru_conv_b[layer]),
                                  lru_wa[layer].astype(BF16), row(lru_ba[layer]), lru_wx[layer].astype(BF16),
                                  row(lru_bx[layer]), row(lru_lambda[layer]), w_cast)
        xt = _out_proj(xt, y, w_out[layer].astype(BF16))
        g = row(ffn_norm[layer])
        if layer % 2 == 0:
            wg_b = w_cast_b.reshape(moe_wg[j].shape)
            xt, wd_b = _dense_ffn(xt, g, dense_wg[j].astype(BF16), dense_wu[j].astype(BF16),
                                  dense_wd[j].astype(BF16), flat2d(moe_wd[j]))
            wd_b = wd_b.reshape(moe_wd[j].shape)
        else:
            wu_b = w_cast_b.reshape(moe_wu[j].shape)
            wr_pad = jnp.zeros((d, LANES), F32).at[:, 0:N_EXPERTS].set(w_router[j])
            info = _router(xt, g, wr_pad)
            asg, block_e, n_real = _routing_plan(info, t)
            y = _moe_ffn(asg, block_e, n_real, xt, g, wg_b, wu_b, wd_b)
            xt = _combine(xt, info, row(final_norm), y, layer == depth - 1)
    return xt.reshape(bsz, seq, d)
```

```python
import functools

import jax
import jax.numpy as jnp
from jax import lax
from jax.experimental import pallas as pl
from jax.experimental.pallas import tpu as pltpu

F32 = jnp.float32
BF16 = jnp.bfloat16

D_MODEL = 2048
CONV_WIDTH = 1024
LRU_WIDTH = 1024
LRU_HEADS = 8
LRU_HEAD_DIM = LRU_WIDTH // LRU_HEADS
D_IN = 2 * CONV_WIDTH + 2 * LRU_WIDTH
CONV_K = 31
LRU_CONV_K = 4
LRU_C = 8.0
N_EXPERTS = 8
TOP_K = 2
EPS = 1e-6

LANES = 128
SUBLANES = 8
BF16_ROWS = 16
MIB = 1024 * 1024

TM_PROJ = 512
TN_IN = 1024
TS_MIX = 256
CONV_HALO = 32
LRU_HALO = 8
CONV_ROWS = 64
TM_FFN = 512
TF_FFN = 1024
TF_MOE = 768
MOE_BM = 512
TM_COMBINE = 512


def _cparams(semantics, vmem_mib):
    return pltpu.CompilerParams(dimension_semantics=semantics, vmem_limit_bytes=vmem_mib * MIB)


def _sigmoid(x):
    return 0.5 * jnp.tanh(0.5 * x) + 0.5


def _rms_norm(x, g):
    ms = jnp.mean(x * x, axis=-1, keepdims=True)
    return x * lax.rsqrt(ms + EPS) * g


def _cast_specs(casts, n_steps, index_map):
    specs = []
    for w in casts:
        rows, cols = w.shape
        assert rows % n_steps == 0 and (rows // n_steps) % BF16_ROWS == 0, (w.shape, n_steps)
        specs.append(pl.BlockSpec((rows // n_steps, cols), index_map))
    return specs


def _cast_shapes(casts):
    return [jax.ShapeDtypeStruct(w.shape, BF16) for w in casts]


def _run_casts(in_refs, out_refs):
    for src, dst in zip(in_refs, out_refs, strict=True):
        dst[...] = src[...].astype(dst.dtype)


def _in_proj_kernel(x_ref, g_ref, w_ref, *refs, n_cast):
    cast_in, o_ref, cast_out = refs[:n_cast], refs[n_cast], refs[n_cast + 1:]
    _run_casts(cast_in, cast_out)
    h = _rms_norm(x_ref[...], g_ref[...]).astype(BF16)
    for j in range(o_ref.shape[1] // TN_IN):
        cols = slice(j * TN_IN, (j + 1) * TN_IN)
        o_ref[:, cols] = jnp.dot(h, w_ref[:, cols], preferred_element_type=F32)


def _in_proj(x, g, w, casts):
    t, d = x.shape
    n = w.shape[1]
    steps = t // TM_PROJ
    cast_specs = _cast_specs(casts, steps, lambda i: (i, 0))
    return pl.pallas_call(
        functools.partial(_in_proj_kernel, n_cast=len(casts)),
        grid=(steps,),
        in_specs=[
            pl.BlockSpec((TM_PROJ, d), lambda i: (i, 0)),
            pl.BlockSpec((1, d), lambda i: (0, 0)),
            pl.BlockSpec((d, n), lambda i: (0, 0), pipeline_mode=pl.Buffered(1)),
            *cast_specs,
        ],
        out_specs=[pl.BlockSpec((TM_PROJ, n), lambda i: (i, 0)), *cast_specs],
        out_shape=[jax.ShapeDtypeStruct((t, n), F32), *_cast_shapes(casts)],
        compiler_params=_cparams(("parallel",), 60),
        name="in_proj",
    )(x, g, w, *casts)


def _causal_depthwise_conv(buf, w_ref, b_ref, out_ref, ts, halo, n_taps):
    base = halo - (n_taps - 1)
    nt = CONV_ROWS // SUBLANES
    nq = (base + n_taps - 1) // SUBLANES + 1
    row8 = lax.broadcasted_iota(jnp.int32, (SUBLANES, LANES), 0)
    for rc in range(ts // CONV_ROWS):
        t0 = rc * CONV_ROWS
        for lc in range(buf.shape[1] // LANES):
            cols = slice(lc * LANES, (lc + 1) * LANES)
            xs = [buf[t0 + SUBLANES * j:t0 + SUBLANES * (j + 1), cols] for j in range(nt + nq - 1)]
            bias = jnp.broadcast_to(b_ref[:, cols], (SUBLANES, LANES))
            out = [bias] * nt
            for p in range(SUBLANES):
                taps = [(q, SUBLANES * q + p - base) for q in range(nq)
                        if 0 <= SUBLANES * q + p - base < n_taps]
                if not taps:
                    continue
                ws = [jnp.broadcast_to(w_ref[k:k + 1, cols], (SUBLANES, LANES)) for _, k in taps]
                zs = []
                for j in range(nt + (1 if p else 0)):
                    z = ws[0] * xs[j + taps[0][0]]
                    for w, (q, _) in zip(ws[1:], taps[1:]):
                        z = z + w * xs[j + q]
                    zs.append(z)
                if p == 0:
                    out = [o + z for o, z in zip(out, zs)]
                else:
                    rs = [pltpu.roll(z, SUBLANES - p, axis=0) for z in zs]
                    keep = row8 < SUBLANES - p
                    out = [o + jnp.where(keep, rs[g], rs[g + 1]) for g, o in enumerate(out)]
            for g in range(nt):
                out_ref[t0 + SUBLANES * g:t0 + SUBLANES * (g + 1), cols] = out[g]


def _mixer_kernel(u_ref, cw_ref, cb_ref, lg_ref, lb_ref, rw_ref, rb_ref, wa_ref, ba_ref,
                  wx_ref, bx_ref, lam_ref, *refs, n_cast):
    cast_in, o_ref, cast_out = refs[:n_cast], refs[n_cast], refs[n_cast + 1:2 * n_cast + 1]
    cbuf, rbuf, abuf, bbuf, hc_ref = refs[2 * n_cast + 1:]
    s = pl.program_id(1)
    ts = TS_MIX
    _run_casts(cast_in, cast_out)

    @pl.when(s == 0)
    def _():
        cbuf[0:CONV_HALO, :] = jnp.zeros((CONV_HALO, CONV_WIDTH), F32)
        rbuf[0:LRU_HALO, :] = jnp.zeros((LRU_HALO, LRU_WIDTH), F32)
        hc_ref[...] = jnp.zeros((SUBLANES, LRU_WIDTH), F32)

    val = u_ref[:, 0:CONV_WIDTH]
    gate = u_ref[:, CONV_WIDTH:2 * CONV_WIDTH]
    cbuf[CONV_HALO:CONV_HALO + ts, :] = val * _sigmoid(gate)
    _causal_depthwise_conv(cbuf, cw_ref, cb_ref, abuf, ts, CONV_HALO, CONV_K)
    cbuf[0:CONV_HALO, :] = cbuf[ts:ts + CONV_HALO, :]
    c = abuf[...]
    mu = jnp.mean(c, axis=-1, keepdims=True)
    cc = c - mu
    var = jnp.mean(cc * cc, axis=-1, keepdims=True)
    cn = cc * lax.rsqrt(var + EPS) * lg_ref[...] + lb_ref[...]
    o_ref[:, 0:CONV_WIDTH] = (cn * _sigmoid(cn)).astype(o_ref.dtype)

    rbuf[LRU_HALO:LRU_HALO + ts, :] = u_ref[:, 2 * CONV_WIDTH:2 * CONV_WIDTH + LRU_WIDTH]
    _causal_depthwise_conv(rbuf, rw_ref, rb_ref, bbuf, ts, LRU_HALO, LRU_CONV_K)
    rbuf[0:LRU_HALO, :] = rbuf[ts:ts + LRU_HALO, :]
    for h in range(LRU_HEADS):
        cols = slice(h * LRU_HEAD_DIM, (h + 1) * LRU_HEAD_DIM)
        xh = bbuf[:, cols]
        xh_b = xh.astype(BF16)
        ga = jnp.dot(xh_b, wa_ref[h], preferred_element_type=F32) + ba_ref[:, cols]
        gx = jnp.dot(xh_b, wx_ref[h], preferred_element_type=F32) + bx_ref[:, cols]
        lam = lam_ref[:, cols]
        e = jnp.exp(-jnp.abs(lam))
        e1 = 1.0 + e
        log1p_e = jnp.where(e1 == 1.0, e, jnp.log(e1) * (e / (e1 - 1.0)))
        sp = jnp.maximum(-lam, 0.0) + log1p_e
        log_a = (-LRU_C) * _sigmoid(ga) * sp
        a = jnp.exp(log_a)
        mult = jnp.sqrt(-jnp.tanh(log_a) * (1.0 + a * a))
        row = lax.broadcasted_iota(jnp.int32, (ts, LRU_HEAD_DIM), 0)
        mult = jnp.where((row == 0) & (s == 0), 1.0, mult)
        abuf[:, cols] = a
        bbuf[:, cols] = mult * (_sigmoid(gx) * xh)

    row8 = lax.broadcasted_iota(jnp.int32, (SUBLANES, LRU_WIDTH), 0)
    h_prev = hc_ref[...]
    for g in range(ts // SUBLANES):
        rows = slice(g * SUBLANES, (g + 1) * SUBLANES)
        a = abuf[rows, :]
        b = bbuf[rows, :]
        for dsh in (1, 2, 4):
            a_s = pltpu.roll(a, dsh, axis=0)
            b_s = pltpu.roll(b, dsh, axis=0)
            m = row8 >= dsh
            b = jnp.where(m, a * b_s + b, b)
            a = jnp.where(m, a * a_s, a)
        hg = a * h_prev + b
        abuf[rows, :] = hg
        h_prev = jnp.broadcast_to(hg[SUBLANES - 1:SUBLANES, :], (SUBLANES, LRU_WIDTH))
    hc_ref[...] = h_prev

    rg = u_ref[:, 2 * CONV_WIDTH + LRU_WIDTH:D_IN]
    gelu = 0.5 * rg * (1.0 + jnp.tanh(0.7978845608028654 * (rg + 0.044715 * (rg * rg * rg))))
    o_ref[:, CONV_WIDTH:CONV_WIDTH + LRU_WIDTH] = (abuf[...] * gelu).astype(o_ref.dtype)


def _mixer_core(u, bsz, seq, cw, cb, lg, lb, rw, rb, wa, ba, wx, bx, lam, casts):
    t = u.shape[0]
    nsb = seq // TS_MIX
    row = lambda n: pl.BlockSpec((1, n), lambda b, s: (0, 0))
    cast_specs = _cast_specs(casts, bsz * nsb, lambda b, s: (b * nsb + s, 0))
    return pl.pallas_call(
        functools.partial(_mixer_kernel, n_cast=len(casts)),
        grid=(bsz, nsb),
        in_specs=[
            pl.BlockSpec((TS_MIX, D_IN), lambda b, s: (b * nsb + s, 0)),
            pl.BlockSpec((CONV_K, CONV_WIDTH), lambda b, s: (0, 0)),
            row(CONV_WIDTH), row(CONV_WIDTH), row(CONV_WIDTH),
            pl.BlockSpec((LRU_CONV_K, LRU_WIDTH), lambda b, s: (0, 0)),
            row(LRU_WIDTH),
            pl.BlockSpec((LRU_HEADS, LRU_HEAD_DIM, LRU_HEAD_DIM), lambda b, s: (0, 0, 0)),
            row(LRU_WIDTH),
            pl.BlockSpec((LRU_HEADS, LRU_HEAD_DIM, LRU_HEAD_DIM), lambda b, s: (0, 0, 0)),
            row(LRU_WIDTH), row(LRU_WIDTH),
            *cast_specs,
        ],
        out_specs=[pl.BlockSpec((TS_MIX, CONV_WIDTH + LRU_WIDTH), lambda b, s: (b * nsb + s, 0)), *cast_specs],
        out_shape=[jax.ShapeDtypeStruct((t, CONV_WIDTH + LRU_WIDTH), BF16), *_cast_shapes(casts)],
        scratch_shapes=[
            pltpu.VMEM((CONV_HALO + TS_MIX, CONV_WIDTH), F32),
            pltpu.VMEM((LRU_HALO + TS_MIX, LRU_WIDTH), F32),
            pltpu.VMEM((TS_MIX, LRU_WIDTH), F32),
            pltpu.VMEM((TS_MIX, LRU_WIDTH), F32),
            pltpu.VMEM((SUBLANES, LRU_WIDTH), F32),
        ],
        compiler_params=_cparams(("parallel", "arbitrary"), 56),
        name="mixer_core",
    )(u, cw, cb, lg, lb, rw, rb, wa, ba, wx, bx, lam, *casts)


def _top2_info(logits, width):
    lane = lax.broadcasted_iota(jnp.int32, logits.shape, 1)
    neg = jnp.float32(-jnp.inf)
    l1 = jnp.where(lane < N_EXPERTS, logits, neg)
    m1 = jnp.max(l1, axis=-1, keepdims=True)
    i1 = jnp.min(jnp.where(l1 == m1, lane, LANES), axis=-1, keepdims=True)
    l2 = jnp.where(lane == i1, neg, l1)
    m2 = jnp.max(l2, axis=-1, keepdims=True)
    i2 = jnp.min(jnp.where(l2 == m2, lane, LANES), axis=-1, keepdims=True)
    dlt = jnp.exp(m2 - m1)
    w1 = 1.0 / (1.0 + dlt)
    w2 = dlt / (1.0 + dlt)
    out = jnp.where(lane == 0, i1.astype(F32),
                    jnp.where(lane == 1, i2.astype(F32),
                              jnp.where(lane == 2, w1, jnp.where(lane == 3, w2, 0.0))))
    return out[:, 0:width]


def _out_proj_kernel(x_ref, y_ref, w_ref, *refs, n_cast, route):
    refs = list(refs)
    if route:
        gf_ref, wr_ref = refs[:2]
        refs = refs[2:]
    cast_in, refs = refs[:n_cast], refs[n_cast:]
    o_ref, refs = refs[0], refs[1:]
    if route:
        info_ref, refs = refs[0], refs[1:]
    cast_out = refs
    _run_casts(cast_in, cast_out)
    xn = x_ref[...] + jnp.dot(y_ref[...], w_ref[...], preferred_element_type=F32)
    o_ref[...] = xn
    if route:
        h = _rms_norm(xn, gf_ref[...]).astype(BF16)
        logits = jnp.dot(h, wr_ref[...], preferred_element_type=F32)
        info_ref[...] = _top2_info(logits, info_ref.shape[1])


def _out_proj(x, y, w, casts, router=None):
    t, d = x.shape
    k = y.shape[1]
    steps = t // TM_PROJ
    cast_specs = _cast_specs(casts, steps, lambda i: (i, 0))
    const = lambda shape: pl.BlockSpec(shape, lambda i: (0, 0))
    route_in = [] if router is None else [const((1, d)), const((d, LANES))]
    route_out = [] if router is None else [pl.BlockSpec((TM_PROJ, SUBLANES), lambda i: (i, 0))]
    route_shape = [] if router is None else [jax.ShapeDtypeStruct((t, SUBLANES), F32)]
    return pl.pallas_call(
        functools.partial(_out_proj_kernel, n_cast=len(casts), route=router is not None),
        grid=(steps,),
        in_specs=[
            pl.BlockSpec((TM_PROJ, d), lambda i: (i, 0)),
            pl.BlockSpec((TM_PROJ, k), lambda i: (i, 0)),
            const((k, d)),
            *route_in,
            *cast_specs,
        ],
        out_specs=[pl.BlockSpec((TM_PROJ, d), lambda i: (i, 0)), *route_out, *cast_specs],
        out_shape=[jax.ShapeDtypeStruct((t, d), F32), *route_shape, *_cast_shapes(casts)],
        compiler_params=_cparams(("parallel",), 56),
        name="out_proj",
    )(x, y, w, *(router or ()), *casts)


def _swiglu_step(h, wg, wu, wd):
    g = jnp.dot(h, wg, preferred_element_type=F32)
    v = jnp.dot(h, wu, preferred_element_type=F32)
    act = (g * _sigmoid(g) * v).astype(BF16)
    return jnp.dot(act, wd, preferred_element_type=F32)


def _dense_ffn_kernel(x_ref, g_ref, wg_ref, wu_ref, wd_ref, *refs, n_cast):
    cast_in, o_ref, cast_out, h_ref = refs[:n_cast], refs[n_cast], refs[n_cast + 1:2 * n_cast + 1], refs[-1]
    _run_casts(cast_in, cast_out)

    @pl.when(pl.program_id(1) == 0)
    def _():
        h_ref[...] = _rms_norm(x_ref[...], g_ref[...]).astype(BF16)
        o_ref[...] = x_ref[...]

    o_ref[...] += _swiglu_step(h_ref[...], wg_ref[...], wu_ref[...], wd_ref[...])


def _dense_ffn(x, g, wg, wu, wd, casts):
    t, d = x.shape
    ff = wg.shape[1]
    nf = ff // TF_FFN
    cast_specs = _cast_specs(casts, t // TM_FFN * nf, lambda i, f: (i * nf + f, 0))
    return pl.pallas_call(
        functools.partial(_dense_ffn_kernel, n_cast=len(casts)),
        grid=(t // TM_FFN, nf),
        in_specs=[
            pl.BlockSpec((TM_FFN, d), lambda i, f: (i, 0)),
            pl.BlockSpec((1, d), lambda i, f: (0, 0)),
            pl.BlockSpec((d, TF_FFN), lambda i, f: (0, f)),
            pl.BlockSpec((d, TF_FFN), lambda i, f: (0, f)),
            pl.BlockSpec((TF_FFN, d), lambda i, f: (f, 0)),
            *cast_specs,
        ],
        out_specs=[pl.BlockSpec((TM_FFN, d), lambda i, f: (i, 0)), *cast_specs],
        out_shape=[jax.ShapeDtypeStruct((t, d), F32), *_cast_shapes(casts)],
        scratch_shapes=[pltpu.VMEM((TM_FFN, d), BF16)],
        compiler_params=_cparams(("parallel", "arbitrary"), 60),
        name="dense_ffn",
    )(x, g, wg, wu, wd, *casts)


def _moe_ffn_kernel(tok_ref, dst_ref, be_ref, nr_ref, x_hbm, g_ref, wg_ref, wu_ref, wd_ref, y_hbm,
                    xg_ref, h_ref, acc_ref, gsem, ssem, *, n_tok):
    b = pl.program_id(0)
    f = pl.program_id(1)
    nb = pl.num_programs(0)
    nf = pl.num_programs(1)
    n_real = nr_ref[0]
    real = b < n_real
    slot = b % 2
    rows_per_step = MOE_BM // nf

    def gather_start(tok, r, slt):
        pltpu.make_async_copy(x_hbm.at[pl.ds(tok, 1), :], xg_ref.at[slt, pl.ds(r, 1), :],
                              gsem.at[slt]).start()

    def scatter_start(a, r, slt):
        pltpu.make_async_copy(acc_ref.at[slt, pl.ds(r, 1), :], y_hbm.at[pl.ds(a, 1), :], ssem).start()

    def gather_wait(slt):
        pltpu.make_async_copy(x_hbm.at[pl.ds(0, MOE_BM), :], xg_ref.at[slt], gsem.at[slt]).wait()

    def scatter_wait():
        pltpu.make_async_copy(acc_ref.at[0], y_hbm.at[pl.ds(0, MOE_BM), :], ssem).wait()

    def drain(blk):
        gather_wait((blk + 1) % 2)

        def body(r, carry):
            scatter_start(dst_ref[blk * MOE_BM + r], r, blk % 2)
            return carry

        lax.fori_loop(0, MOE_BM, body, 0, unroll=8)
        scatter_wait()

    @pl.when(real & (f == 0))
    def _():
        @pl.when(b == 0)
        def _():
            def body(r, carry):
                gather_start(tok_ref[r], r, 0)
                return carry

            lax.fori_loop(0, MOE_BM, body, 0, unroll=8)
            acc_ref[1] = jnp.zeros(acc_ref.shape[1:], acc_ref.dtype)

        gather_wait(slot)
        h_ref[...] = _rms_norm(xg_ref[slot], g_ref[...]).astype(BF16)

    @pl.when(real)
    def _():
        prev_base = jnp.maximum(b - 1, 0) * MOE_BM
        for i in range(rows_per_step):
            r = f * rows_per_step + i
            gather_start(tok_ref[(b + 1) * MOE_BM + r], r, 1 - slot)
            a_prev = jnp.where(b == 0, TOP_K * n_tok + r, dst_ref[prev_base + r])
            scatter_start(a_prev, r, 1 - slot)
        y = _swiglu_step(h_ref[...], wg_ref[0], wu_ref[0], wd_ref[0])

        @pl.when(f == 0)
        def _():
            acc_ref[slot] = y

        @pl.when(f > 0)
        def _():
            acc_ref[slot] += y

        @pl.when(f == nf - 1)
        def _():
            scatter_wait()

    @pl.when((b == n_real) & (f == 0))
    def _():
        drain(b - 1)

    @pl.when(real & (b == nb - 1) & (f == nf - 1))
    def _():
        drain(b)


def _moe_ffn(tok, dst, block_e, n_real, x, g, wg, wu, wd):
    t, d = x.shape
    ff = wg.shape[2]
    nf = ff // TF_MOE
    n_blocks = block_e.shape[0]
    assert MOE_BM % nf == 0 and tok.shape[0] == (n_blocks + 1) * MOE_BM == dst.shape[0]

    def f_eff(b, f, nr):
        return jnp.where(b < nr[0], f, nf - 1)

    grid_spec = pltpu.PrefetchScalarGridSpec(
        num_scalar_prefetch=4,
        grid=(n_blocks, nf),
        in_specs=[
            pl.BlockSpec(memory_space=pl.ANY),
            pl.BlockSpec((1, d), lambda b, f, tok, dst, be, nr: (0, 0)),
            pl.BlockSpec((1, d, TF_MOE), lambda b, f, tok, dst, be, nr: (be[b], 0, f_eff(b, f, nr))),
            pl.BlockSpec((1, d, TF_MOE), lambda b, f, tok, dst, be, nr: (be[b], 0, f_eff(b, f, nr))),
            pl.BlockSpec((1, TF_MOE, d), lambda b, f, tok, dst, be, nr: (be[b], f_eff(b, f, nr), 0)),
        ],
        out_specs=pl.BlockSpec(memory_space=pl.ANY),
        scratch_shapes=[
            pltpu.VMEM((2, MOE_BM, d), F32),
            pltpu.VMEM((MOE_BM, d), BF16),
            pltpu.VMEM((2, MOE_BM, d), F32),
            pltpu.SemaphoreType.DMA((2,)),
            pltpu.SemaphoreType.DMA(()),
        ],
    )
    return pl.pallas_call(
        functools.partial(_moe_ffn_kernel, n_tok=t),
        grid_spec=grid_spec,
        out_shape=jax.ShapeDtypeStruct((TOP_K * t + MOE_BM, d), F32),
        compiler_params=_cparams(("arbitrary", "arbitrary"), 56),
        name="moe_ffn",
    )(tok, dst, block_e, n_real, x, g, wg, wu, wd)


def _combine_kernel(x_ref, info_ref, g_ref, ya_ref, yb_ref, o_ref, *, final_norm):
    w1 = info_ref[:, 2:3]
    w2 = info_ref[:, 3:4]
    z = x_ref[...] + (ya_ref[...] * w1 + yb_ref[...] * w2)
    o_ref[...] = _rms_norm(z, g_ref[...]) if final_norm else z


def _combine(x, info, g, y, final_norm):
    t, d = x.shape
    nt = t // TM_COMBINE
    return pl.pallas_call(
        functools.partial(_combine_kernel, final_norm=final_norm),
        grid=(nt,),
        in_specs=[
            pl.BlockSpec((TM_COMBINE, d), lambda i: (i, 0)),
            pl.BlockSpec((TM_COMBINE, SUBLANES), lambda i: (i, 0)),
            pl.BlockSpec((1, d), lambda i: (0, 0)),
            pl.BlockSpec((TM_COMBINE, d), lambda i: (i, 0)),
            pl.BlockSpec((TM_COMBINE, d), lambda i: (nt + i, 0)),
        ],
        out_specs=pl.BlockSpec((TM_COMBINE, d), lambda i: (i, 0)),
        out_shape=jax.ShapeDtypeStruct((t, d), F32),
        compiler_params=_cparams(("parallel",), 48),
        name="moe_combine",
    )(x, info, g, y, y)


def _routing_plan(info, n_tok):
    n_asg = n_tok * TOP_K
    n_blocks = n_asg // MOE_BM + N_EXPERTS
    flat_e = info[:, 0:TOP_K].astype(jnp.int32).reshape(n_asg)
    onehot = (flat_e[:, None] == jnp.arange(N_EXPERTS, dtype=jnp.int32)[None, :]).astype(jnp.int32)
    csum = jnp.cumsum(onehot, axis=0)
    counts = csum[-1]
    padded = ((counts + MOE_BM - 1) // MOE_BM) * MOE_BM
    p_ends = jnp.cumsum(padded)
    p_starts = p_ends - padded
    pos = jnp.sum(onehot * (csum - 1 + p_starts[None, :]), axis=1).astype(jnp.int32)
    flat = jnp.arange(n_asg, dtype=jnp.int32)
    dest = (flat % TOP_K) * n_tok + flat // TOP_K
    spare = n_asg + jnp.arange((n_blocks + 1) * MOE_BM, dtype=jnp.int32) % MOE_BM
    dst = spare.at[pos].set(dest)
    tok = jnp.where(dst >= n_asg, 0, dst % n_tok)
    block_start = jnp.arange(n_blocks, dtype=jnp.int32) * MOE_BM
    block_e = jnp.minimum(jnp.searchsorted(p_ends, block_start, side="right"), N_EXPERTS - 1).astype(jnp.int32)
    n_real = (p_ends[-1] // MOE_BM).astype(jnp.int32).reshape(1)
    return tok, dst, block_e, n_real


def kernel(x, mix_norm, w_in, conv_w, conv_b, conv_ln_g, conv_ln_b, lru_conv_w, lru_conv_b, lru_wa, lru_ba, lru_wx, lru_bx, lru_lambda, w_out, ffn_norm, dense_wg, dense_wu, dense_wd, w_router, moe_wg, moe_wu, moe_wd, final_norm):
    bsz, seq, d = x.shape
    depth = w_in.shape[0]
    t = bsz * seq
    xt = x.reshape(t, d)
    row = lambda v: v.reshape(1, -1)
    assert depth % 2 == 0, "the final RMSNorm is fused into the last routed layer's combine"
    flat2d = lambda w: w.reshape(-1, w.shape[-1])
    w_in_b = w_in[0].astype(BF16)
    for layer in range(0, depth, 2):
        j = layer // 2
        odd = layer + 1
        u, wg_d, wu_d, w_out_b = _in_proj(xt, row(mix_norm[layer]), w_in_b,
                                           [dense_wg[j], dense_wu[j], w_out[layer]])
        y, moe_wg_b, wd_d = _mixer_core(
            u, bsz, seq, conv_w[layer], row(conv_b[layer]), row(conv_ln_g[layer]), row(conv_ln_b[layer]),
            lru_conv_w[layer], row(lru_conv_b[layer]), lru_wa[layer].astype(BF16), row(lru_ba[layer]),
            lru_wx[layer].astype(BF16), row(lru_bx[layer]), row(lru_lambda[layer]),
            [flat2d(moe_wg[j]), dense_wd[j]])
        xt, w_in_odd = _out_proj(xt, y, w_out_b, [w_in[odd]])
        xt, moe_wd_b = _dense_ffn(xt, row(ffn_norm[layer]), wg_d, wu_d, wd_d, [flat2d(moe_wd[j])])

        hosted = [w_out[odd]] + ([w_in[odd + 1]] if odd + 1 < depth else [])
        u, w_out_b, *nxt = _in_proj(xt, row(mix_norm[odd]), w_in_odd, hosted)
        if nxt:
            w_in_b = nxt[0]
        y, moe_wu_b = _mixer_core(
            u, bsz, seq, conv_w[odd], row(conv_b[odd]), row(conv_ln_g[odd]), row(conv_ln_b[odd]),
            lru_conv_w[odd], row(lru_conv_b[odd]), lru_wa[odd].astype(BF16), row(lru_ba[odd]),
            lru_wx[odd].astype(BF16), row(lru_bx[odd]), row(lru_lambda[odd]), [flat2d(moe_wu[j])])
        wr = jnp.zeros((d, LANES), BF16).at[:, 0:N_EXPERTS].set(w_router[j].astype(BF16))
        g = row(ffn_norm[odd])
        xt, info = _out_proj(xt, y, w_out_b, [], router=(g, wr))
        tok, dst, block_e, n_real = _routing_plan(info, t)
        y = _moe_ffn(tok, dst, block_e, n_real, xt, g, moe_wg_b.reshape(moe_wg[j].shape),
                     moe_wu_b.reshape(moe_wu[j].shape), moe_wd_b.reshape(moe_wd[j].shape))
        xt = _combine(xt, info, row(final_norm), y, odd == depth - 1)
    return xt.reshape(bsz, seq, d)
```

```python
import functools

import jax
import jax.numpy as jnp
from jax import lax
from jax.experimental import pallas as pl
from jax.experimental.pallas import tpu as pltpu

F32 = jnp.float32
BF16 = jnp.bfloat16

D_MODEL = 2048
CONV_WIDTH = 1024
LRU_WIDTH = 1024
LRU_HEADS = 8
LRU_HEAD_DIM = LRU_WIDTH // LRU_HEADS
D_IN = 2 * CONV_WIDTH + 2 * LRU_WIDTH
CONV_K = 31
LRU_CONV_K = 4
LRU_C = 8.0
N_EXPERTS = 8
TOP_K = 2
EPS = 1e-6

LANES = 128
SUBLANES = 8
BF16_ROWS = 16
MIB = 1024 * 1024

TM_PROJ = 512
TN_IN = 1024
TS_MIX = 256
CONV_HALO = 32
LRU_HALO = 8
CONV_ROWS = 64
TM_FFN = 512
TF_FFN = 1024
TF_MOE = 1024
MOE_ISSUE_STEPS = 4
MOE_BM = 512
TM_COMBINE = 512


def _cparams(semantics, vmem_mib):
    return pltpu.CompilerParams(dimension_semantics=semantics, vmem_limit_bytes=vmem_mib * MIB)


def _sigmoid(x):
    return 0.5 * jnp.tanh(0.5 * x) + 0.5


def _rms_norm(x, g):
    ms = jnp.mean(x * x, axis=-1, keepdims=True)
    return x * lax.rsqrt(ms + EPS) * g


def _cast_specs(casts, n_steps, index_map):
    in_specs, out_specs = [], []
    for w, layer in casts:
        _, rows, cols = w.shape
        assert rows % n_steps == 0 and (rows // n_steps) % BF16_ROWS == 0, (w.shape, n_steps)
        in_specs.append(pl.BlockSpec((None, rows // n_steps, cols),
                                     lambda *idx, layer=layer: (layer, *index_map(*idx))))
        out_specs.append(pl.BlockSpec((rows // n_steps, cols), index_map))
    return in_specs, out_specs


def _cast_shapes(casts):
    return [jax.ShapeDtypeStruct(w.shape[1:], BF16) for w, _ in casts]


def _cast_args(casts):
    return [w for w, _ in casts]


def _run_casts(in_refs, out_refs):
    for src, dst in zip(in_refs, out_refs, strict=True):
        dst[...] = src[...].astype(dst.dtype)


def _in_proj_kernel(x_ref, g_ref, w_ref, *refs, n_cast):
    cast_in, o_ref, cast_out = refs[:n_cast], refs[n_cast], refs[n_cast + 1:]
    _run_casts(cast_in, cast_out)
    h = _rms_norm(x_ref[...], g_ref[...]).astype(BF16)
    for j in range(o_ref.shape[1] // TN_IN):
        cols = slice(j * TN_IN, (j + 1) * TN_IN)
        o_ref[:, cols] = jnp.dot(h, w_ref[:, cols], preferred_element_type=F32)


def _in_proj(x, g, w, casts):
    t, d = x.shape
    n = w.shape[1]
    steps = t // TM_PROJ
    cast_in, cast_out = _cast_specs(casts, steps, lambda i: (i, 0))
    return pl.pallas_call(
        functools.partial(_in_proj_kernel, n_cast=len(casts)),
        grid=(steps,),
        in_specs=[
            pl.BlockSpec((TM_PROJ, d), lambda i: (i, 0)),
            pl.BlockSpec((1, d), lambda i: (0, 0)),
            pl.BlockSpec((d, n), lambda i: (0, 0), pipeline_mode=pl.Buffered(1)),
            *cast_in,
        ],
        out_specs=[pl.BlockSpec((TM_PROJ, n), lambda i: (i, 0)), *cast_out],
        out_shape=[jax.ShapeDtypeStruct((t, n), F32), *_cast_shapes(casts)],
        compiler_params=_cparams(("parallel",), 60),
        name="in_proj",
    )(x, g, w, *_cast_args(casts))


def _causal_depthwise_conv(buf, w_ref, b_ref, out_ref, ts, halo, n_taps):
    base = halo - (n_taps - 1)
    nt = CONV_ROWS // SUBLANES
    nq = (base + n_taps - 1) // SUBLANES + 1
    row8 = lax.broadcasted_iota(jnp.int32, (SUBLANES, LANES), 0)
    for rc in range(ts // CONV_ROWS):
        t0 = rc * CONV_ROWS
        for lc in range(buf.shape[1] // LANES):
            cols = slice(lc * LANES, (lc + 1) * LANES)
            xs = [buf[t0 + SUBLANES * j:t0 + SUBLANES * (j + 1), cols] for j in range(nt + nq - 1)]
            bias = jnp.broadcast_to(b_ref[:, cols], (SUBLANES, LANES))
            out = [bias] * nt
            for p in range(SUBLANES):
                taps = [(q, SUBLANES * q + p - base) for q in range(nq)
                        if 0 <= SUBLANES * q + p - base < n_taps]
                if not taps:
                    continue
                ws = [jnp.broadcast_to(w_ref[k:k + 1, cols], (SUBLANES, LANES)) for _, k in taps]
                zs = []
                for j in range(nt + (1 if p else 0)):
                    z = ws[0] * xs[j + taps[0][0]]
                    for w, (q, _) in zip(ws[1:], taps[1:]):
                        z = z + w * xs[j + q]
                    zs.append(z)
                if p == 0:
                    out = [o + z for o, z in zip(out, zs)]
                else:
                    rs = [pltpu.roll(z, SUBLANES - p, axis=0) for z in zs]
                    keep = row8 < SUBLANES - p
                    out = [o + jnp.where(keep, rs[g], rs[g + 1]) for g, o in enumerate(out)]
            for g in range(nt):
                out_ref[t0 + SUBLANES * g:t0 + SUBLANES * (g + 1), cols] = out[g]


def _mixer_kernel(u_ref, cw_ref, cb_ref, lg_ref, lb_ref, rw_ref, rb_ref, wa_ref, ba_ref,
                  wx_ref, bx_ref, lam_ref, *refs, n_cast):
    cast_in, o_ref, cast_out = refs[:n_cast], refs[n_cast], refs[n_cast + 1:2 * n_cast + 1]
    cbuf, rbuf, abuf, bbuf, hc_ref = refs[2 * n_cast + 1:]
    s = pl.program_id(1)
    ts = TS_MIX
    _run_casts(cast_in, cast_out)

    @pl.when(s == 0)
    def _():
        cbuf[0:CONV_HALO, :] = jnp.zeros((CONV_HALO, CONV_WIDTH), F32)
        rbuf[0:LRU_HALO, :] = jnp.zeros((LRU_HALO, LRU_WIDTH), F32)
        hc_ref[...] = jnp.zeros((SUBLANES, LRU_WIDTH), F32)

    val = u_ref[:, 0:CONV_WIDTH]
    gate = u_ref[:, CONV_WIDTH:2 * CONV_WIDTH]
    cbuf[CONV_HALO:CONV_HALO + ts, :] = val * _sigmoid(gate)
    _causal_depthwise_conv(cbuf, cw_ref, cb_ref, abuf, ts, CONV_HALO, CONV_K)
    cbuf[0:CONV_HALO, :] = cbuf[ts:ts + CONV_HALO, :]
    c = abuf[...]
    mu = jnp.mean(c, axis=-1, keepdims=True)
    cc = c - mu
    var = jnp.mean(cc * cc, axis=-1, keepdims=True)
    cn = cc * lax.rsqrt(var + EPS) * lg_ref[...] + lb_ref[...]
    o_ref[:, 0:CONV_WIDTH] = (cn * _sigmoid(cn)).astype(o_ref.dtype)

    rbuf[LRU_HALO:LRU_HALO + ts, :] = u_ref[:, 2 * CONV_WIDTH:2 * CONV_WIDTH + LRU_WIDTH]
    _causal_depthwise_conv(rbuf, rw_ref, rb_ref, bbuf, ts, LRU_HALO, LRU_CONV_K)
    rbuf[0:LRU_HALO, :] = rbuf[ts:ts + LRU_HALO, :]
    for h in range(LRU_HEADS):
        cols = slice(h * LRU_HEAD_DIM, (h + 1) * LRU_HEAD_DIM)
        xh = bbuf[:, cols]
        xh_b = xh.astype(BF16)
        ga = jnp.dot(xh_b, wa_ref[h], preferred_element_type=F32) + ba_ref[:, cols]
        gx = jnp.dot(xh_b, wx_ref[h], preferred_element_type=F32) + bx_ref[:, cols]
        lam = lam_ref[:, cols]
        e = jnp.exp(-jnp.abs(lam))
        e1 = 1.0 + e
        log1p_e = jnp.where(e1 == 1.0, e, jnp.log(e1) * (e / (e1 - 1.0)))
        sp = jnp.maximum(-lam, 0.0) + log1p_e
        log_a = (-LRU_C) * _sigmoid(ga) * sp
        a = jnp.exp(log_a)
        mult = jnp.sqrt(-jnp.tanh(log_a) * (1.0 + a * a))
        row = lax.broadcasted_iota(jnp.int32, (ts, LRU_HEAD_DIM), 0)
        mult = jnp.where((row == 0) & (s == 0), 1.0, mult)
        abuf[:, cols] = a
        bbuf[:, cols] = mult * (_sigmoid(gx) * xh)

    row8 = lax.broadcasted_iota(jnp.int32, (SUBLANES, LRU_WIDTH), 0)
    h_prev = hc_ref[...]
    for g in range(ts // SUBLANES):
        rows = slice(g * SUBLANES, (g + 1) * SUBLANES)
        a = abuf[rows, :]
        b = bbuf[rows, :]
        for dsh in (1, 2, 4):
            a_s = pltpu.roll(a, dsh, axis=0)
            b_s = pltpu.roll(b, dsh, axis=0)
            m = row8 >= dsh
            b = jnp.where(m, a * b_s + b, b)
            a = jnp.where(m, a * a_s, a)
        hg = a * h_prev + b
        abuf[rows, :] = hg
        h_prev = jnp.broadcast_to(hg[SUBLANES - 1:SUBLANES, :], (SUBLANES, LRU_WIDTH))
    hc_ref[...] = h_prev

    rg = u_ref[:, 2 * CONV_WIDTH + LRU_WIDTH:D_IN]
    gelu = 0.5 * rg * (1.0 + jnp.tanh(0.7978845608028654 * (rg + 0.044715 * (rg * rg * rg))))
    o_ref[:, CONV_WIDTH:CONV_WIDTH + LRU_WIDTH] = (abuf[...] * gelu).astype(o_ref.dtype)


def _mixer_core(u, bsz, seq, cw, cb, lg, lb, rw, rb, wa, ba, wx, bx, lam, casts):
    t = u.shape[0]
    nsb = seq // TS_MIX
    row = lambda n: pl.BlockSpec((1, n), lambda b, s: (0, 0))
    cast_in, cast_out = _cast_specs(casts, bsz * nsb, lambda b, s: (b * nsb + s, 0))
    return pl.pallas_call(
        functools.partial(_mixer_kernel, n_cast=len(casts)),
        grid=(bsz, nsb),
        in_specs=[
            pl.BlockSpec((TS_MIX, D_IN), lambda b, s: (b * nsb + s, 0)),
            pl.BlockSpec((CONV_K, CONV_WIDTH), lambda b, s: (0, 0)),
            row(CONV_WIDTH), row(CONV_WIDTH), row(CONV_WIDTH),
            pl.BlockSpec((LRU_CONV_K, LRU_WIDTH), lambda b, s: (0, 0)),
            row(LRU_WIDTH),
            pl.BlockSpec((LRU_HEADS, LRU_HEAD_DIM, LRU_HEAD_DIM), lambda b, s: (0, 0, 0)),
            row(LRU_WIDTH),
            pl.BlockSpec((LRU_HEADS, LRU_HEAD_DIM, LRU_HEAD_DIM), lambda b, s: (0, 0, 0)),
            row(LRU_WIDTH), row(LRU_WIDTH),
            *cast_in,
        ],
        out_specs=[pl.BlockSpec((TS_MIX, CONV_WIDTH + LRU_WIDTH), lambda b, s: (b * nsb + s, 0)), *cast_out],
        out_shape=[jax.ShapeDtypeStruct((t, CONV_WIDTH + LRU_WIDTH), BF16), *_cast_shapes(casts)],
        scratch_shapes=[
            pltpu.VMEM((CONV_HALO + TS_MIX, CONV_WIDTH), F32),
            pltpu.VMEM((LRU_HALO + TS_MIX, LRU_WIDTH), F32),
            pltpu.VMEM((TS_MIX, LRU_WIDTH), F32),
            pltpu.VMEM((TS_MIX, LRU_WIDTH), F32),
            pltpu.VMEM((SUBLANES, LRU_WIDTH), F32),
        ],
        compiler_params=_cparams(("parallel", "arbitrary"), 56),
        name="mixer_core",
    )(u, cw, cb, lg, lb, rw, rb, wa, ba, wx, bx, lam, *_cast_args(casts))


def _top2_info(logits, width):
    lane = lax.broadcasted_iota(jnp.int32, logits.shape, 1)
    neg = jnp.float32(-jnp.inf)
    l1 = jnp.where(lane < N_EXPERTS, logits, neg)
    m1 = jnp.max(l1, axis=-1, keepdims=True)
    i1 = jnp.min(jnp.where(l1 == m1, lane, LANES), axis=-1, keepdims=True)
    l2 = jnp.where(lane == i1, neg, l1)
    m2 = jnp.max(l2, axis=-1, keepdims=True)
    i2 = jnp.min(jnp.where(l2 == m2, lane, LANES), axis=-1, keepdims=True)
    dlt = jnp.exp(m2 - m1)
    w1 = 1.0 / (1.0 + dlt)
    w2 = dlt / (1.0 + dlt)
    out = jnp.where(lane == 0, i1.astype(F32),
                    jnp.where(lane == 1, i2.astype(F32),
                              jnp.where(lane == 2, w1, jnp.where(lane == 3, w2, 0.0))))
    return out[:, 0:width]


def _out_proj_kernel(x_ref, y_ref, w_ref, *refs, n_cast, route):
    refs = list(refs)
    if route:
        gf_ref, wr_ref = refs[:2]
        refs = refs[2:]
    cast_in, refs = refs[:n_cast], refs[n_cast:]
    o_ref, refs = refs[0], refs[1:]
    if route:
        info_ref, refs = refs[0], refs[1:]
    cast_out = refs
    _run_casts(cast_in, cast_out)
    xn = x_ref[...] + jnp.dot(y_ref[...], w_ref[...], preferred_element_type=F32)
    o_ref[...] = xn
    if route:
        h = _rms_norm(xn, gf_ref[...]).astype(BF16)
        logits = jnp.dot(h, wr_ref[...], preferred_element_type=F32)
        info_ref[...] = _top2_info(logits, info_ref.shape[1])


def _out_proj(x, y, w, casts, router=None):
    t, d = x.shape
    k = y.shape[1]
    steps = t // TM_PROJ
    cast_in, cast_out = _cast_specs(casts, steps, lambda i: (i, 0))
    const = lambda shape: pl.BlockSpec(shape, lambda i: (0, 0))
    route_in = [] if router is None else [const((1, d)), const((d, LANES))]
    route_out = [] if router is None else [pl.BlockSpec((TM_PROJ, SUBLANES), lambda i: (i, 0))]
    route_shape = [] if router is None else [jax.ShapeDtypeStruct((t, SUBLANES), F32)]
    return pl.pallas_call(
        functools.partial(_out_proj_kernel, n_cast=len(casts), route=router is not None),
        grid=(steps,),
        in_specs=[
            pl.BlockSpec((TM_PROJ, d), lambda i: (i, 0)),
            pl.BlockSpec((TM_PROJ, k), lambda i: (i, 0)),
            const((k, d)),
            *route_in,
            *cast_in,
        ],
        out_specs=[pl.BlockSpec((TM_PROJ, d), lambda i: (i, 0)), *route_out, *cast_out],
        out_shape=[jax.ShapeDtypeStruct((t, d), F32), *route_shape, *_cast_shapes(casts)],
        compiler_params=_cparams(("parallel",), 56),
        name="out_proj",
    )(x, y, w, *(router or ()), *_cast_args(casts))


def _swiglu_step(h, wg, wu, wd):
    g = jnp.dot(h, wg, preferred_element_type=F32)
    v = jnp.dot(h, wu, preferred_element_type=F32)
    act = (g * _sigmoid(g) * v).astype(BF16)
    return jnp.dot(act, wd, preferred_element_type=F32)


def _dense_ffn_kernel(x_ref, g_ref, wg_ref, wu_ref, wd_ref, *refs, n_cast):
    cast_in, o_ref, cast_out, h_ref = refs[:n_cast], refs[n_cast], refs[n_cast + 1:2 * n_cast + 1], refs[-1]
    _run_casts(cast_in, cast_out)

    @pl.when(pl.program_id(1) == 0)
    def _():
        h_ref[...] = _rms_norm(x_ref[...], g_ref[...]).astype(BF16)
        o_ref[...] = x_ref[...]

    o_ref[...] += _swiglu_step(h_ref[...], wg_ref[...], wu_ref[...], wd_ref[...])


def _dense_ffn(x, g, wg, wu, wd, casts):
    t, d = x.shape
    ff = wg.shape[1]
    nf = ff // TF_FFN
    cast_in, cast_out = _cast_specs(casts, t // TM_FFN * nf, lambda i, f: (i * nf + f, 0))
    return pl.pallas_call(
        functools.partial(_dense_ffn_kernel, n_cast=len(casts)),
        grid=(t // TM_FFN, nf),
        in_specs=[
            pl.BlockSpec((TM_FFN, d), lambda i, f: (i, 0)),
            pl.BlockSpec((1, d), lambda i, f: (0, 0)),
            pl.BlockSpec((d, TF_FFN), lambda i, f: (0, f)),
            pl.BlockSpec((d, TF_FFN), lambda i, f: (0, f)),
            pl.BlockSpec((TF_FFN, d), lambda i, f: (f, 0)),
            *cast_in,
        ],
        out_specs=[pl.BlockSpec((TM_FFN, d), lambda i, f: (i, 0)), *cast_out],
        out_shape=[jax.ShapeDtypeStruct((t, d), F32), *_cast_shapes(casts)],
        scratch_shapes=[pltpu.VMEM((TM_FFN, d), BF16)],
        compiler_params=_cparams(("parallel", "arbitrary"), 60),
        name="dense_ffn",
    )(x, g, wg, wu, wd, *_cast_args(casts))


def _moe_ffn_kernel(tok_ref, dst_ref, be_ref, nr_ref, x_hbm, g_ref, wg_ref, wu_ref, wd_ref, y_hbm,
                    xg_ref, h_ref, acc_ref, gsem, ssem, *, n_tok):
    b = pl.program_id(0)
    f = pl.program_id(1)
    nb = pl.num_programs(0)
    nf = pl.num_programs(1)
    n_real = nr_ref[0]
    real = b < n_real
    slot = b % 2
    rows_per_step = MOE_BM // MOE_ISSUE_STEPS

    def gather_start(tok, r, slt):
        pltpu.make_async_copy(x_hbm.at[pl.ds(tok, 1), :], xg_ref.at[slt, pl.ds(r, 1), :],
                              gsem.at[slt]).start()

    def scatter_start(a, r, slt):
        pltpu.make_async_copy(acc_ref.at[slt, pl.ds(r, 1), :], y_hbm.at[pl.ds(a, 1), :], ssem).start()

    def gather_wait(slt):
        pltpu.make_async_copy(x_hbm.at[pl.ds(0, MOE_BM), :], xg_ref.at[slt], gsem.at[slt]).wait()

    def scatter_wait():
        pltpu.make_async_copy(acc_ref.at[0], y_hbm.at[pl.ds(0, MOE_BM), :], ssem).wait()

    def drain(blk):
        gather_wait((blk + 1) % 2)

        def body(r, carry):
            scatter_start(dst_ref[blk * MOE_BM + r], r, blk % 2)
            return carry

        lax.fori_loop(0, MOE_BM, body, 0, unroll=8)
        scatter_wait()

    @pl.when(real & (f == 0))
    def _():
        @pl.when(b == 0)
        def _():
            def body(r, carry):
                gather_start(tok_ref[r], r, 0)
                return carry

            lax.fori_loop(0, MOE_BM, body, 0, unroll=8)
            acc_ref[1] = jnp.zeros(acc_ref.shape[1:], acc_ref.dtype)

        gather_wait(slot)
        h_ref[...] = _rms_norm(xg_ref[slot], g_ref[...]).astype(BF16)

    @pl.when(real)
    def _():
        y = _swiglu_step(h_ref[...], wg_ref[0], wu_ref[0], wd_ref[0])

        @pl.when(f == 0)
        def _():
            acc_ref[slot] = y

        @pl.when(f > 0)
        def _():
            acc_ref[slot] += y

        @pl.when(f < MOE_ISSUE_STEPS)
        def _():
            prev_base = jnp.maximum(b - 1, 0) * MOE_BM
            for i in range(rows_per_step):
                r = f * rows_per_step + i
                gather_start(tok_ref[(b + 1) * MOE_BM + r], r, 1 - slot)
                a_prev = jnp.where(b == 0, TOP_K * n_tok + r, dst_ref[prev_base + r])
                scatter_start(a_prev, r, 1 - slot)

        @pl.when(f == nf - 1)
        def _():
            scatter_wait()

    @pl.when((b == n_real) & (f == 0))
    def _():
        drain(b - 1)

    @pl.when(real & (b == nb - 1) & (f == nf - 1))
    def _():
        drain(b)


def _moe_ffn(tok, dst, block_e, n_real, x, g, wg, wu, wd):
    t, d = x.shape
    ff = wg.shape[2]
    nf = ff // TF_MOE
    n_blocks = block_e.shape[0]
    assert nf > MOE_ISSUE_STEPS and MOE_BM % MOE_ISSUE_STEPS == 0
    assert tok.shape[0] == (n_blocks + 1) * MOE_BM == dst.shape[0]

    def f_eff(b, f, nr):
        return jnp.where(b < nr[0], f, nf - 1)

    grid_spec = pltpu.PrefetchScalarGridSpec(
        num_scalar_prefetch=4,
        grid=(n_blocks, nf),
        in_specs=[
            pl.BlockSpec(memory_space=pl.ANY),
            pl.BlockSpec((1, d), lambda b, f, tok, dst, be, nr: (0, 0)),
            pl.BlockSpec((1, d, TF_MOE), lambda b, f, tok, dst, be, nr: (be[b], 0, f_eff(b, f, nr))),
            pl.BlockSpec((1, d, TF_MOE), lambda b, f, tok, dst, be, nr: (be[b], 0, f_eff(b, f, nr))),
            pl.BlockSpec((1, TF_MOE, d), lambda b, f, tok, dst, be, nr: (be[b], f_eff(b, f, nr), 0)),
        ],
        out_specs=pl.BlockSpec(memory_space=pl.ANY),
        scratch_shapes=[
            pltpu.VMEM((2, MOE_BM, d), F32),
            pltpu.VMEM((MOE_BM, d), BF16),
            pltpu.VMEM((2, MOE_BM, d), F32),
            pltpu.SemaphoreType.DMA((2,)),
            pltpu.SemaphoreType.DMA(()),
        ],
    )
    return pl.pallas_call(
        functools.partial(_moe_ffn_kernel, n_tok=t),
        grid_spec=grid_spec,
        out_shape=jax.ShapeDtypeStruct((TOP_K * t + MOE_BM, d), F32),
        compiler_params=_cparams(("arbitrary", "arbitrary"), 56),
        name="moe_ffn",
    )(tok, dst, block_e, n_real, x, g, wg, wu, wd)


def _combine_kernel(x_ref, info_ref, g_ref, ya_ref, yb_ref, o_ref, *, final_norm):
    w1 = info_ref[:, 2:3]
    w2 = info_ref[:, 3:4]
    z = x_ref[...] + (ya_ref[...] * w1 + yb_ref[...] * w2)
    o_ref[...] = _rms_norm(z, g_ref[...]) if final_norm else z


def _combine(x, info, g, y, final_norm):
    t, d = x.shape
    nt = t // TM_COMBINE
    return pl.pallas_call(
        functools.partial(_combine_kernel, final_norm=final_norm),
        grid=(nt,),
        in_specs=[
            pl.BlockSpec((TM_COMBINE, d), lambda i: (i, 0)),
            pl.BlockSpec((TM_COMBINE, SUBLANES), lambda i: (i, 0)),
            pl.BlockSpec((1, d), lambda i: (0, 0)),
            pl.BlockSpec((TM_COMBINE, d), lambda i: (i, 0)),
            pl.BlockSpec((TM_COMBINE, d), lambda i: (nt + i, 0)),
        ],
        out_specs=pl.BlockSpec((TM_COMBINE, d), lambda i: (i, 0)),
        out_shape=jax.ShapeDtypeStruct((t, d), F32),
        compiler_params=_cparams(("parallel",), 48),
        name="moe_combine",
    )(x, info, g, y, y)


def _routing_plan(info, n_tok):
    n_asg = n_tok * TOP_K
    n_blocks = n_asg // MOE_BM + N_EXPERTS
    flat_e = info[:, 0:TOP_K].astype(jnp.int32).reshape(n_asg)
    onehot = (flat_e[:, None] == jnp.arange(N_EXPERTS, dtype=jnp.int32)[None, :]).astype(jnp.int32)
    csum = jnp.cumsum(onehot, axis=0)
    counts = csum[-1]
    padded = ((counts + MOE_BM - 1) // MOE_BM) * MOE_BM
    p_ends = jnp.cumsum(padded)
    p_starts = p_ends - padded
    pos = jnp.sum(onehot * (csum - 1 + p_starts[None, :]), axis=1).astype(jnp.int32)
    flat = jnp.arange(n_asg, dtype=jnp.int32)
    dest = (flat % TOP_K) * n_tok + flat // TOP_K
    spare = n_asg + jnp.arange((n_blocks + 1) * MOE_BM, dtype=jnp.int32) % MOE_BM
    dst = spare.at[pos].set(dest)
    tok = jnp.where(dst >= n_asg, 0, dst % n_tok)
    block_start = jnp.arange(n_blocks, dtype=jnp.int32) * MOE_BM
    block_e = jnp.minimum(jnp.searchsorted(p_ends, block_start, side="right"), N_EXPERTS - 1).astype(jnp.int32)
    n_real = (p_ends[-1] // MOE_BM).astype(jnp.int32).reshape(1)
    return tok, dst, block_e, n_real


def kernel(x, mix_norm, w_in, conv_w, conv_b, conv_ln_g, conv_ln_b, lru_conv_w, lru_conv_b, lru_wa, lru_ba, lru_wx, lru_bx, lru_lambda, w_out, ffn_norm, dense_wg, dense_wu, dense_wd, w_router, moe_wg, moe_wu, moe_wd, final_norm):
    bsz, seq, d = x.shape
    depth = w_in.shape[0]
    t = bsz * seq
    xt = x.reshape(t, d)
    row = lambda v: v.reshape(1, -1)
    assert depth % 2 == 0, "the final RMSNorm is fused into the last routed layer's combine"
    fold = lambda w: w.reshape(w.shape[0], -1, w.shape[-1])
    w_in_b = w_in[0].astype(BF16)
    for layer in range(0, depth, 2):
        j = layer // 2
        odd = layer + 1
        u, wg_d, wu_d, w_out_b = _in_proj(xt, row(mix_norm[layer]), w_in_b,
                                           [(dense_wg, j), (dense_wu, j), (w_out, layer)])
        y, moe_wg_b, wd_d = _mixer_core(
            u, bsz, seq, conv_w[layer], row(conv_b[layer]), row(conv_ln_g[layer]), row(conv_ln_b[layer]),
            lru_conv_w[layer], row(lru_conv_b[layer]), lru_wa[layer].astype(BF16), row(lru_ba[layer]),
            lru_wx[layer].astype(BF16), row(lru_bx[layer]), row(lru_lambda[layer]),
            [(fold(moe_wg), j), (dense_wd, j)])
        xt, w_in_odd = _out_proj(xt, y, w_out_b, [(w_in, odd)])
        xt, moe_wd_b = _dense_ffn(xt, row(ffn_norm[layer]), wg_d, wu_d, wd_d, [(fold(moe_wd), j)])

        hosted = [(w_out, odd)] + ([(w_in, odd + 1)] if odd + 1 < depth else [])
        u, w_out_b, *nxt = _in_proj(xt, row(mix_norm[odd]), w_in_odd, hosted)
        if nxt:
            w_in_b = nxt[0]
        y, moe_wu_b = _mixer_core(
            u, bsz, seq, conv_w[odd], row(conv_b[odd]), row(conv_ln_g[odd]), row(conv_ln_b[odd]),
            lru_conv_w[odd], row(lru_conv_b[odd]), lru_wa[odd].astype(BF16), row(lru_ba[odd]),
            lru_wx[odd].astype(BF16), row(lru_bx[odd]), row(lru_lambda[odd]), [(fold(moe_wu), j)])
        wr = jnp.zeros((d, LANES), BF16).at[:, 0:N_EXPERTS].set(w_router[j].astype(BF16))
        g = row(ffn_norm[odd])
        xt, info = _out_proj(xt, y, w_out_b, [], router=(g, wr))
        tok, dst, block_e, n_real = _routing_plan(info, t)
        y = _moe_ffn(tok, dst, block_e, n_real, xt, g, moe_wg_b.reshape(moe_wg[j].shape),
                     moe_wu_b.reshape(moe_wu[j].shape), moe_wd_b.reshape(moe_wd[j].shape))
        xt = _combine(xt, info, row(final_norm), y, odd == depth - 1)
    return xt.reshape(bsz, seq, d)
```

```python
import functools

import jax
import jax.numpy as jnp
from jax import lax
from jax.experimental import pallas as pl
from jax.experimental.pallas import tpu as pltpu

F32 = jnp.float32
BF16 = jnp.bfloat16

D_MODEL = 2048
CONV_WIDTH = 1024
LRU_WIDTH = 1024
LRU_HEADS = 8
LRU_HEAD_DIM = LRU_WIDTH // LRU_HEADS
D_IN = 2 * CONV_WIDTH + 2 * LRU_WIDTH
CONV_K = 31
LRU_CONV_K = 4
LRU_C = 8.0
N_EXPERTS = 8
TOP_K = 2
EPS = 1e-6

LANES = 128
SUBLANES = 8
BF16_ROWS = 16
MIB = 1024 * 1024

TM_PROJ = 512
TN_IN = 1024
TS_MIX = 256
CONV_HALO = 32
LRU_HALO = 8
CONV_ROWS = 64
TM_FFN = 512
TF_FFN = 1024
TF_MOE = 768
MOE_ISSUE_STEPS = 8
MOE_BM = 512
TM_COMBINE = 512


def _cparams(semantics, vmem_mib):
    return pltpu.CompilerParams(dimension_semantics=semantics, vmem_limit_bytes=vmem_mib * MIB)


def _sigmoid(x):
    return 0.5 * jnp.tanh(0.5 * x) + 0.5


def _rms_norm(x, g):
    ms = jnp.mean(x * x, axis=-1, keepdims=True)
    return x * lax.rsqrt(ms + EPS) * g


def _cast_specs(casts, n_steps, index_map):
    in_specs, out_specs = [], []
    for w, layer in casts:
        _, rows, cols = w.shape
        assert rows % n_steps == 0 and (rows // n_steps) % BF16_ROWS == 0, (w.shape, n_steps)
        in_specs.append(pl.BlockSpec((None, rows // n_steps, cols),
                                     lambda *idx, layer=layer: (layer, *index_map(*idx))))
        out_specs.append(pl.BlockSpec((rows // n_steps, cols), index_map))
    return in_specs, out_specs


def _cast_shapes(casts):
    return [jax.ShapeDtypeStruct(w.shape[1:], BF16) for w, _ in casts]


def _cast_args(casts):
    return [w for w, _ in casts]


def _run_casts(in_refs, out_refs):
    for src, dst in zip(in_refs, out_refs, strict=True):
        dst[...] = src[...].astype(dst.dtype)


def _in_proj_kernel(x_ref, g_ref, w_ref, *refs, n_cast):
    cast_in, o_ref, cast_out = refs[:n_cast], refs[n_cast], refs[n_cast + 1:]
    _run_casts(cast_in, cast_out)
    h = _rms_norm(x_ref[...], g_ref[...]).astype(BF16)
    for j in range(o_ref.shape[1] // TN_IN):
        cols = slice(j * TN_IN, (j + 1) * TN_IN)
        o_ref[:, cols] = jnp.dot(h, w_ref[:, cols], preferred_element_type=F32)


def _in_proj(x, g, w, casts):
    t, d = x.shape
    n = w.shape[1]
    steps = t // TM_PROJ
    cast_in, cast_out = _cast_specs(casts, steps, lambda i: (i, 0))
    return pl.pallas_call(
        functools.partial(_in_proj_kernel, n_cast=len(casts)),
        grid=(steps,),
        in_specs=[
            pl.BlockSpec((TM_PROJ, d), lambda i: (i, 0)),
            pl.BlockSpec((1, d), lambda i: (0, 0)),
            pl.BlockSpec((d, n), lambda i: (0, 0), pipeline_mode=pl.Buffered(1)),
            *cast_in,
        ],
        out_specs=[pl.BlockSpec((TM_PROJ, n), lambda i: (i, 0)), *cast_out],
        out_shape=[jax.ShapeDtypeStruct((t, n), F32), *_cast_shapes(casts)],
        compiler_params=_cparams(("parallel",), 60),
        name="in_proj",
    )(x, g, w, *_cast_args(casts))


def _causal_depthwise_conv(buf, w_ref, b_ref, out_ref, ts, halo, n_taps):
    base = halo - (n_taps - 1)
    nt = CONV_ROWS // SUBLANES
    nq = (base + n_taps - 1) // SUBLANES + 1
    row8 = lax.broadcasted_iota(jnp.int32, (SUBLANES, LANES), 0)
    for rc in range(ts // CONV_ROWS):
        t0 = rc * CONV_ROWS
        for lc in range(buf.shape[1] // LANES):
            cols = slice(lc * LANES, (lc + 1) * LANES)
            xs = [buf[t0 + SUBLANES * j:t0 + SUBLANES * (j + 1), cols] for j in range(nt + nq - 1)]
            bias = jnp.broadcast_to(b_ref[:, cols], (SUBLANES, LANES))
            out = [bias] * nt
            for p in range(SUBLANES):
                taps = [(q, SUBLANES * q + p - base) for q in range(nq)
                        if 0 <= SUBLANES * q + p - base < n_taps]
                if not taps:
                    continue
                ws = [jnp.broadcast_to(w_ref[k:k + 1, cols], (SUBLANES, LANES)) for _, k in taps]
                zs = []
                for j in range(nt + (1 if p else 0)):
                    z = ws[0] * xs[j + taps[0][0]]
                    for w, (q, _) in zip(ws[1:], taps[1:]):
                        z = z + w * xs[j + q]
                    zs.append(z)
                if p == 0:
                    out = [o + z for o, z in zip(out, zs)]
                else:
                    rs = [pltpu.roll(z, SUBLANES - p, axis=0) for z in zs]
                    keep = row8 < SUBLANES - p
                    out = [o + jnp.where(keep, rs[g], rs[g + 1]) for g, o in enumerate(out)]
            for g in range(nt):
                out_ref[t0 + SUBLANES * g:t0 + SUBLANES * (g + 1), cols] = out[g]


def _mixer_kernel(u_ref, cw_ref, cb_ref, lg_ref, lb_ref, rw_ref, rb_ref, wa_ref, ba_ref,
                  wx_ref, bx_ref, lam_ref, *refs, n_cast):
    cast_in, o_ref, cast_out = refs[:n_cast], refs[n_cast], refs[n_cast + 1:2 * n_cast + 1]
    cbuf, rbuf, abuf, bbuf, hc_ref = refs[2 * n_cast + 1:]
    s = pl.program_id(1)
    ts = TS_MIX
    _run_casts(cast_in, cast_out)

    @pl.when(s == 0)
    def _():
        cbuf[0:CONV_HALO, :] = jnp.zeros((CONV_HALO, CONV_WIDTH), F32)
        rbuf[0:LRU_HALO, :] = jnp.zeros((LRU_HALO, LRU_WIDTH), F32)
        hc_ref[...] = jnp.zeros((SUBLANES, LRU_WIDTH), F32)

    val = u_ref[:, 0:CONV_WIDTH]
    gate = u_ref[:, CONV_WIDTH:2 * CONV_WIDTH]
    cbuf[CONV_HALO:CONV_HALO + ts, :] = val * _sigmoid(gate)
    _causal_depthwise_conv(cbuf, cw_ref, cb_ref, abuf, ts, CONV_HALO, CONV_K)
    cbuf[0:CONV_HALO, :] = cbuf[ts:ts + CONV_HALO, :]
    c = abuf[...]
    mu = jnp.mean(c, axis=-1, keepdims=True)
    cc = c - mu
    var = jnp.mean(cc * cc, axis=-1, keepdims=True)
    cn = cc * lax.rsqrt(var + EPS) * lg_ref[...] + lb_ref[...]
    o_ref[:, 0:CONV_WIDTH] = (cn * _sigmoid(cn)).astype(o_ref.dtype)

    rbuf[LRU_HALO:LRU_HALO + ts, :] = u_ref[:, 2 * CONV_WIDTH:2 * CONV_WIDTH + LRU_WIDTH]
    _causal_depthwise_conv(rbuf, rw_ref, rb_ref, bbuf, ts, LRU_HALO, LRU_CONV_K)
    rbuf[0:LRU_HALO, :] = rbuf[ts:ts + LRU_HALO, :]
    for h in range(LRU_HEADS):
        cols = slice(h * LRU_HEAD_DIM, (h + 1) * LRU_HEAD_DIM)
        xh = bbuf[:, cols]
        xh_b = xh.astype(BF16)
        ga = jnp.dot(xh_b, wa_ref[h], preferred_element_type=F32) + ba_ref[:, cols]
        gx = jnp.dot(xh_b, wx_ref[h], preferred_element_type=F32) + bx_ref[:, cols]
        lam = lam_ref[:, cols]
        e = jnp.exp(-jnp.abs(lam))
        e1 = 1.0 + e
        log1p_e = jnp.where(e1 == 1.0, e, jnp.log(e1) * (e / (e1 - 1.0)))
        sp = jnp.maximum(-lam, 0.0) + log1p_e
        log_a = (-LRU_C) * _sigmoid(ga) * sp
        a = jnp.exp(log_a)
        mult = jnp.sqrt(-jnp.tanh(log_a) * (1.0 + a * a))
        row = lax.broadcasted_iota(jnp.int32, (ts, LRU_HEAD_DIM), 0)
        mult = jnp.where((row == 0) & (s == 0), 1.0, mult)
        abuf[:, cols] = a
        bbuf[:, cols] = mult * (_sigmoid(gx) * xh)

    row8 = lax.broadcasted_iota(jnp.int32, (SUBLANES, LRU_WIDTH), 0)
    h_prev = hc_ref[...]
    for g in range(ts // SUBLANES):
        rows = slice(g * SUBLANES, (g + 1) * SUBLANES)
        a = abuf[rows, :]
        b = bbuf[rows, :]
        for dsh in (1, 2, 4):
            a_s = pltpu.roll(a, dsh, axis=0)
            b_s = pltpu.roll(b, dsh, axis=0)
            m = row8 >= dsh
            b = jnp.where(m, a * b_s + b, b)
            a = jnp.where(m, a * a_s, a)
        hg = a * h_prev + b
        abuf[rows, :] = hg
        h_prev = jnp.broadcast_to(hg[SUBLANES - 1:SUBLANES, :], (SUBLANES, LRU_WIDTH))
    hc_ref[...] = h_prev

    rg = u_ref[:, 2 * CONV_WIDTH + LRU_WIDTH:D_IN]
    gelu = 0.5 * rg * (1.0 + jnp.tanh(0.7978845608028654 * (rg + 0.044715 * (rg * rg * rg))))
    o_ref[:, CONV_WIDTH:CONV_WIDTH + LRU_WIDTH] = (abuf[...] * gelu).astype(o_ref.dtype)


def _mixer_core(u, bsz, seq, cw, cb, lg, lb, rw, rb, wa, ba, wx, bx, lam, casts):
    t = u.shape[0]
    nsb = seq // TS_MIX
    row = lambda n: pl.BlockSpec((1, n), lambda b, s: (0, 0))
    cast_in, cast_out = _cast_specs(casts, bsz * nsb, lambda b, s: (b * nsb + s, 0))
    return pl.pallas_call(
        functools.partial(_mixer_kernel, n_cast=len(casts)),
        grid=(bsz, nsb),
        in_specs=[
            pl.BlockSpec((TS_MIX, D_IN), lambda b, s: (b * nsb + s, 0)),
            pl.BlockSpec((CONV_K, CONV_WIDTH), lambda b, s: (0, 0)),
            row(CONV_WIDTH), row(CONV_WIDTH), row(CONV_WIDTH),
            pl.BlockSpec((LRU_CONV_K, LRU_WIDTH), lambda b, s: (0, 0)),
            row(LRU_WIDTH),
            pl.BlockSpec((LRU_HEADS, LRU_HEAD_DIM, LRU_HEAD_DIM), lambda b, s: (0, 0, 0)),
            row(LRU_WIDTH),
            pl.BlockSpec((LRU_HEADS, LRU_HEAD_DIM, LRU_HEAD_DIM), lambda b, s: (0, 0, 0)),
            row(LRU_WIDTH), row(LRU_WIDTH),
            *cast_in,
        ],
        out_specs=[pl.BlockSpec((TS_MIX, CONV_WIDTH + LRU_WIDTH), lambda b, s: (b * nsb + s, 0)), *cast_out],
        out_shape=[jax.ShapeDtypeStruct((t, CONV_WIDTH + LRU_WIDTH), BF16), *_cast_shapes(casts)],
        scratch_shapes=[
            pltpu.VMEM((CONV_HALO + TS_MIX, CONV_WIDTH), F32),
            pltpu.VMEM((LRU_HALO + TS_MIX, LRU_WIDTH), F32),
            pltpu.VMEM((TS_MIX, LRU_WIDTH), F32),
            pltpu.VMEM((TS_MIX, LRU_WIDTH), F32),
            pltpu.VMEM((SUBLANES, LRU_WIDTH), F32),
        ],
        compiler_params=_cparams(("parallel", "arbitrary"), 56),
        name="mixer_core",
    )(u, cw, cb, lg, lb, rw, rb, wa, ba, wx, bx, lam, *_cast_args(casts))


def _top2_info(logits, width):
    lane = lax.broadcasted_iota(jnp.int32, logits.shape, 1)
    neg = jnp.float32(-jnp.inf)
    l1 = jnp.where(lane < N_EXPERTS, logits, neg)
    m1 = jnp.max(l1, axis=-1, keepdims=True)
    i1 = jnp.min(jnp.where(l1 == m1, lane, LANES), axis=-1, keepdims=True)
    l2 = jnp.where(lane == i1, neg, l1)
    m2 = jnp.max(l2, axis=-1, keepdims=True)
    i2 = jnp.min(jnp.where(l2 == m2, lane, LANES), axis=-1, keepdims=True)
    dlt = jnp.exp(m2 - m1)
    w1 = 1.0 / (1.0 + dlt)
    w2 = dlt / (1.0 + dlt)
    out = jnp.where(lane == 0, i1.astype(F32),
                    jnp.where(lane == 1, i2.astype(F32),
                              jnp.where(lane == 2, w1, jnp.where(lane == 3, w2, 0.0))))
    return out[:, 0:width]


def _out_proj_kernel(x_ref, y_ref, w_ref, *refs, n_cast, route):
    refs = list(refs)
    if route:
        gf_ref, wr_ref = refs[:2]
        refs = refs[2:]
    cast_in, refs = refs[:n_cast], refs[n_cast:]
    o_ref, refs = refs[0], refs[1:]
    if route:
        info_ref, refs = refs[0], refs[1:]
    cast_out = refs
    _run_casts(cast_in, cast_out)
    xn = x_ref[...] + jnp.dot(y_ref[...], w_ref[...], preferred_element_type=F32)
    o_ref[...] = xn
    if route:
        h = _rms_norm(xn, gf_ref[...]).astype(BF16)
        logits = jnp.dot(h, wr_ref[...], preferred_element_type=F32)
        info_ref[...] = _top2_info(logits, info_ref.shape[1])


def _out_proj(x, y, w, casts, router=None):
    t, d = x.shape
    k = y.shape[1]
    steps = t // TM_PROJ
    cast_in, cast_out = _cast_specs(casts, steps, lambda i: (i, 0))
    const = lambda shape: pl.BlockSpec(shape, lambda i: (0, 0))
    route_in = [] if router is None else [const((1, d)), const((d, LANES))]
    route_out = [] if router is None else [pl.BlockSpec((TM_PROJ, SUBLANES), lambda i: (i, 0))]
    route_shape = [] if router is None else [jax.ShapeDtypeStruct((t, SUBLANES), F32)]
    return pl.pallas_call(
        functools.partial(_out_proj_kernel, n_cast=len(casts), route=router is not None),
        grid=(steps,),
        in_specs=[
            pl.BlockSpec((TM_PROJ, d), lambda i: (i, 0)),
            pl.BlockSpec((TM_PROJ, k), lambda i: (i, 0)),
            const((k, d)),
            *route_in,
            *cast_in,
        ],
        out_specs=[pl.BlockSpec((TM_PROJ, d), lambda i: (i, 0)), *route_out, *cast_out],
        out_shape=[jax.ShapeDtypeStruct((t, d), F32), *route_shape, *_cast_shapes(casts)],
        compiler_params=_cparams(("parallel",), 56),
        name="out_proj",
    )(x, y, w, *(router or ()), *_cast_args(casts))


def _swiglu_step(h, wg, wu, wd):
    g = jnp.dot(h, wg, preferred_element_type=F32)
    v = jnp.dot(h, wu, preferred_element_type=F32)
    act = (g * _sigmoid(g) * v).astype(BF16)
    return jnp.dot(act, wd, preferred_element_type=F32)


def _dense_ffn_kernel(x_ref, g_ref, wg_ref, wu_ref, wd_ref, *refs, n_cast):
    cast_in, o_ref, cast_out, h_ref = refs[:n_cast], refs[n_cast], refs[n_cast + 1:2 * n_cast + 1], refs[-1]
    _run_casts(cast_in, cast_out)

    @pl.when(pl.program_id(1) == 0)
    def _():
        h_ref[...] = _rms_norm(x_ref[...], g_ref[...]).astype(BF16)
        o_ref[...] = x_ref[...]

    o_ref[...] += _swiglu_step(h_ref[...], wg_ref[...], wu_ref[...], wd_ref[...])


def _dense_ffn(x, g, wg, wu, wd, casts):
    t, d = x.shape
    ff = wg.shape[1]
    nf = ff // TF_FFN
    cast_in, cast_out = _cast_specs(casts, t // TM_FFN * nf, lambda i, f: (i * nf + f, 0))
    return pl.pallas_call(
        functools.partial(_dense_ffn_kernel, n_cast=len(casts)),
        grid=(t // TM_FFN, nf),
        in_specs=[
            pl.BlockSpec((TM_FFN, d), lambda i, f: (i, 0)),
            pl.BlockSpec((1, d), lambda i, f: (0, 0)),
            pl.BlockSpec((d, TF_FFN), lambda i, f: (0, f)),
            pl.BlockSpec((d, TF_FFN), lambda i, f: (0, f)),
            pl.BlockSpec((TF_FFN, d), lambda i, f: (f, 0)),
            *cast_in,
        ],
        out_specs=[pl.BlockSpec((TM_FFN, d), lambda i, f: (i, 0)), *cast_out],
        out_shape=[jax.ShapeDtypeStruct((t, d), F32), *_cast_shapes(casts)],
        scratch_shapes=[pltpu.VMEM((TM_FFN, d), BF16)],
        compiler_params=_cparams(("parallel", "arbitrary"), 60),
        name="dense_ffn",
    )(x, g, wg, wu, wd, *_cast_args(casts))


def _moe_ffn_kernel(tok_ref, dst_ref, be_ref, nr_ref, x_hbm, g_ref, wg_ref, wu_ref, wd_ref, y_hbm,
                    xg_ref, h_ref, acc_ref, gsem, ssem, *, n_tok):
    b = pl.program_id(0)
    f = pl.program_id(1)
    nb = pl.num_programs(0)
    nf = pl.num_programs(1)
    n_real = nr_ref[0]
    real = b < n_real
    slot = b % 2
    rows_per_step = MOE_BM // MOE_ISSUE_STEPS

    def gather_start(tok, r, slt):
        pltpu.make_async_copy(x_hbm.at[pl.ds(tok, 1), :], xg_ref.at[slt, pl.ds(r, 1), :],
                              gsem.at[slt]).start()

    def scatter_start(a, r, slt):
        pltpu.make_async_copy(acc_ref.at[slt, pl.ds(r, 1), :], y_hbm.at[pl.ds(a, 1), :], ssem).start()

    def gather_wait(slt):
        pltpu.make_async_copy(x_hbm.at[pl.ds(0, MOE_BM), :], xg_ref.at[slt], gsem.at[slt]).wait()

    def scatter_wait():
        pltpu.make_async_copy(acc_ref.at[0], y_hbm.at[pl.ds(0, MOE_BM), :], ssem).wait()

    def drain(blk):
        gather_wait((blk + 1) % 2)

        def body(r, carry):
            scatter_start(dst_ref[blk * MOE_BM + r], r, blk % 2)
            return carry

        lax.fori_loop(0, MOE_BM, body, 0, unroll=8)
        scatter_wait()

    @pl.when(real & (f == 0))
    def _():
        @pl.when(b == 0)
        def _():
            def body(r, carry):
                gather_start(tok_ref[r], r, 0)
                return carry

            lax.fori_loop(0, MOE_BM, body, 0, unroll=8)
            acc_ref[1] = jnp.zeros(acc_ref.shape[1:], acc_ref.dtype)

        gather_wait(slot)
        h_ref[...] = _rms_norm(xg_ref[slot], g_ref[...]).astype(BF16)

    @pl.when(real)
    def _():
        y = _swiglu_step(h_ref[...], wg_ref[0], wu_ref[0], wd_ref[0])

        @pl.when(f == 0)
        def _():
            acc_ref[slot] = y

        @pl.when(f > 0)
        def _():
            acc_ref[slot] += y

        @pl.when(f < MOE_ISSUE_STEPS)
        def _():
            prev_base = jnp.maximum(b - 1, 0) * MOE_BM
            for i in range(rows_per_step):
                r = f * rows_per_step + i
                gather_start(tok_ref[(b + 1) * MOE_BM + r], r, 1 - slot)
                a_prev = jnp.where(b == 0, TOP_K * n_tok + r, dst_ref[prev_base + r])
                scatter_start(a_prev, r, 1 - slot)

        @pl.when(f == nf - 1)
        def _():
            scatter_wait()

    @pl.when((b == n_real) & (f == 0))
    def _():
        drain(b - 1)

    @pl.when(real & (b == nb - 1) & (f == nf - 1))
    def _():
        drain(b)


def _moe_ffn(tok, dst, block_e, n_real, x, g, wg, wu, wd):
    t, d = x.shape
    ff = wg.shape[2]
    nf = ff // TF_MOE
    n_blocks = block_e.shape[0]
    assert nf >= MOE_ISSUE_STEPS and MOE_BM % MOE_ISSUE_STEPS == 0
    assert tok.shape[0] == (n_blocks + 1) * MOE_BM == dst.shape[0]

    def f_eff(b, f, nr):
        return jnp.where(b < nr[0], f, nf - 1)

    grid_spec = pltpu.PrefetchScalarGridSpec(
        num_scalar_prefetch=4,
        grid=(n_blocks, nf),
        in_specs=[
            pl.BlockSpec(memory_space=pl.ANY),
            pl.BlockSpec((1, d), lambda b, f, tok, dst, be, nr: (0, 0)),
            pl.BlockSpec((1, d, TF_MOE), lambda b, f, tok, dst, be, nr: (be[b], 0, f_eff(b, f, nr))),
            pl.BlockSpec((1, d, TF_MOE), lambda b, f, tok, dst, be, nr: (be[b], 0, f_eff(b, f, nr))),
            pl.BlockSpec((1, TF_MOE, d), lambda b, f, tok, dst, be, nr: (be[b], f_eff(b, f, nr), 0)),
        ],
        out_specs=pl.BlockSpec(memory_space=pl.ANY),
        scratch_shapes=[
            pltpu.VMEM((2, MOE_BM, d), F32),
            pltpu.VMEM((MOE_BM, d), BF16),
            pltpu.VMEM((2, MOE_BM, d), F32),
            pltpu.SemaphoreType.DMA((2,)),
            pltpu.SemaphoreType.DMA(()),
        ],
    )
    return pl.pallas_call(
        functools.partial(_moe_ffn_kernel, n_tok=t),
        grid_spec=grid_spec,
        out_shape=jax.ShapeDtypeStruct((TOP_K * t + MOE_BM, d), F32),
        compiler_params=_cparams(("arbitrary", "arbitrary"), 56),
        name="moe_ffn",
    )(tok, dst, block_e, n_real, x, g, wg, wu, wd)


def _combine_kernel(x_ref, info_ref, g_ref, ya_ref, yb_ref, o_ref, *, final_norm):
    w1 = info_ref[:, 2:3]
    w2 = info_ref[:, 3:4]
    z = x_ref[...] + (ya_ref[...] * w1 + yb_ref[...] * w2)
    o_ref[...] = _rms_norm(z, g_ref[...]) if final_norm else z


def _combine(x, info, g, y, final_norm):
    t, d = x.shape
    nt = t // TM_COMBINE
    return pl.pallas_call(
        functools.partial(_combine_kernel, final_norm=final_norm),
        grid=(nt,),
        in_specs=[
            pl.BlockSpec((TM_COMBINE, d), lambda i: (i, 0)),
            pl.BlockSpec((TM_COMBINE, SUBLANES), lambda i: (i, 0)),
            pl.BlockSpec((1, d), lambda i: (0, 0)),
            pl.BlockSpec((TM_COMBINE, d), lambda i: (i, 0)),
            pl.BlockSpec((TM_COMBINE, d), lambda i: (nt + i, 0)),
        ],
        out_specs=pl.BlockSpec((TM_COMBINE, d), lambda i: (i, 0)),
        out_shape=jax.ShapeDtypeStruct((t, d), F32),
        compiler_params=_cparams(("parallel",), 48),
        name="moe_combine",
    )(x, info, g, y, y)


def _routing_plan(info, n_tok):
    n_asg = n_tok * TOP_K
    n_blocks = n_asg // MOE_BM + N_EXPERTS
    flat_e = info[:, 0:TOP_K].astype(jnp.int32).reshape(n_asg)
    onehot = (flat_e[:, None] == jnp.arange(N_EXPERTS, dtype=jnp.int32)[None, :]).astype(jnp.int32)
    csum = jnp.cumsum(onehot, axis=0)
    counts = csum[-1]
    padded = ((counts + MOE_BM - 1) // MOE_BM) * MOE_BM
    p_ends = jnp.cumsum(padded)
    p_starts = p_ends - padded
    pos = jnp.sum(onehot * (csum - 1 + p_starts[None, :]), axis=1).astype(jnp.int32)
    flat = jnp.arange(n_asg, dtype=jnp.int32)
    dest = (flat % TOP_K) * n_tok + flat // TOP_K
    spare = n_asg + jnp.arange((n_blocks + 1) * MOE_BM, dtype=jnp.int32) % MOE_BM
    dst = spare.at[pos].set(dest)
    tok = jnp.where(dst >= n_asg, 0, dst % n_tok)
    block_start = jnp.arange(n_blocks, dtype=jnp.int32) * MOE_BM
    block_e = jnp.minimum(jnp.searchsorted(p_ends, block_start, side="right"), N_EXPERTS - 1).astype(jnp.int32)
    n_real = (p_ends[-1] // MOE_BM).astype(jnp.int32).reshape(1)
    return tok, dst, block_e, n_real


def kernel(x, mix_norm, w_in, conv_w, conv_b, conv_ln_g, conv_ln_b, lru_conv_w, lru_conv_b, lru_wa, lru_ba, lru_wx, lru_bx, lru_lambda, w_out, ffn_norm, dense_wg, dense_wu, dense_wd, w_router, moe_wg, moe_wu, moe_wd, final_norm):
    bsz, seq, d = x.shape
    depth = w_in.shape[0]
    t = bsz * seq
    xt = x.reshape(t, d)
    row = lambda v: v.reshape(1, -1)
    assert depth % 2 == 0, "the final RMSNorm is fused into the last routed layer's combine"
    fold = lambda w: w.reshape(w.shape[0], -1, w.shape[-1])
    w_in_b = w_in[0].astype(BF16)
    for layer in range(0, depth, 2):
        j = layer // 2
        odd = layer + 1
        u, wg_d, wu_d, w_out_b = _in_proj(xt, row(mix_norm[layer]), w_in_b,
                                           [(dense_wg, j), (dense_wu, j), (w_out, layer)])
        y, moe_wg_b, wd_d = _mixer_core(
            u, bsz, seq, conv_w[layer], row(conv_b[layer]), row(conv_ln_g[layer]), row(conv_ln_b[layer]),
            lru_conv_w[layer], row(lru_conv_b[layer]), lru_wa[layer].astype(BF16), row(lru_ba[layer]),
            lru_wx[layer].astype(BF16), row(lru_bx[layer]), row(lru_lambda[layer]),
            [(fold(moe_wg), j), (dense_wd, j)])
        xt, w_in_odd = _out_proj(xt, y, w_out_b, [(w_in, odd)])
        xt, moe_wd_b = _dense_ffn(xt, row(ffn_norm[layer]), wg_d, wu_d, wd_d, [(fold(moe_wd), j)])

        hosted = [(w_out, odd)] + ([(w_in, odd + 1)] if odd + 1 < depth else [])
        u, w_out_b, *nxt = _in_proj(xt, row(mix_norm[odd]), w_in_odd, hosted)
        if nxt:
            w_in_b = nxt[0]
        y, moe_wu_b = _mixer_core(
            u, bsz, seq, conv_w[odd], row(conv_b[odd]), row(conv_ln_g[odd]), row(conv_ln_b[odd]),
            lru_conv_w[odd], row(lru_conv_b[odd]), lru_wa[odd].astype(BF16), row(lru_ba[odd]),
            lru_wx[odd].astype(BF16), row(lru_bx[odd]), row(lru_lambda[odd]), [(fold(moe_wu), j)])
        wr = jnp.zeros((d, LANES), BF16).at[:, 0:N_EXPERTS].set(w_router[j].astype(BF16))
        g = row(ffn_norm[odd])
        xt, info = _out_proj(xt, y, w_out_b, [], router=(g, wr))
        tok, dst, block_e, n_real = _routing_plan(info, t)
        y = _moe_ffn(tok, dst, block_e, n_real, xt, g, moe_wg_b.reshape(moe_wg[j].shape),
                     moe_wu_b.reshape(moe_wu[j].shape), moe_wd_b.reshape(moe_wd[j].shape))
        xt = _combine(xt, info, row(final_norm), y, odd == depth - 1)
    return xt.reshape(bsz, seq, d)
```

```python
import functools

import jax
import jax.numpy as jnp
from jax import lax
from jax.experimental import pallas as pl
from jax.experimental.pallas import tpu as pltpu

F32 = jnp.float32
BF16 = jnp.bfloat16

D_MODEL = 2048
CONV_WIDTH = 1024
LRU_WIDTH = 1024
LRU_HEADS = 8
LRU_HEAD_DIM = LRU_WIDTH // LRU_HEADS
D_IN = 2 * CONV_WIDTH + 2 * LRU_WIDTH
CONV_K = 31
LRU_CONV_K = 4
LRU_C = 8.0
N_EXPERTS = 8
TOP_K = 2
EPS = 1e-6

LANES = 128
SUBLANES = 8
BF16_ROWS = 16
MIB = 1024 * 1024

TM_PROJ = 512
TN_IN = 1024
TS_MIX = 256
CONV_HALO = 32
LRU_HALO = 8
CONV_ROWS = 64
TM_FFN = 512
TF_FFN = 1024
TF_MOE = 768
MOE_BM = 512
TM_COMBINE = 512


def _cparams(semantics, vmem_mib):
    return pltpu.CompilerParams(dimension_semantics=semantics, vmem_limit_bytes=vmem_mib * MIB)


def _sigmoid(x):
    return 0.5 * jnp.tanh(0.5 * x) + 0.5


def _rms_norm(x, g):
    ms = jnp.mean(x * x, axis=-1, keepdims=True)
    return x * lax.rsqrt(ms + EPS) * g


def _cast_specs(casts, n_steps, index_map):
    in_specs, out_specs = [], []
    for w, layer in casts:
        _, rows, cols = w.shape
        assert rows % n_steps == 0 and (rows // n_steps) % BF16_ROWS == 0, (w.shape, n_steps)
        in_specs.append(pl.BlockSpec((None, rows // n_steps, cols),
                                     lambda *idx, layer=layer: (layer, *index_map(*idx))))
        out_specs.append(pl.BlockSpec((rows // n_steps, cols), index_map))
    return in_specs, out_specs


def _cast_shapes(casts):
    return [jax.ShapeDtypeStruct(w.shape[1:], BF16) for w, _ in casts]


def _cast_args(casts):
    return [w for w, _ in casts]


def _run_casts(in_refs, out_refs):
    for src, dst in zip(in_refs, out_refs, strict=True):
        dst[...] = src[...].astype(dst.dtype)


def _in_proj_kernel(x_ref, g_ref, w_ref, *refs, n_cast):
    cast_in, o_ref, cast_out = refs[:n_cast], refs[n_cast], refs[n_cast + 1:]
    _run_casts(cast_in, cast_out)
    h = _rms_norm(x_ref[...], g_ref[...]).astype(BF16)
    for j in range(o_ref.shape[1] // TN_IN):
        cols = slice(j * TN_IN, (j + 1) * TN_IN)
        o_ref[:, cols] = jnp.dot(h, w_ref[:, cols], preferred_element_type=F32)


def _in_proj(x, g, w, casts):
    t, d = x.shape
    n = w.shape[1]
    steps = t // TM_PROJ
    cast_in, cast_out = _cast_specs(casts, steps, lambda i: (i, 0))
    return pl.pallas_call(
        functools.partial(_in_proj_kernel, n_cast=len(casts)),
        grid=(steps,),
        in_specs=[
            pl.BlockSpec((TM_PROJ, d), lambda i: (i, 0)),
            pl.BlockSpec((1, d), lambda i: (0, 0)),
            pl.BlockSpec((d, n), lambda i: (0, 0), pipeline_mode=pl.Buffered(1)),
            *cast_in,
        ],
        out_specs=[pl.BlockSpec((TM_PROJ, n), lambda i: (i, 0)), *cast_out],
        out_shape=[jax.ShapeDtypeStruct((t, n), F32), *_cast_shapes(casts)],
        compiler_params=_cparams(("parallel",), 60),
        name="in_proj",
    )(x, g, w, *_cast_args(casts))


def _causal_depthwise_conv(buf, w_ref, b_ref, out_ref, ts, halo, n_taps):
    base = halo - (n_taps - 1)
    nt = CONV_ROWS // SUBLANES
    nq = (base + n_taps - 1) // SUBLANES + 1
    row8 = lax.broadcasted_iota(jnp.int32, (SUBLANES, LANES), 0)
    for rc in range(ts // CONV_ROWS):
        t0 = rc * CONV_ROWS
        for lc in range(buf.shape[1] // LANES):
            cols = slice(lc * LANES, (lc + 1) * LANES)
            xs = [buf[t0 + SUBLANES * j:t0 + SUBLANES * (j + 1), cols] for j in range(nt + nq - 1)]
            bias = jnp.broadcast_to(b_ref[:, cols], (SUBLANES, LANES))
            out = [bias] * nt
            for p in range(SUBLANES):
                taps = [(q, SUBLANES * q + p - base) for q in range(nq)
                        if 0 <= SUBLANES * q + p - base < n_taps]
                if not taps:
                    continue
                ws = [jnp.broadcast_to(w_ref[k:k + 1, cols], (SUBLANES, LANES)) for _, k in taps]
                zs = []
                for j in range(nt + (1 if p else 0)):
                    z = ws[0] * xs[j + taps[0][0]]
                    for w, (q, _) in zip(ws[1:], taps[1:]):
                        z = z + w * xs[j + q]
                    zs.append(z)
                if p == 0:
                    out = [o + z for o, z in zip(out, zs)]
                else:
                    rs = [pltpu.roll(z, SUBLANES - p, axis=0) for z in zs]
                    keep = row8 < SUBLANES - p
                    out = [o + jnp.where(keep, rs[g], rs[g + 1]) for g, o in enumerate(out)]
            for g in range(nt):
                out_ref[t0 + SUBLANES * g:t0 + SUBLANES * (g + 1), cols] = out[g]


def _mixer_kernel(u_ref, cw_ref, cb_ref, lg_ref, lb_ref, rw_ref, rb_ref, wa_ref, ba_ref,
                  wx_ref, bx_ref, lam_ref, *refs, n_cast):
    cast_in, o_ref, cast_out = refs[:n_cast], refs[n_cast], refs[n_cast + 1:2 * n_cast + 1]
    cbuf, rbuf, abuf, bbuf, hc_ref = refs[2 * n_cast + 1:]
    s = pl.program_id(1)
    ts = TS_MIX
    _run_casts(cast_in, cast_out)

    @pl.when(s == 0)
    def _():
        cbuf[0:CONV_HALO, :] = jnp.zeros((CONV_HALO, CONV_WIDTH), F32)
        rbuf[0:LRU_HALO, :] = jnp.zeros((LRU_HALO, LRU_WIDTH), F32)
        hc_ref[...] = jnp.zeros((SUBLANES, LRU_WIDTH), F32)

    val = u_ref[:, 0:CONV_WIDTH]
    gate = u_ref[:, CONV_WIDTH:2 * CONV_WIDTH]
    cbuf[CONV_HALO:CONV_HALO + ts, :] = val * _sigmoid(gate)
    _causal_depthwise_conv(cbuf, cw_ref, cb_ref, abuf, ts, CONV_HALO, CONV_K)
    cbuf[0:CONV_HALO, :] = cbuf[ts:ts + CONV_HALO, :]
    c = abuf[...]
    mu = jnp.mean(c, axis=-1, keepdims=True)
    cc = c - mu
    var = jnp.mean(cc * cc, axis=-1, keepdims=True)
    cn = cc * lax.rsqrt(var + EPS) * lg_ref[...] + lb_ref[...]
    o_ref[:, 0:CONV_WIDTH] = (cn * _sigmoid(cn)).astype(o_ref.dtype)

    rbuf[LRU_HALO:LRU_HALO + ts, :] = u_ref[:, 2 * CONV_WIDTH:2 * CONV_WIDTH + LRU_WIDTH]
    _causal_depthwise_conv(rbuf, rw_ref, rb_ref, bbuf, ts, LRU_HALO, LRU_CONV_K)
    rbuf[0:LRU_HALO, :] = rbuf[ts:ts + LRU_HALO, :]
    for h in range(LRU_HEADS):
        cols = slice(h * LRU_HEAD_DIM, (h + 1) * LRU_HEAD_DIM)
        xh = bbuf[:, cols]
        xh_b = xh.astype(BF16)
        ga = jnp.dot(xh_b, wa_ref[h], preferred_element_type=F32) + ba_ref[:, cols]
        gx = jnp.dot(xh_b, wx_ref[h], preferred_element_type=F32) + bx_ref[:, cols]
        lam = lam_ref[:, cols]
        e = jnp.exp(-jnp.abs(lam))
        e1 = 1.0 + e
        log1p_e = jnp.where(e1 == 1.0, e, jnp.log(e1) * (e / (e1 - 1.0)))
        sp = jnp.maximum(-lam, 0.0) + log1p_e
        log_a = (-LRU_C) * _sigmoid(ga) * sp
        a = jnp.exp(log_a)
        mult = jnp.sqrt(-jnp.tanh(log_a) * (1.0 + a * a))
        row = lax.broadcasted_iota(jnp.int32, (ts, LRU_HEAD_DIM), 0)
        mult = jnp.where((row == 0) & (s == 0), 1.0, mult)
        abuf[:, cols] = a
        bbuf[:, cols] = mult * (_sigmoid(gx) * xh)

    row8 = lax.broadcasted_iota(jnp.int32, (SUBLANES, LRU_WIDTH), 0)
    h_prev = hc_ref[...]
    for g in range(ts // SUBLANES):
        rows = slice(g * SUBLANES, (g + 1) * SUBLANES)
        a = abuf[rows, :]
        b = bbuf[rows, :]
        for dsh in (1, 2, 4):
            a_s = pltpu.roll(a, dsh, axis=0)
            b_s = pltpu.roll(b, dsh, axis=0)
            m = row8 >= dsh
            b = jnp.where(m, a * b_s + b, b)
            a = jnp.where(m, a * a_s, a)
        hg = a * h_prev + b
        abuf[rows, :] = hg
        h_prev = jnp.broadcast_to(hg[SUBLANES - 1:SUBLANES, :], (SUBLANES, LRU_WIDTH))
    hc_ref[...] = h_prev

    rg = u_ref[:, 2 * CONV_WIDTH + LRU_WIDTH:D_IN]
    gelu = 0.5 * rg * (1.0 + jnp.tanh(0.7978845608028654 * (rg + 0.044715 * (rg * rg * rg))))
    o_ref[:, CONV_WIDTH:CONV_WIDTH + LRU_WIDTH] = (abuf[...] * gelu).astype(o_ref.dtype)


def _mixer_core(u, bsz, seq, cw, cb, lg, lb, rw, rb, wa, ba, wx, bx, lam, casts):
    t = u.shape[0]
    nsb = seq // TS_MIX
    row = lambda n: pl.BlockSpec((1, n), lambda b, s: (0, 0))
    cast_in, cast_out = _cast_specs(casts, bsz * nsb, lambda b, s: (b * nsb + s, 0))
    return pl.pallas_call(
        functools.partial(_mixer_kernel, n_cast=len(casts)),
        grid=(bsz, nsb),
        in_specs=[
            pl.BlockSpec((TS_MIX, D_IN), lambda b, s: (b * nsb + s, 0)),
            pl.BlockSpec((CONV_K, CONV_WIDTH), lambda b, s: (0, 0)),
            row(CONV_WIDTH), row(CONV_WIDTH), row(CONV_WIDTH),
            pl.BlockSpec((LRU_CONV_K, LRU_WIDTH), lambda b, s: (0, 0)),
            row(LRU_WIDTH),
            pl.BlockSpec((LRU_HEADS, LRU_HEAD_DIM, LRU_HEAD_DIM), lambda b, s: (0, 0, 0)),
            row(LRU_WIDTH),
            pl.BlockSpec((LRU_HEADS, LRU_HEAD_DIM, LRU_HEAD_DIM), lambda b, s: (0, 0, 0)),
            row(LRU_WIDTH), row(LRU_WIDTH),
            *cast_in,
        ],
        out_specs=[pl.BlockSpec((TS_MIX, CONV_WIDTH + LRU_WIDTH), lambda b, s: (b * nsb + s, 0)), *cast_out],
        out_shape=[jax.ShapeDtypeStruct((t, CONV_WIDTH + LRU_WIDTH), BF16), *_cast_shapes(casts)],
        scratch_shapes=[
            pltpu.VMEM((CONV_HALO + TS_MIX, CONV_WIDTH), F32),
            pltpu.VMEM((LRU_HALO + TS_MIX, LRU_WIDTH), F32),
            pltpu.VMEM((TS_MIX, LRU_WIDTH), F32),
            pltpu.VMEM((TS_MIX, LRU_WIDTH), F32),
            pltpu.VMEM((SUBLANES, LRU_WIDTH), F32),
        ],
        compiler_params=_cparams(("parallel", "arbitrary"), 56),
        name="mixer_core",
    )(u, cw, cb, lg, lb, rw, rb, wa, ba, wx, bx, lam, *_cast_args(casts))


def _top2_info(logits, width):
    lane = lax.broadcasted_iota(jnp.int32, logits.shape, 1)
    neg = jnp.float32(-jnp.inf)
    l1 = jnp.where(lane < N_EXPERTS, logits, neg)
    m1 = jnp.max(l1, axis=-1, keepdims=True)
    i1 = jnp.min(jnp.where(l1 == m1, lane, LANES), axis=-1, keepdims=True)
    l2 = jnp.where(lane == i1, neg, l1)
    m2 = jnp.max(l2, axis=-1, keepdims=True)
    i2 = jnp.min(jnp.where(l2 == m2, lane, LANES), axis=-1, keepdims=True)
    dlt = jnp.exp(m2 - m1)
    w1 = 1.0 / (1.0 + dlt)
    w2 = dlt / (1.0 + dlt)
    out = jnp.where(lane == 0, i1.astype(F32),
                    jnp.where(lane == 1, i2.astype(F32),
                              jnp.where(lane == 2, w1, jnp.where(lane == 3, w2, 0.0))))
    return out[:, 0:width]


def _out_proj_kernel(x_ref, y_ref, w_ref, *refs, n_cast, route):
    refs = list(refs)
    if route:
        gf_ref, wr_ref = refs[:2]
        refs = refs[2:]
    cast_in, refs = refs[:n_cast], refs[n_cast:]
    o_ref, refs = refs[0], refs[1:]
    if route:
        info_ref, refs = refs[0], refs[1:]
    cast_out = refs
    _run_casts(cast_in, cast_out)
    xn = x_ref[...] + jnp.dot(y_ref[...], w_ref[...], preferred_element_type=F32)
    o_ref[...] = xn
    if route:
        h = _rms_norm(xn, gf_ref[...]).astype(BF16)
        logits = jnp.dot(h, wr_ref[...], preferred_element_type=F32)
        info_ref[...] = _top2_info(logits, info_ref.shape[1])


def _out_proj(x, y, w, casts, router=None):
    t, d = x.shape
    k = y.shape[1]
    steps = t // TM_PROJ
    cast_in, cast_out = _cast_specs(casts, steps, lambda i: (i, 0))
    const = lambda shape: pl.BlockSpec(shape, lambda i: (0, 0))
    route_in = [] if router is None else [const((1, d)), const((d, LANES))]
    route_out = [] if router is None else [pl.BlockSpec((TM_PROJ, SUBLANES), lambda i: (i, 0))]
    route_shape = [] if router is None else [jax.ShapeDtypeStruct((t, SUBLANES), F32)]
    return pl.pallas_call(
        functools.partial(_out_proj_kernel, n_cast=len(casts), route=router is not None),
        grid=(steps,),
        in_specs=[
            pl.BlockSpec((TM_PROJ, d), lambda i: (i, 0)),
            pl.BlockSpec((TM_PROJ, k), lambda i: (i, 0)),
            const((k, d)),
            *route_in,
            *cast_in,
        ],
        out_specs=[pl.BlockSpec((TM_PROJ, d), lambda i: (i, 0)), *route_out, *cast_out],
        out_shape=[jax.ShapeDtypeStruct((t, d), F32), *route_shape, *_cast_shapes(casts)],
        compiler_params=_cparams(("parallel",), 56),
        name="out_proj",
    )(x, y, w, *(router or ()), *_cast_args(casts))


def _swiglu_step(h, wg, wu, wd):
    g = jnp.dot(h, wg, preferred_element_type=F32)
    v = jnp.dot(h, wu, preferred_element_type=F32)
    act = (g * _sigmoid(g) * v).astype(BF16)
    return jnp.dot(act, wd, preferred_element_type=F32)


def _dense_ffn_kernel(x_ref, g_ref, wg_ref, wu_ref, wd_ref, *refs, n_cast):
    cast_in, o_ref, cast_out, h_ref = refs[:n_cast], refs[n_cast], refs[n_cast + 1:2 * n_cast + 1], refs[-1]
    _run_casts(cast_in, cast_out)

    @pl.when(pl.program_id(1) == 0)
    def _():
        h_ref[...] = _rms_norm(x_ref[...], g_ref[...]).astype(BF16)
        o_ref[...] = x_ref[...]

    o_ref[...] += _swiglu_step(h_ref[...], wg_ref[...], wu_ref[...], wd_ref[...])


def _dense_ffn(x, g, wg, wu, wd, casts):
    t, d = x.shape
    ff = wg.shape[1]
    nf = ff // TF_FFN
    cast_in, cast_out = _cast_specs(casts, t // TM_FFN * nf, lambda i, f: (i * nf + f, 0))
    return pl.pallas_call(
        functools.partial(_dense_ffn_kernel, n_cast=len(casts)),
        grid=(t // TM_FFN, nf),
        in_specs=[
            pl.BlockSpec((TM_FFN, d), lambda i, f: (i, 0)),
            pl.BlockSpec((1, d), lambda i, f: (0, 0)),
            pl.BlockSpec((d, TF_FFN), lambda i, f: (0, f)),
            pl.BlockSpec((d, TF_FFN), lambda i, f: (0, f)),
            pl.BlockSpec((TF_FFN, d), lambda i, f: (f, 0)),
            *cast_in,
        ],
        out_specs=[pl.BlockSpec((TM_FFN, d), lambda i, f: (i, 0)), *cast_out],
        out_shape=[jax.ShapeDtypeStruct((t, d), F32), *_cast_shapes(casts)],
        scratch_shapes=[pltpu.VMEM((TM_FFN, d), BF16)],
        compiler_params=_cparams(("parallel", "arbitrary"), 60),
        name="dense_ffn",
    )(x, g, wg, wu, wd, *_cast_args(casts))


def _moe_ffn_kernel(tok_ref, dst_ref, be_ref, nr_ref, x_hbm, g_ref, wg_ref, wu_ref, wd_ref, y_hbm,
                    xg_ref, h_ref, acc_ref, gsem, ssem, *, n_tok):
    b = pl.program_id(0)
    f = pl.program_id(1)
    nb = pl.num_programs(0)
    nf = pl.num_programs(1)
    n_real = nr_ref[0]
    real = b < n_real
    slot = b % 2
    rows_per_step = MOE_BM // nf

    def gather_start(tok, r, slt):
        pltpu.make_async_copy(x_hbm.at[pl.ds(tok, 1), :], xg_ref.at[slt, pl.ds(r, 1), :],
                              gsem.at[slt]).start()

    def scatter_start(a, r, slt):
        pltpu.make_async_copy(acc_ref.at[slt, pl.ds(r, 1), :], y_hbm.at[pl.ds(a, 1), :], ssem).start()

    def gather_wait(slt):
        pltpu.make_async_copy(x_hbm.at[pl.ds(0, MOE_BM), :], xg_ref.at[slt], gsem.at[slt]).wait()

    def scatter_wait():
        pltpu.make_async_copy(acc_ref.at[0], y_hbm.at[pl.ds(0, MOE_BM), :], ssem).wait()

    def drain(blk):
        gather_wait((blk + 1) % 2)

        def body(r, carry):
            scatter_start(dst_ref[blk * MOE_BM + r], r, blk % 2)
            return carry

        lax.fori_loop(0, MOE_BM, body, 0, unroll=8)
        scatter_wait()

    @pl.when(real & (f == 0))
    def _():
        @pl.when(b == 0)
        def _():
            def body(r, carry):
                gather_start(tok_ref[r], r, 0)
                return carry

            lax.fori_loop(0, MOE_BM, body, 0, unroll=8)
            acc_ref[1] = jnp.zeros(acc_ref.shape[1:], acc_ref.dtype)

        gather_wait(slot)
        h_ref[...] = _rms_norm(xg_ref[slot], g_ref[...]).astype(BF16)
        acc_ref[slot] = jnp.zeros(acc_ref.shape[1:], acc_ref.dtype)

    @pl.when(real)
    def _():
        prev_base = jnp.maximum(b - 1, 0) * MOE_BM
        for i in range(rows_per_step):
            r = f * rows_per_step + i
            gather_start(tok_ref[(b + 1) * MOE_BM + r], r, 1 - slot)
            a_prev = jnp.where(b == 0, TOP_K * n_tok + r, dst_ref[prev_base + r])
            scatter_start(a_prev, r, 1 - slot)
        acc_ref[slot] += _swiglu_step(h_ref[...], wg_ref[0], wu_ref[0], wd_ref[0])

        @pl.when(f == nf - 1)
        def _():
            scatter_wait()

    @pl.when((b == n_real) & (f == 0))
    def _():
        drain(b - 1)

    @pl.when(real & (b == nb - 1) & (f == nf - 1))
    def _():
        drain(b)


def _moe_ffn(tok, dst, block_e, n_real, x, g, wg, wu, wd):
    t, d = x.shape
    ff = wg.shape[2]
    nf = ff // TF_MOE
    n_blocks = block_e.shape[0]
    assert MOE_BM % nf == 0
    assert tok.shape[0] == (n_blocks + 1) * MOE_BM == dst.shape[0]

    def f_eff(b, f, nr):
        return jnp.where(b < nr[0], f, nf - 1)

    grid_spec = pltpu.PrefetchScalarGridSpec(
        num_scalar_prefetch=4,
        grid=(n_blocks, nf),
        in_specs=[
            pl.BlockSpec(memory_space=pl.ANY),
            pl.BlockSpec((1, d), lambda b, f, tok, dst, be, nr: (0, 0)),
            pl.BlockSpec((1, d, TF_MOE), lambda b, f, tok, dst, be, nr: (be[b], 0, f_eff(b, f, nr))),
            pl.BlockSpec((1, d, TF_MOE), lambda b, f, tok, dst, be, nr: (be[b], 0, f_eff(b, f, nr))),
            pl.BlockSpec((1, TF_MOE, d), lambda b, f, tok, dst, be, nr: (be[b], f_eff(b, f, nr), 0)),
        ],
        out_specs=pl.BlockSpec(memory_space=pl.ANY),
        scratch_shapes=[
            pltpu.VMEM((2, MOE_BM, d), F32),
            pltpu.VMEM((MOE_BM, d), BF16),
            pltpu.VMEM((2, MOE_BM, d), F32),
            pltpu.SemaphoreType.DMA((2,)),
            pltpu.SemaphoreType.DMA(()),
        ],
    )
    return pl.pallas_call(
        functools.partial(_moe_ffn_kernel, n_tok=t),
        grid_spec=grid_spec,
        out_shape=jax.ShapeDtypeStruct((TOP_K * t + MOE_BM, d), F32),
        compiler_params=_cparams(("arbitrary", "arbitrary"), 56),
        name="moe_ffn",
    )(tok, dst, block_e, n_real, x, g, wg, wu, wd)


def _combine_kernel(x_ref, info_ref, g_ref, ya_ref, yb_ref, o_ref, *, final_norm):
    w1 = info_ref[:, 2:3]
    w2 = info_ref[:, 3:4]
    z = x_ref[...] + (ya_ref[...] * w1 + yb_ref[...] * w2)
    o_ref[...] = _rms_norm(z, g_ref[...]) if final_norm else z


def _combine(x, info, g, y, final_norm):
    t, d = x.shape
    nt = t // TM_COMBINE
    return pl.pallas_call(
        functools.partial(_combine_kernel, final_norm=final_norm),
        grid=(nt,),
        in_specs=[
            pl.BlockSpec((TM_COMBINE, d), lambda i: (i, 0)),
            pl.BlockSpec((TM_COMBINE, SUBLANES), lambda i: (i, 0)),
            pl.BlockSpec((1, d), lambda i: (0, 0)),
            pl.BlockSpec((TM_COMBINE, d), lambda i: (i, 0)),
            pl.BlockSpec((TM_COMBINE, d), lambda i: (nt + i, 0)),
        ],
        out_specs=pl.BlockSpec((TM_COMBINE, d), lambda i: (i, 0)),
        out_shape=jax.ShapeDtypeStruct((t, d), F32),
        compiler_params=_cparams(("parallel",), 48),
        name="moe_combine",
    )(x, info, g, y, y)


def _routing_plan(info, n_tok):
    n_asg = n_tok * TOP_K
    n_blocks = n_asg // MOE_BM + N_EXPERTS
    flat_e = info[:, 0:TOP_K].astype(jnp.int32).reshape(n_asg)
    onehot = (flat_e[:, None] == jnp.arange(N_EXPERTS, dtype=jnp.int32)[None, :]).astype(jnp.int32)
    csum = jnp.cumsum(onehot, axis=0)
    counts = csum[-1]
    padded = ((counts + MOE_BM - 1) // MOE_BM) * MOE_BM
    p_ends = jnp.cumsum(padded)
    p_starts = p_ends - padded
    pos = jnp.sum(onehot * (csum - 1 + p_starts[None, :]), axis=1).astype(jnp.int32)
    flat = jnp.arange(n_asg, dtype=jnp.int32)
    dest = (flat % TOP_K) * n_tok + flat // TOP_K
    spare = n_asg + jnp.arange((n_blocks + 1) * MOE_BM, dtype=jnp.int32) % MOE_BM
    dst = spare.at[pos].set(dest)
    tok = jnp.where(dst >= n_asg, 0, dst % n_tok)
    block_start = jnp.arange(n_blocks, dtype=jnp.int32) * MOE_BM
    block_e = jnp.minimum(jnp.searchsorted(p_ends, block_start, side="right"), N_EXPERTS - 1).astype(jnp.int32)
    n_real = (p_ends[-1] // MOE_BM).astype(jnp.int32).reshape(1)
    return tok, dst, block_e, n_real


def kernel(x, mix_norm, w_in, conv_w, conv_b, conv_ln_g, conv_ln_b, lru_conv_w, lru_conv_b, lru_wa, lru_ba, lru_wx, lru_bx, lru_lambda, w_out, ffn_norm, dense_wg, dense_wu, dense_wd, w_router, moe_wg, moe_wu, moe_wd, final_norm):
    bsz, seq, d = x.shape
    depth = w_in.shape[0]
    t = bsz * seq
    xt = x.reshape(t, d)
    row = lambda v: v.reshape(1, -1)
    assert depth % 2 == 0, "the final RMSNorm is fused into the last routed layer's combine"
    fold = lambda w: w.reshape(w.shape[0], -1, w.shape[-1])
    w_in_b = w_in[0].astype(BF16)
    for layer in range(0, depth, 2):
        j = layer // 2
        odd = layer + 1
        u, wg_d, wu_d, w_out_b = _in_proj(xt, row(mix_norm[layer]), w_in_b,
                                           [(dense_wg, j), (dense_wu, j), (w_out, layer)])
        y, moe_wg_b, wd_d = _mixer_core(
            u, bsz, seq, conv_w[layer], row(conv_b[layer]), row(conv_ln_g[layer]), row(conv_ln_b[layer]),
            lru_conv_w[layer], row(lru_conv_b[layer]), lru_wa[layer].astype(BF16), row(lru_ba[layer]),
            lru_wx[layer].astype(BF16), row(lru_bx[layer]), row(lru_lambda[layer]),
            [(fold(moe_wg), j), (dense_wd, j)])
        xt, w_in_odd = _out_proj(xt, y, w_out_b, [(w_in, odd)])
        xt, moe_wd_b = _dense_ffn(xt, row(ffn_norm[layer]), wg_d, wu_d, wd_d, [(fold(moe_wd), j)])

        hosted = [(w_out, odd)] + ([(w_in, odd + 1)] if odd + 1 < depth else [])
        u, w_out_b, *nxt = _in_proj(xt, row(mix_norm[odd]), w_in_odd, hosted)
        if nxt:
            w_in_b = nxt[0]
        y, moe_wu_b = _mixer_core(
            u, bsz, seq, conv_w[odd], row(conv_b[odd]), row(conv_ln_g[odd]), row(conv_ln_b[odd]),
            lru_conv_w[odd], row(lru_conv_b[odd]), lru_wa[odd].astype(BF16), row(lru_ba[odd]),
            lru_wx[odd].astype(BF16), row(lru_bx[odd]), row(lru_lambda[odd]), [(fold(moe_wu), j)])
        wr = jnp.zeros((d, LANES), BF16).at[:, 0:N_EXPERTS].set(w_router[j].astype(BF16))
        g = row(ffn_norm[odd])
        xt, info = _out_proj(xt, y, w_out_b, [], router=(g, wr))
        tok, dst, block_e, n_real = _routing_plan(info, t)
        y = _moe_ffn(tok, dst, block_e, n_real, xt, g, moe_wg_b.reshape(moe_wg[j].shape),
                     moe_wu_b.reshape(moe_wu[j].shape), moe_wd_b.reshape(moe_wd[j].shape))
        xt = _combine(xt, info, row(final_norm), y, odd == depth - 1)
    return xt.reshape(bsz, seq, d)
```

```python
import functools

import jax
import jax.numpy as jnp
from jax import lax
from jax.experimental import pallas as pl
from jax.experimental.pallas import tpu as pltpu

F32 = jnp.float32
BF16 = jnp.bfloat16

D_MODEL = 2048
CONV_WIDTH = 1024
LRU_WIDTH = 1024
LRU_HEADS = 8
LRU_HEAD_DIM = LRU_WIDTH // LRU_HEADS
D_IN = 2 * CONV_WIDTH + 2 * LRU_WIDTH
CONV_K = 31
LRU_CONV_K = 4
LRU_C = 8.0
N_EXPERTS = 8
TOP_K = 2
EPS = 1e-6

LANES = 128
SUBLANES = 8
BF16_ROWS = 16
MIB = 1024 * 1024

TM_PROJ = 512
TN_IN = 1024
TS_MIX = 256
CONV_HALO = 32
LRU_HALO = 8
CONV_ROWS = 64
TM_FFN = 512
TF_FFN = 1024
TF_MOE = 768
MOE_BM = 512
MOE_ROW_QUANTUM = 128
TM_COMBINE = 512


def _cparams(semantics, vmem_mib):
    return pltpu.CompilerParams(dimension_semantics=semantics, vmem_limit_bytes=vmem_mib * MIB)


def _sigmoid(x):
    return 0.5 * jnp.tanh(0.5 * x) + 0.5


def _rms_norm(x, g):
    ms = jnp.mean(x * x, axis=-1, keepdims=True)
    return x * lax.rsqrt(ms + EPS) * g


def _cast_specs(casts, n_steps, index_map):
    in_specs, out_specs = [], []
    for w, layer in casts:
        _, rows, cols = w.shape
        assert rows % n_steps == 0 and (rows // n_steps) % BF16_ROWS == 0, (w.shape, n_steps)
        in_specs.append(pl.BlockSpec((None, rows // n_steps, cols),
                                     lambda *idx, layer=layer: (layer, *index_map(*idx))))
        out_specs.append(pl.BlockSpec((rows // n_steps, cols), index_map))
    return in_specs, out_specs


def _cast_shapes(casts):
    return [jax.ShapeDtypeStruct(w.shape[1:], BF16) for w, _ in casts]


def _cast_args(casts):
    return [w for w, _ in casts]


def _run_casts(in_refs, out_refs):
    for src, dst in zip(in_refs, out_refs, strict=True):
        dst[...] = src[...].astype(dst.dtype)


def _in_proj_kernel(x_ref, g_ref, w_ref, *refs, n_cast):
    cast_in, o_ref, cast_out = refs[:n_cast], refs[n_cast], refs[n_cast + 1:]
    _run_casts(cast_in, cast_out)
    h = _rms_norm(x_ref[...], g_ref[...]).astype(BF16)
    for j in range(o_ref.shape[1] // TN_IN):
        cols = slice(j * TN_IN, (j + 1) * TN_IN)
        o_ref[:, cols] = jnp.dot(h, w_ref[:, cols], preferred_element_type=F32)


def _in_proj(x, g, w, casts):
    t, d = x.shape
    n = w.shape[1]
    steps = t // TM_PROJ
    cast_in, cast_out = _cast_specs(casts, steps, lambda i: (i, 0))
    return pl.pallas_call(
        functools.partial(_in_proj_kernel, n_cast=len(casts)),
        grid=(steps,),
        in_specs=[
            pl.BlockSpec((TM_PROJ, d), lambda i: (i, 0)),
            pl.BlockSpec((1, d), lambda i: (0, 0)),
            pl.BlockSpec((d, n), lambda i: (0, 0), pipeline_mode=pl.Buffered(1)),
            *cast_in,
        ],
        out_specs=[pl.BlockSpec((TM_PROJ, n), lambda i: (i, 0)), *cast_out],
        out_shape=[jax.ShapeDtypeStruct((t, n), F32), *_cast_shapes(casts)],
        compiler_params=_cparams(("parallel",), 60),
        name="in_proj",
    )(x, g, w, *_cast_args(casts))


def _causal_depthwise_conv(buf, w_ref, b_ref, out_ref, ts, halo, n_taps):
    base = halo - (n_taps - 1)
    nt = CONV_ROWS // SUBLANES
    nq = (base + n_taps - 1) // SUBLANES + 1
    row8 = lax.broadcasted_iota(jnp.int32, (SUBLANES, LANES), 0)
    for rc in range(ts // CONV_ROWS):
        t0 = rc * CONV_ROWS
        for lc in range(buf.shape[1] // LANES):
            cols = slice(lc * LANES, (lc + 1) * LANES)
            xs = [buf[t0 + SUBLANES * j:t0 + SUBLANES * (j + 1), cols] for j in range(nt + nq - 1)]
            bias = jnp.broadcast_to(b_ref[:, cols], (SUBLANES, LANES))
            out = [bias] * nt
            for p in range(SUBLANES):
                taps = [(q, SUBLANES * q + p - base) for q in range(nq)
                        if 0 <= SUBLANES * q + p - base < n_taps]
                if not taps:
                    continue
                ws = [jnp.broadcast_to(w_ref[k:k + 1, cols], (SUBLANES, LANES)) for _, k in taps]
                zs = []
                for j in range(nt + (1 if p else 0)):
                    z = ws[0] * xs[j + taps[0][0]]
                    for w, (q, _) in zip(ws[1:], taps[1:]):
                        z = z + w * xs[j + q]
                    zs.append(z)
                if p == 0:
                    out = [o + z for o, z in zip(out, zs)]
                else:
                    rs = [pltpu.roll(z, SUBLANES - p, axis=0) for z in zs]
                    keep = row8 < SUBLANES - p
                    out = [o + jnp.where(keep, rs[g], rs[g + 1]) for g, o in enumerate(out)]
            for g in range(nt):
                out_ref[t0 + SUBLANES * g:t0 + SUBLANES * (g + 1), cols] = out[g]


def _mixer_kernel(u_ref, cw_ref, cb_ref, lg_ref, lb_ref, rw_ref, rb_ref, wa_ref, ba_ref,
                  wx_ref, bx_ref, lam_ref, *refs, n_cast):
    cast_in, o_ref, cast_out = refs[:n_cast], refs[n_cast], refs[n_cast + 1:2 * n_cast + 1]
    cbuf, rbuf, abuf, bbuf, hc_ref = refs[2 * n_cast + 1:]
    s = pl.program_id(1)
    ts = TS_MIX
    _run_casts(cast_in, cast_out)

    @pl.when(s == 0)
    def _():
        cbuf[0:CONV_HALO, :] = jnp.zeros((CONV_HALO, CONV_WIDTH), F32)
        rbuf[0:LRU_HALO, :] = jnp.zeros((LRU_HALO, LRU_WIDTH), F32)
        hc_ref[...] = jnp.zeros((SUBLANES, LRU_WIDTH), F32)

    val = u_ref[:, 0:CONV_WIDTH]
    gate = u_ref[:, CONV_WIDTH:2 * CONV_WIDTH]
    cbuf[CONV_HALO:CONV_HALO + ts, :] = val * _sigmoid(gate)
    _causal_depthwise_conv(cbuf, cw_ref, cb_ref, abuf, ts, CONV_HALO, CONV_K)
    cbuf[0:CONV_HALO, :] = cbuf[ts:ts + CONV_HALO, :]
    c = abuf[...]
    mu = jnp.mean(c, axis=-1, keepdims=True)
    cc = c - mu
    var = jnp.mean(cc * cc, axis=-1, keepdims=True)
    cn = cc * lax.rsqrt(var + EPS) * lg_ref[...] + lb_ref[...]
    o_ref[:, 0:CONV_WIDTH] = (cn * _sigmoid(cn)).astype(o_ref.dtype)

    rbuf[LRU_HALO:LRU_HALO + ts, :] = u_ref[:, 2 * CONV_WIDTH:2 * CONV_WIDTH + LRU_WIDTH]
    _causal_depthwise_conv(rbuf, rw_ref, rb_ref, bbuf, ts, LRU_HALO, LRU_CONV_K)
    rbuf[0:LRU_HALO, :] = rbuf[ts:ts + LRU_HALO, :]
    for h in range(LRU_HEADS):
        cols = slice(h * LRU_HEAD_DIM, (h + 1) * LRU_HEAD_DIM)
        xh = bbuf[:, cols]
        xh_b = xh.astype(BF16)
        ga = jnp.dot(xh_b, wa_ref[h], preferred_element_type=F32) + ba_ref[:, cols]
        gx = jnp.dot(xh_b, wx_ref[h], preferred_element_type=F32) + bx_ref[:, cols]
        lam = lam_ref[:, cols]
        e = jnp.exp(-jnp.abs(lam))
        e1 = 1.0 + e
        log1p_e = jnp.where(e1 == 1.0, e, jnp.log(e1) * (e / (e1 - 1.0)))
        sp = jnp.maximum(-lam, 0.0) + log1p_e
        log_a = (-LRU_C) * _sigmoid(ga) * sp
        a = jnp.exp(log_a)
        mult = jnp.sqrt(-jnp.tanh(log_a) * (1.0 + a * a))
        row = lax.broadcasted_iota(jnp.int32, (ts, LRU_HEAD_DIM), 0)
        mult = jnp.where((row == 0) & (s == 0), 1.0, mult)
        abuf[:, cols] = a
        bbuf[:, cols] = mult * (_sigmoid(gx) * xh)

    row8 = lax.broadcasted_iota(jnp.int32, (SUBLANES, LRU_WIDTH), 0)
    h_prev = hc_ref[...]
    for g in range(ts // SUBLANES):
        rows = slice(g * SUBLANES, (g + 1) * SUBLANES)
        a = abuf[rows, :]
        b = bbuf[rows, :]
        for dsh in (1, 2, 4):
            a_s = pltpu.roll(a, dsh, axis=0)
            b_s = pltpu.roll(b, dsh, axis=0)
            m = row8 >= dsh
            b = jnp.where(m, a * b_s + b, b)
            a = jnp.where(m, a * a_s, a)
        hg = a * h_prev + b
        abuf[rows, :] = hg
        h_prev = jnp.broadcast_to(hg[SUBLANES - 1:SUBLANES, :], (SUBLANES, LRU_WIDTH))
    hc_ref[...] = h_prev

    rg = u_ref[:, 2 * CONV_WIDTH + LRU_WIDTH:D_IN]
    gelu = 0.5 * rg * (1.0 + jnp.tanh(0.7978845608028654 * (rg + 0.044715 * (rg * rg * rg))))
    o_ref[:, CONV_WIDTH:CONV_WIDTH + LRU_WIDTH] = (abuf[...] * gelu).astype(o_ref.dtype)


def _mixer_core(u, bsz, seq, cw, cb, lg, lb, rw, rb, wa, ba, wx, bx, lam, casts):
    t = u.shape[0]
    nsb = seq // TS_MIX
    row = lambda n: pl.BlockSpec((1, n), lambda b, s: (0, 0))
    cast_in, cast_out = _cast_specs(casts, bsz * nsb, lambda b, s: (b * nsb + s, 0))
    return pl.pallas_call(
        functools.partial(_mixer_kernel, n_cast=len(casts)),
        grid=(bsz, nsb),
        in_specs=[
            pl.BlockSpec((TS_MIX, D_IN), lambda b, s: (b * nsb + s, 0)),
            pl.BlockSpec((CONV_K, CONV_WIDTH), lambda b, s: (0, 0)),
            row(CONV_WIDTH), row(CONV_WIDTH), row(CONV_WIDTH),
            pl.BlockSpec((LRU_CONV_K, LRU_WIDTH), lambda b, s: (0, 0)),
            row(LRU_WIDTH),
            pl.BlockSpec((LRU_HEADS, LRU_HEAD_DIM, LRU_HEAD_DIM), lambda b, s: (0, 0, 0)),
            row(LRU_WIDTH),
            pl.BlockSpec((LRU_HEADS, LRU_HEAD_DIM, LRU_HEAD_DIM), lambda b, s: (0, 0, 0)),
            row(LRU_WIDTH), row(LRU_WIDTH),
            *cast_in,
        ],
        out_specs=[pl.BlockSpec((TS_MIX, CONV_WIDTH + LRU_WIDTH), lambda b, s: (b * nsb + s, 0)), *cast_out],
        out_shape=[jax.ShapeDtypeStruct((t, CONV_WIDTH + LRU_WIDTH), BF16), *_cast_shapes(casts)],
        scratch_shapes=[
            pltpu.VMEM((CONV_HALO + TS_MIX, CONV_WIDTH), F32),
            pltpu.VMEM((LRU_HALO + TS_MIX, LRU_WIDTH), F32),
            pltpu.VMEM((TS_MIX, LRU_WIDTH), F32),
            pltpu.VMEM((TS_MIX, LRU_WIDTH), F32),
            pltpu.VMEM((SUBLANES, LRU_WIDTH), F32),
        ],
        compiler_params=_cparams(("parallel", "arbitrary"), 56),
        name="mixer_core",
    )(u, cw, cb, lg, lb, rw, rb, wa, ba, wx, bx, lam, *_cast_args(casts))


def _top2_info(logits, width):
    lane = lax.broadcasted_iota(jnp.int32, logits.shape, 1)
    neg = jnp.float32(-jnp.inf)
    l1 = jnp.where(lane < N_EXPERTS, logits, neg)
    m1 = jnp.max(l1, axis=-1, keepdims=True)
    i1 = jnp.min(jnp.where(l1 == m1, lane, LANES), axis=-1, keepdims=True)
    l2 = jnp.where(lane == i1, neg, l1)
    m2 = jnp.max(l2, axis=-1, keepdims=True)
    i2 = jnp.min(jnp.where(l2 == m2, lane, LANES), axis=-1, keepdims=True)
    dlt = jnp.exp(m2 - m1)
    w1 = 1.0 / (1.0 + dlt)
    w2 = dlt / (1.0 + dlt)
    out = jnp.where(lane == 0, i1.astype(F32),
                    jnp.where(lane == 1, i2.astype(F32),
                              jnp.where(lane == 2, w1, jnp.where(lane == 3, w2, 0.0))))
    return out[:, 0:width]


def _out_proj_kernel(x_ref, y_ref, w_ref, *refs, n_cast, route):
    refs = list(refs)
    if route:
        gf_ref, wr_ref = refs[:2]
        refs = refs[2:]
    cast_in, refs = refs[:n_cast], refs[n_cast:]
    o_ref, refs = refs[0], refs[1:]
    if route:
        info_ref, refs = refs[0], refs[1:]
    cast_out = refs
    _run_casts(cast_in, cast_out)
    xn = x_ref[...] + jnp.dot(y_ref[...], w_ref[...], preferred_element_type=F32)
    o_ref[...] = xn
    if route:
        h = _rms_norm(xn, gf_ref[...]).astype(BF16)
        logits = jnp.dot(h, wr_ref[...], preferred_element_type=F32)
        info_ref[...] = _top2_info(logits, info_ref.shape[1])


def _out_proj(x, y, w, casts, router=None):
    t, d = x.shape
    k = y.shape[1]
    steps = t // TM_PROJ
    cast_in, cast_out = _cast_specs(casts, steps, lambda i: (i, 0))
    const = lambda shape: pl.BlockSpec(shape, lambda i: (0, 0))
    route_in = [] if router is None else [const((1, d)), const((d, LANES))]
    route_out = [] if router is None else [pl.BlockSpec((TM_PROJ, SUBLANES), lambda i: (i, 0))]
    route_shape = [] if router is None else [jax.ShapeDtypeStruct((t, SUBLANES), F32)]
    return pl.pallas_call(
        functools.partial(_out_proj_kernel, n_cast=len(casts), route=router is not None),
        grid=(steps,),
        in_specs=[
            pl.BlockSpec((TM_PROJ, d), lambda i: (i, 0)),
            pl.BlockSpec((TM_PROJ, k), lambda i: (i, 0)),
            const((k, d)),
            *route_in,
            *cast_in,
        ],
        out_specs=[pl.BlockSpec((TM_PROJ, d), lambda i: (i, 0)), *route_out, *cast_out],
        out_shape=[jax.ShapeDtypeStruct((t, d), F32), *route_shape, *_cast_shapes(casts)],
        compiler_params=_cparams(("parallel",), 56),
        name="out_proj",
    )(x, y, w, *(router or ()), *_cast_args(casts))


def _swiglu_step(h, wg, wu, wd):
    g = jnp.dot(h, wg, preferred_element_type=F32)
    v = jnp.dot(h, wu, preferred_element_type=F32)
    act = (g * _sigmoid(g) * v).astype(BF16)
    return jnp.dot(act, wd, preferred_element_type=F32)


def _dense_ffn_kernel(x_ref, g_ref, wg_ref, wu_ref, wd_ref, *refs, n_cast):
    cast_in, o_ref, cast_out, h_ref = refs[:n_cast], refs[n_cast], refs[n_cast + 1:2 * n_cast + 1], refs[-1]
    _run_casts(cast_in, cast_out)

    @pl.when(pl.program_id(1) == 0)
    def _():
        h_ref[...] = _rms_norm(x_ref[...], g_ref[...]).astype(BF16)
        o_ref[...] = x_ref[...]

    o_ref[...] += _swiglu_step(h_ref[...], wg_ref[...], wu_ref[...], wd_ref[...])


def _dense_ffn(x, g, wg, wu, wd, casts):
    t, d = x.shape
    ff = wg.shape[1]
    nf = ff // TF_FFN
    cast_in, cast_out = _cast_specs(casts, t // TM_FFN * nf, lambda i, f: (i * nf + f, 0))
    return pl.pallas_call(
        functools.partial(_dense_ffn_kernel, n_cast=len(casts)),
        grid=(t // TM_FFN, nf),
        in_specs=[
            pl.BlockSpec((TM_FFN, d), lambda i, f: (i, 0)),
            pl.BlockSpec((1, d), lambda i, f: (0, 0)),
            pl.BlockSpec((d, TF_FFN), lambda i, f: (0, f)),
            pl.BlockSpec((d, TF_FFN), lambda i, f: (0, f)),
            pl.BlockSpec((TF_FFN, d), lambda i, f: (f, 0)),
            *cast_in,
        ],
        out_specs=[pl.BlockSpec((TM_FFN, d), lambda i, f: (i, 0)), *cast_out],
        out_shape=[jax.ShapeDtypeStruct((t, d), F32), *_cast_shapes(casts)],
        scratch_shapes=[pltpu.VMEM((TM_FFN, d), BF16)],
        compiler_params=_cparams(("parallel", "arbitrary"), 60),
        name="dense_ffn",
    )(x, g, wg, wu, wd, *_cast_args(casts))


def _moe_ffn_kernel(tok_ref, dst_ref, be_ref, nr_ref, x_hbm, g_ref, wg_ref, wu_ref, wd_ref, y_hbm,
                    xg_ref, h_ref, acc_ref, gsem, ssem, *, n_tok):
    b = pl.program_id(0)
    f = pl.program_id(1)
    nb = pl.num_programs(0)
    nf = pl.num_programs(1)
    n_real = nr_ref[0]
    real = b < n_real
    slot = b % 2
    rows_per_step = MOE_BM // nf

    def gather_start(tok, r, slt):
        pltpu.make_async_copy(x_hbm.at[pl.ds(tok, 1), :], xg_ref.at[slt, pl.ds(r, 1), :],
                              gsem.at[slt]).start()

    def scatter_start(a, r, slt):
        pltpu.make_async_copy(acc_ref.at[slt, pl.ds(r, 1), :], y_hbm.at[pl.ds(a, 1), :], ssem).start()

    def gather_wait(slt):
        pltpu.make_async_copy(x_hbm.at[pl.ds(0, MOE_BM), :], xg_ref.at[slt], gsem.at[slt]).wait()

    def scatter_wait():
        pltpu.make_async_copy(acc_ref.at[0], y_hbm.at[pl.ds(0, MOE_BM), :], ssem).wait()

    def drain(blk):
        gather_wait((blk + 1) % 2)

        def body(r, carry):
            scatter_start(dst_ref[blk * MOE_BM + r], r, blk % 2)
            return carry

        lax.fori_loop(0, MOE_BM, body, 0, unroll=8)
        scatter_wait()

    @pl.when(real & (f == 0))
    def _():
        @pl.when(b == 0)
        def _():
            def body(r, carry):
                gather_start(tok_ref[r], r, 0)
                return carry

            lax.fori_loop(0, MOE_BM, body, 0, unroll=8)
            acc_ref[1] = jnp.zeros(acc_ref.shape[1:], acc_ref.dtype)

        gather_wait(slot)
        h_ref[...] = _rms_norm(xg_ref[slot], g_ref[...]).astype(BF16)
        acc_ref[slot] = jnp.zeros(acc_ref.shape[1:], acc_ref.dtype)

    def step(m_rows):
        prev_base = jnp.maximum(b - 1, 0) * MOE_BM
        for i in range(rows_per_step):
            r = f * rows_per_step + i
            gather_start(tok_ref[(b + 1) * MOE_BM + r], r, 1 - slot)
            a_prev = jnp.where(b == 0, TOP_K * n_tok + r, dst_ref[prev_base + r])
            scatter_start(a_prev, r, 1 - slot)
        acc_ref[slot, 0:m_rows] += _swiglu_step(h_ref[0:m_rows], wg_ref[0], wu_ref[0], wd_ref[0])

    rows_class = be_ref[b] // N_EXPERTS
    for c in range(MOE_BM // MOE_ROW_QUANTUM):
        @pl.when(real & (rows_class == c))
        def _():
            step((c + 1) * MOE_ROW_QUANTUM)

    @pl.when(real & (f == nf - 1))
    def _():
        scatter_wait()

    @pl.when((b == n_real) & (f == 0))
    def _():
        drain(b - 1)

    @pl.when(real & (b == nb - 1) & (f == nf - 1))
    def _():
        drain(b)


def _moe_ffn(tok, dst, block_e, n_real, x, g, wg, wu, wd):
    t, d = x.shape
    ff = wg.shape[2]
    nf = ff // TF_MOE
    n_blocks = block_e.shape[0]
    assert MOE_BM % nf == 0
    assert tok.shape[0] == (n_blocks + 1) * MOE_BM == dst.shape[0]

    def f_eff(b, f, nr):
        return jnp.where(b < nr[0], f, nf - 1)

    grid_spec = pltpu.PrefetchScalarGridSpec(
        num_scalar_prefetch=4,
        grid=(n_blocks, nf),
        in_specs=[
            pl.BlockSpec(memory_space=pl.ANY),
            pl.BlockSpec((1, d), lambda b, f, tok, dst, be, nr: (0, 0)),
            pl.BlockSpec((1, d, TF_MOE), lambda b, f, tok, dst, be, nr: (be[b] % N_EXPERTS, 0, f_eff(b, f, nr))),
            pl.BlockSpec((1, d, TF_MOE), lambda b, f, tok, dst, be, nr: (be[b] % N_EXPERTS, 0, f_eff(b, f, nr))),
            pl.BlockSpec((1, TF_MOE, d), lambda b, f, tok, dst, be, nr: (be[b] % N_EXPERTS, f_eff(b, f, nr), 0)),
        ],
        out_specs=pl.BlockSpec(memory_space=pl.ANY),
        scratch_shapes=[
            pltpu.VMEM((2, MOE_BM, d), F32),
            pltpu.VMEM((MOE_BM, d), BF16),
            pltpu.VMEM((2, MOE_BM, d), F32),
            pltpu.SemaphoreType.DMA((2,)),
            pltpu.SemaphoreType.DMA(()),
        ],
    )
    return pl.pallas_call(
        functools.partial(_moe_ffn_kernel, n_tok=t),
        grid_spec=grid_spec,
        out_shape=jax.ShapeDtypeStruct((TOP_K * t + MOE_BM, d), F32),
        compiler_params=_cparams(("arbitrary", "arbitrary"), 56),
        name="moe_ffn",
    )(tok, dst, block_e, n_real, x, g, wg, wu, wd)


def _combine_kernel(x_ref, info_ref, g_ref, ya_ref, yb_ref, o_ref, *, final_norm):
    w1 = info_ref[:, 2:3]
    w2 = info_ref[:, 3:4]
    z = x_ref[...] + (ya_ref[...] * w1 + yb_ref[...] * w2)
    o_ref[...] = _rms_norm(z, g_ref[...]) if final_norm else z


def _combine(x, info, g, y, final_norm):
    t, d = x.shape
    nt = t // TM_COMBINE
    return pl.pallas_call(
        functools.partial(_combine_kernel, final_norm=final_norm),
        grid=(nt,),
        in_specs=[
            pl.BlockSpec((TM_COMBINE, d), lambda i: (i, 0)),
            pl.BlockSpec((TM_COMBINE, SUBLANES), lambda i: (i, 0)),
            pl.BlockSpec((1, d), lambda i: (0, 0)),
            pl.BlockSpec((TM_COMBINE, d), lambda i: (i, 0)),
            pl.BlockSpec((TM_COMBINE, d), lambda i: (nt + i, 0)),
        ],
        out_specs=pl.BlockSpec((TM_COMBINE, d), lambda i: (i, 0)),
        out_shape=jax.ShapeDtypeStruct((t, d), F32),
        compiler_params=_cparams(("parallel",), 48),
        name="moe_combine",
    )(x, info, g, y, y)


def _routing_plan(info, n_tok):
    n_asg = n_tok * TOP_K
    n_blocks = n_asg // MOE_BM + N_EXPERTS
    flat_e = info[:, 0:TOP_K].astype(jnp.int32).reshape(n_asg)
    onehot = (flat_e[:, None] == jnp.arange(N_EXPERTS, dtype=jnp.int32)[None, :]).astype(jnp.int32)
    csum = jnp.cumsum(onehot, axis=0)
    counts = csum[-1]
    padded = ((counts + MOE_BM - 1) // MOE_BM) * MOE_BM
    p_ends = jnp.cumsum(padded)
    p_starts = p_ends - padded
    pos = jnp.sum(onehot * (csum - 1 + p_starts[None, :]), axis=1).astype(jnp.int32)
    flat = jnp.arange(n_asg, dtype=jnp.int32)
    dest = (flat % TOP_K) * n_tok + flat // TOP_K
    spare = n_asg + jnp.arange((n_blocks + 1) * MOE_BM, dtype=jnp.int32) % MOE_BM
    dst = spare.at[pos].set(dest)
    tok = jnp.where(dst >= n_asg, 0, dst % n_tok)
    block_start = jnp.arange(n_blocks, dtype=jnp.int32) * MOE_BM
    block_e = jnp.minimum(jnp.searchsorted(p_ends, block_start, side="right"), N_EXPERTS - 1).astype(jnp.int32)
    n_real = (p_ends[-1] // MOE_BM).astype(jnp.int32).reshape(1)
    n_valid = jnp.clip((p_starts + counts)[block_e] - block_start, 1, MOE_BM)
    block_meta = block_e + N_EXPERTS * ((n_valid - 1) // MOE_ROW_QUANTUM).astype(jnp.int32)
    return tok, dst, block_meta, n_real


def kernel(x, mix_norm, w_in, conv_w, conv_b, conv_ln_g, conv_ln_b, lru_conv_w, lru_conv_b, lru_wa, lru_ba, lru_wx, lru_bx, lru_lambda, w_out, ffn_norm, dense_wg, dense_wu, dense_wd, w_router, moe_wg, moe_wu, moe_wd, final_norm):
    bsz, seq, d = x.shape
    depth = w_in.shape[0]
    t = bsz * seq
    xt = x.reshape(t, d)
    row = lambda v: v.reshape(1, -1)
    assert depth % 2 == 0, "the final RMSNorm is fused into the last routed layer's combine"
    fold = lambda w: w.reshape(w.shape[0], -1, w.shape[-1])
    w_in_b = w_in[0].astype(BF16)
    for layer in range(0, depth, 2):
        j = layer // 2
        odd = layer + 1
        u, wg_d, wu_d, w_out_b = _in_proj(xt, row(mix_norm[layer]), w_in_b,
                                           [(dense_wg, j), (dense_wu, j), (w_out, layer)])
        y, moe_wg_b, wd_d = _mixer_core(
            u, bsz, seq, conv_w[layer], row(conv_b[layer]), row(conv_ln_g[layer]), row(conv_ln_b[layer]),
            lru_conv_w[layer], row(lru_conv_b[layer]), lru_wa[layer].astype(BF16), row(lru_ba[layer]),
            lru_wx[layer].astype(BF16), row(lru_bx[layer]), row(lru_lambda[layer]),
            [(fold(moe_wg), j), (dense_wd, j)])
        xt, w_in_odd = _out_proj(xt, y, w_out_b, [(w_in, odd)])
        xt, moe_wd_b = _dense_ffn(xt, row(ffn_norm[layer]), wg_d, wu_d, wd_d, [(fold(moe_wd), j)])

        hosted = [(w_out, odd)] + ([(w_in, odd + 1)] if odd + 1 < depth else [])
        u, w_out_b, *nxt = _in_proj(xt, row(mix_norm[odd]), w_in_odd, hosted)
        if nxt:
            w_in_b = nxt[0]
        y, moe_wu_b = _mixer_core(
            u, bsz, seq, conv_w[odd], row(conv_b[odd]), row(conv_ln_g[odd]), row(conv_ln_b[odd]),
            lru_conv_w[odd], row(lru_conv_b[odd]), lru_wa[odd].astype(BF16), row(lru_ba[odd]),
            lru_wx[odd].astype(BF16), row(lru_bx[odd]), row(lru_lambda[odd]), [(fold(moe_wu), j)])
        wr = jnp.zeros((d, LANES), BF16).at[:, 0:N_EXPERTS].set(w_router[j].astype(BF16))
        g = row(ffn_norm[odd])
        xt, info = _out_proj(xt, y, w_out_b, [], router=(g, wr))
        tok, dst, block_e, n_real = _routing_plan(info, t)
        y = _moe_ffn(tok, dst, block_e, n_real, xt, g, moe_wg_b.reshape(moe_wg[j].shape),
                     moe_wu_b.reshape(moe_wu[j].shape), moe_wd_b.reshape(moe_wd[j].shape))
        xt = _combine(xt, info, row(final_norm), y, odd == depth - 1)
    return xt.reshape(bsz, seq, d)
```

```python
import functools

import jax
import jax.numpy as jnp
from jax import lax
from jax.experimental import pallas as pl
from jax.experimental.pallas import tpu as pltpu

F32 = jnp.float32
BF16 = jnp.bfloat16

D_MODEL = 2048
CONV_WIDTH = 1024
LRU_WIDTH = 1024
LRU_HEADS = 8
LRU_HEAD_DIM = LRU_WIDTH // LRU_HEADS
D_IN = 2 * CONV_WIDTH + 2 * LRU_WIDTH
CONV_K = 31
LRU_CONV_K = 4
LRU_C = 8.0
N_EXPERTS = 8
TOP_K = 2
EPS = 1e-6

LANES = 128
SUBLANES = 8
BF16_ROWS = 16
MIB = 1024 * 1024

TM_PROJ = 512
TN_IN = 1024
TS_MIX = 256
CONV_HALO = 32
LRU_HALO = 8
CONV_ROWS = 64
TM_FFN = 512
TF_FFN = 1024
TF_MOE = 768
MOE_BM = 512
TM_COMBINE = 512


def _cparams(semantics, vmem_mib):
    return pltpu.CompilerParams(dimension_semantics=semantics, vmem_limit_bytes=vmem_mib * MIB)


def _sigmoid(x):
    return 0.5 * jnp.tanh(0.5 * x) + 0.5


def _rms_norm(x, g):
    ms = jnp.mean(x * x, axis=-1, keepdims=True)
    return x * lax.rsqrt(ms + EPS) * g


def _cast_specs(casts, n_steps, index_map):
    in_specs, out_specs = [], []
    for w, layer in casts:
        _, rows, cols = w.shape
        assert rows % n_steps == 0 and (rows // n_steps) % BF16_ROWS == 0, (w.shape, n_steps)
        in_specs.append(pl.BlockSpec((None, rows // n_steps, cols),
                                     lambda *idx, layer=layer: (layer, *index_map(*idx))))
        out_specs.append(pl.BlockSpec((rows // n_steps, cols), index_map))
    return in_specs, out_specs


def _cast_shapes(casts):
    return [jax.ShapeDtypeStruct(w.shape[1:], BF16) for w, _ in casts]


def _cast_args(casts):
    return [w for w, _ in casts]


def _run_casts(in_refs, out_refs):
    for src, dst in zip(in_refs, out_refs, strict=True):
        dst[...] = src[...].astype(dst.dtype)


def _in_proj_kernel(x_ref, g_ref, w_ref, *refs, n_cast):
    cast_in, o_ref, cast_out = refs[:n_cast], refs[n_cast], refs[n_cast + 1:]
    _run_casts(cast_in, cast_out)
    h = _rms_norm(x_ref[...], g_ref[...]).astype(BF16)
    for j in range(o_ref.shape[1] // TN_IN):
        cols = slice(j * TN_IN, (j + 1) * TN_IN)
        o_ref[:, cols] = jnp.dot(h, w_ref[:, cols], preferred_element_type=F32)


def _in_proj(x, g, w, casts):
    t, d = x.shape
    n = w.shape[1]
    steps = t // TM_PROJ
    cast_in, cast_out = _cast_specs(casts, steps, lambda i: (i, 0))
    return pl.pallas_call(
        functools.partial(_in_proj_kernel, n_cast=len(casts)),
        grid=(steps,),
        in_specs=[
            pl.BlockSpec((TM_PROJ, d), lambda i: (i, 0)),
            pl.BlockSpec((1, d), lambda i: (0, 0)),
            pl.BlockSpec((d, n), lambda i: (0, 0), pipeline_mode=pl.Buffered(1)),
            *cast_in,
        ],
        out_specs=[pl.BlockSpec((TM_PROJ, n), lambda i: (i, 0)), *cast_out],
        out_shape=[jax.ShapeDtypeStruct((t, n), F32), *_cast_shapes(casts)],
        compiler_params=_cparams(("parallel",), 60),
        name="in_proj",
    )(x, g, w, *_cast_args(casts))


def _causal_depthwise_conv(buf, w_ref, b_ref, out_ref, ts, halo, n_taps):
    base = halo - (n_taps - 1)
    nt = CONV_ROWS // SUBLANES
    nq = (base + n_taps - 1) // SUBLANES + 1
    row8 = lax.broadcasted_iota(jnp.int32, (SUBLANES, LANES), 0)
    for rc in range(ts // CONV_ROWS):
        t0 = rc * CONV_ROWS
        for lc in range(buf.shape[1] // LANES):
            cols = slice(lc * LANES, (lc + 1) * LANES)
            xs = [buf[t0 + SUBLANES * j:t0 + SUBLANES * (j + 1), cols] for j in range(nt + nq - 1)]
            bias = jnp.broadcast_to(b_ref[:, cols], (SUBLANES, LANES))
            out = [bias] * nt
            for p in range(SUBLANES):
                taps = [(q, SUBLANES * q + p - base) for q in range(nq)
                        if 0 <= SUBLANES * q + p - base < n_taps]
                if not taps:
                    continue
                ws = [jnp.broadcast_to(w_ref[k:k + 1, cols], (SUBLANES, LANES)) for _, k in taps]
                zs = []
                for j in range(nt + (1 if p else 0)):
                    z = ws[0] * xs[j + taps[0][0]]
                    for w, (q, _) in zip(ws[1:], taps[1:]):
                        z = z + w * xs[j + q]
                    zs.append(z)
                if p == 0:
                    out = [o + z for o, z in zip(out, zs)]
                else:
                    rs = [pltpu.roll(z, SUBLANES - p, axis=0) for z in zs]
                    keep = row8 < SUBLANES - p
                    out = [o + jnp.where(keep, rs[g], rs[g + 1]) for g, o in enumerate(out)]
            for g in range(nt):
                out_ref[t0 + SUBLANES * g:t0 + SUBLANES * (g + 1), cols] = out[g]


def _mixer_kernel(u_ref, cw_ref, cb_ref, lg_ref, lb_ref, rw_ref, rb_ref, wa_ref, ba_ref,
                  wx_ref, bx_ref, lam_ref, *refs, n_cast):
    cast_in, o_ref, cast_out = refs[:n_cast], refs[n_cast], refs[n_cast + 1:2 * n_cast + 1]
    cbuf, rbuf, abuf, bbuf, hc_ref = refs[2 * n_cast + 1:]
    s = pl.program_id(1)
    ts = TS_MIX
    _run_casts(cast_in, cast_out)

    @pl.when(s == 0)
    def _():
        cbuf[0:CONV_HALO, :] = jnp.zeros((CONV_HALO, CONV_WIDTH), F32)
        rbuf[0:LRU_HALO, :] = jnp.zeros((LRU_HALO, LRU_WIDTH), F32)
        hc_ref[...] = jnp.zeros((SUBLANES, LRU_WIDTH), F32)

    val = u_ref[:, 0:CONV_WIDTH]
    gate = u_ref[:, CONV_WIDTH:2 * CONV_WIDTH]
    cbuf[CONV_HALO:CONV_HALO + ts, :] = val * _sigmoid(gate)
    _causal_depthwise_conv(cbuf, cw_ref, cb_ref, abuf, ts, CONV_HALO, CONV_K)
    cbuf[0:CONV_HALO, :] = cbuf[ts:ts + CONV_HALO, :]
    c = abuf[...]
    mu = jnp.mean(c, axis=-1, keepdims=True)
    cc = c - mu
    var = jnp.mean(cc * cc, axis=-1, keepdims=True)
    cn = cc * lax.rsqrt(var + EPS) * lg_ref[...] + lb_ref[...]
    o_ref[:, 0:CONV_WIDTH] = (cn * _sigmoid(cn)).astype(o_ref.dtype)

    rbuf[LRU_HALO:LRU_HALO + ts, :] = u_ref[:, 2 * CONV_WIDTH:2 * CONV_WIDTH + LRU_WIDTH]
    _causal_depthwise_conv(rbuf, rw_ref, rb_ref, bbuf, ts, LRU_HALO, LRU_CONV_K)
    rbuf[0:LRU_HALO, :] = rbuf[ts:ts + LRU_HALO, :]
    for h in range(LRU_HEADS):
        cols = slice(h * LRU_HEAD_DIM, (h + 1) * LRU_HEAD_DIM)
        xh = bbuf[:, cols]
        xh_b = xh.astype(BF16)
        ga = jnp.dot(xh_b, wa_ref[h], preferred_element_type=F32) + ba_ref[:, cols]
        gx = jnp.dot(xh_b, wx_ref[h], preferred_element_type=F32) + bx_ref[:, cols]
        lam = lam_ref[:, cols]
        e = jnp.exp(-jnp.abs(lam))
        e1 = 1.0 + e
        log1p_e = jnp.where(e1 == 1.0, e, jnp.log(e1) * (e / (e1 - 1.0)))
        sp = jnp.maximum(-lam, 0.0) + log1p_e
        log_a = (-LRU_C) * _sigmoid(ga) * sp
        a = jnp.exp(log_a)
        mult = jnp.sqrt(-jnp.tanh(log_a) * (1.0 + a * a))
        row = lax.broadcasted_iota(jnp.int32, (ts, LRU_HEAD_DIM), 0)
        mult = jnp.where((row == 0) & (s == 0), 1.0, mult)
        abuf[:, cols] = a
        bbuf[:, cols] = mult * (_sigmoid(gx) * xh)

    row8 = lax.broadcasted_iota(jnp.int32, (SUBLANES, LRU_WIDTH), 0)
    h_prev = hc_ref[...]
    for g in range(ts // SUBLANES):
        rows = slice(g * SUBLANES, (g + 1) * SUBLANES)
        a = abuf[rows, :]
        b = bbuf[rows, :]
        for dsh in (1, 2, 4):
            a_s = pltpu.roll(a, dsh, axis=0)
            b_s = pltpu.roll(b, dsh, axis=0)
            m = row8 >= dsh
            b = jnp.where(m, a * b_s + b, b)
            a = jnp.where(m, a * a_s, a)
        hg = a * h_prev + b
        abuf[rows, :] = hg
        h_prev = jnp.broadcast_to(hg[SUBLANES - 1:SUBLANES, :], (SUBLANES, LRU_WIDTH))
    hc_ref[...] = h_prev

    rg = u_ref[:, 2 * CONV_WIDTH + LRU_WIDTH:D_IN]
    gelu = 0.5 * rg * (1.0 + jnp.tanh(0.7978845608028654 * (rg + 0.044715 * (rg * rg * rg))))
    o_ref[:, CONV_WIDTH:CONV_WIDTH + LRU_WIDTH] = (abuf[...] * gelu).astype(o_ref.dtype)


def _mixer_core(u, bsz, seq, cw, cb, lg, lb, rw, rb, wa, ba, wx, bx, lam, casts):
    t = u.shape[0]
    nsb = seq // TS_MIX
    row = lambda n: pl.BlockSpec((1, n), lambda b, s: (0, 0))
    cast_in, cast_out = _cast_specs(casts, bsz * nsb, lambda b, s: (b * nsb + s, 0))
    return pl.pallas_call(
        functools.partial(_mixer_kernel, n_cast=len(casts)),
        grid=(bsz, nsb),
        in_specs=[
            pl.BlockSpec((TS_MIX, D_IN), lambda b, s: (b * nsb + s, 0)),
            pl.BlockSpec((CONV_K, CONV_WIDTH), lambda b, s: (0, 0)),
            row(CONV_WIDTH), row(CONV_WIDTH), row(CONV_WIDTH),
            pl.BlockSpec((LRU_CONV_K, LRU_WIDTH), lambda b, s: (0, 0)),
            row(LRU_WIDTH),
            pl.BlockSpec((LRU_HEADS, LRU_HEAD_DIM, LRU_HEAD_DIM), lambda b, s: (0, 0, 0)),
            row(LRU_WIDTH),
            pl.BlockSpec((LRU_HEADS, LRU_HEAD_DIM, LRU_HEAD_DIM), lambda b, s: (0, 0, 0)),
            row(LRU_WIDTH), row(LRU_WIDTH),
            *cast_in,
        ],
        out_specs=[pl.BlockSpec((TS_MIX, CONV_WIDTH + LRU_WIDTH), lambda b, s: (b * nsb + s, 0)), *cast_out],
        out_shape=[jax.ShapeDtypeStruct((t, CONV_WIDTH + LRU_WIDTH), BF16), *_cast_shapes(casts)],
        scratch_shapes=[
            pltpu.VMEM((CONV_HALO + TS_MIX, CONV_WIDTH), F32),
            pltpu.VMEM((LRU_HALO + TS_MIX, LRU_WIDTH), F32),
            pltpu.VMEM((TS_MIX, LRU_WIDTH), F32),
            pltpu.VMEM((TS_MIX, LRU_WIDTH), F32),
            pltpu.VMEM((SUBLANES, LRU_WIDTH), F32),
        ],
        compiler_params=_cparams(("parallel", "arbitrary"), 56),
        name="mixer_core",
    )(u, cw, cb, lg, lb, rw, rb, wa, ba, wx, bx, lam, *_cast_args(casts))


def _top2_info(logits, width):
    lane = lax.broadcasted_iota(jnp.int32, logits.shape, 1)
    neg = jnp.float32(-jnp.inf)
    l1 = jnp.where(lane < N_EXPERTS, logits, neg)
    m1 = jnp.max(l1, axis=-1, keepdims=True)
    i1 = jnp.min(jnp.where(l1 == m1, lane, LANES), axis=-1, keepdims=True)
    l2 = jnp.where(lane == i1, neg, l1)
    m2 = jnp.max(l2, axis=-1, keepdims=True)
    i2 = jnp.min(jnp.where(l2 == m2, lane, LANES), axis=-1, keepdims=True)
    dlt = jnp.exp(m2 - m1)
    w1 = 1.0 / (1.0 + dlt)
    w2 = dlt / (1.0 + dlt)
    out = jnp.where(lane == 0, i1.astype(F32),
                    jnp.where(lane == 1, i2.astype(F32),
                              jnp.where(lane == 2, w1, jnp.where(lane == 3, w2, 0.0))))
    return out[:, 0:width]


def _out_proj_kernel(x_ref, y_ref, w_ref, *refs, n_cast, route):
    refs = list(refs)
    if route:
        gf_ref, wr_ref = refs[:2]
        refs = refs[2:]
    cast_in, refs = refs[:n_cast], refs[n_cast:]
    o_ref, refs = refs[0], refs[1:]
    if route:
        info_ref, refs = refs[0], refs[1:]
    cast_out = refs
    _run_casts(cast_in, cast_out)
    xn = x_ref[...] + jnp.dot(y_ref[...], w_ref[...], preferred_element_type=F32)
    o_ref[...] = xn
    if route:
        h = _rms_norm(xn, gf_ref[...]).astype(BF16)
        logits = jnp.dot(h, wr_ref[...], preferred_element_type=F32)
        info_ref[...] = _top2_info(logits, info_ref.shape[1])


def _out_proj(x, y, w, casts, router=None):
    t, d = x.shape
    k = y.shape[1]
    steps = t // TM_PROJ
    cast_in, cast_out = _cast_specs(casts, steps, lambda i: (i, 0))
    const = lambda shape: pl.BlockSpec(shape, lambda i: (0, 0))
    route_in = [] if router is None else [const((1, d)), const((d, LANES))]
    route_out = [] if router is None else [pl.BlockSpec((TM_PROJ, SUBLANES), lambda i: (i, 0))]
    route_shape = [] if router is None else [jax.ShapeDtypeStruct((t, SUBLANES), F32)]
    return pl.pallas_call(
        functools.partial(_out_proj_kernel, n_cast=len(casts), route=router is not None),
        grid=(steps,),
        in_specs=[
            pl.BlockSpec((TM_PROJ, d), lambda i: (i, 0)),
            pl.BlockSpec((TM_PROJ, k), lambda i: (i, 0)),
            const((k, d)),
            *route_in,
            *cast_in,
        ],
        out_specs=[pl.BlockSpec((TM_PROJ, d), lambda i: (i, 0)), *route_out, *cast_out],
        out_shape=[jax.ShapeDtypeStruct((t, d), F32), *route_shape, *_cast_shapes(casts)],
        compiler_params=_cparams(("parallel",), 56),
        name="out_proj",
    )(x, y, w, *(router or ()), *_cast_args(casts))


def _swiglu_step(h, wg, wu, wd):
    g = jnp.dot(h, wg, preferred_element_type=F32)
    v = jnp.dot(h, wu, preferred_element_type=F32)
    act = (g * _sigmoid(g) * v).astype(BF16)
    return jnp.dot(act, wd, preferred_element_type=F32)


def _dense_ffn_kernel(x_ref, g_ref, wg_ref, wu_ref, wd_ref, *refs, n_cast):
    cast_in, o_ref, cast_out, h_ref = refs[:n_cast], refs[n_cast], refs[n_cast + 1:2 * n_cast + 1], refs[-1]
    _run_casts(cast_in, cast_out)

    @pl.when(pl.program_id(1) == 0)
    def _():
        h_ref[...] = _rms_norm(x_ref[...], g_ref[...]).astype(BF16)
        o_ref[...] = x_ref[...]

    o_ref[...] += _swiglu_step(h_ref[...], wg_ref[...], wu_ref[...], wd_ref[...])


def _dense_ffn(x, g, wg, wu, wd, casts):
    t, d = x.shape
    ff = wg.shape[1]
    nf = ff // TF_FFN
    cast_in, cast_out = _cast_specs(casts, t // TM_FFN * nf, lambda i, f: (i * nf + f, 0))
    return pl.pallas_call(
        functools.partial(_dense_ffn_kernel, n_cast=len(casts)),
        grid=(t // TM_FFN, nf),
        in_specs=[
            pl.BlockSpec((TM_FFN, d), lambda i, f: (i, 0)),
            pl.BlockSpec((1, d), lambda i, f: (0, 0)),
            pl.BlockSpec((d, TF_FFN), lambda i, f: (0, f)),
            pl.BlockSpec((d, TF_FFN), lambda i, f: (0, f)),
            pl.BlockSpec((TF_FFN, d), lambda i, f: (f, 0)),
            *cast_in,
        ],
        out_specs=[pl.BlockSpec((TM_FFN, d), lambda i, f: (i, 0)), *cast_out],
        out_shape=[jax.ShapeDtypeStruct((t, d), F32), *_cast_shapes(casts)],
        scratch_shapes=[pltpu.VMEM((TM_FFN, d), BF16)],
        compiler_params=_cparams(("parallel", "arbitrary"), 60),
        name="dense_ffn",
    )(x, g, wg, wu, wd, *_cast_args(casts))


def _moe_ffn_kernel(tok_ref, dst_ref, be_ref, nr_ref, x_hbm, g_ref, wg_ref, wu_ref, wd_ref, y_hbm,
                    xg_ref, h_ref, acc_ref, gsem, ssem, *, n_tok):
    b = pl.program_id(0)
    f = pl.program_id(1)
    nb = pl.num_programs(0)
    nf = pl.num_programs(1)
    n_real = nr_ref[0]
    real = b < n_real
    slot = b % 2
    rows_per_step = MOE_BM // nf

    def gather_start(tok, r, slt):
        pltpu.make_async_copy(x_hbm.at[pl.ds(tok, 1), :], xg_ref.at[slt, pl.ds(r, 1), :],
                              gsem.at[slt]).start()

    def scatter_start(a, r, slt):
        pltpu.make_async_copy(acc_ref.at[slt, pl.ds(r, 1), :], y_hbm.at[pl.ds(a, 1), :], ssem).start()

    def gather_wait(slt):
        pltpu.make_async_copy(x_hbm.at[pl.ds(0, MOE_BM), :], xg_ref.at[slt], gsem.at[slt]).wait()

    def scatter_wait():
        pltpu.make_async_copy(acc_ref.at[0], y_hbm.at[pl.ds(0, MOE_BM), :], ssem).wait()

    def drain(blk):
        gather_wait((blk + 1) % 2)

        def body(r, carry):
            scatter_start(dst_ref[blk * MOE_BM + r], r, blk % 2)
            return carry

        lax.fori_loop(0, MOE_BM, body, 0, unroll=8)
        scatter_wait()

    @pl.when(real & (f == 0))
    def _():
        @pl.when(b == 0)
        def _():
            def body(r, carry):
                gather_start(tok_ref[r], r, 0)
                return carry

            lax.fori_loop(0, MOE_BM, body, 0, unroll=8)
            acc_ref[1] = jnp.zeros(acc_ref.shape[1:], acc_ref.dtype)

        gather_wait(slot)
        h_ref[...] = _rms_norm(xg_ref[slot], g_ref[...]).astype(BF16)
        acc_ref[slot] = jnp.zeros(acc_ref.shape[1:], acc_ref.dtype)

    @pl.when(real)
    def _():
        prev_base = jnp.maximum(b - 1, 0) * MOE_BM
        for i in range(rows_per_step):
            r = f * rows_per_step + i
            gather_start(tok_ref[(b + 1) * MOE_BM + r], r, 1 - slot)
            a_prev = jnp.where(b == 0, TOP_K * n_tok + r, dst_ref[prev_base + r])
            scatter_start(a_prev, r, 1 - slot)
        acc_ref[slot] += _swiglu_step(h_ref[...], wg_ref[0], wu_ref[0], wd_ref[0])

        @pl.when(f == nf - 1)
        def _():
            scatter_wait()

    @pl.when((b == n_real) & (f == 0))
    def _():
        drain(b - 1)

    @pl.when(real & (b == nb - 1) & (f == nf - 1))
    def _():
        drain(b)


def _moe_ffn(tok, dst, block_e, n_real, x, g, wg, wu, wd):
    t, d = x.shape
    ff = wg.shape[2]
    nf = ff // TF_MOE
    n_blocks = block_e.shape[0]
    assert MOE_BM % nf == 0
    assert tok.shape[0] == (n_blocks + 1) * MOE_BM == dst.shape[0]

    def f_eff(b, f, nr):
        return jnp.where(b < nr[0], f, nf - 1)

    grid_spec = pltpu.PrefetchScalarGridSpec(
        num_scalar_prefetch=4,
        grid=(n_blocks, nf),
        in_specs=[
            pl.BlockSpec(memory_space=pl.ANY),
            pl.BlockSpec((1, d), lambda b, f, tok, dst, be, nr: (0, 0)),
            pl.BlockSpec((1, d, TF_MOE), lambda b, f, tok, dst, be, nr: (be[b], 0, f_eff(b, f, nr))),
            pl.BlockSpec((1, d, TF_MOE), lambda b, f, tok, dst, be, nr: (be[b], 0, f_eff(b, f, nr))),
            pl.BlockSpec((1, TF_MOE, d), lambda b, f, tok, dst, be, nr: (be[b], f_eff(b, f, nr), 0)),
        ],
        out_specs=pl.BlockSpec(memory_space=pl.ANY),
        scratch_shapes=[
            pltpu.VMEM((2, MOE_BM, d), F32),
            pltpu.VMEM((MOE_BM, d), BF16),
            pltpu.VMEM((2, MOE_BM, d), F32),
            pltpu.SemaphoreType.DMA((2,)),
            pltpu.SemaphoreType.DMA(()),
        ],
    )
    return pl.pallas_call(
        functools.partial(_moe_ffn_kernel, n_tok=t),
        grid_spec=grid_spec,
        out_shape=jax.ShapeDtypeStruct((TOP_K * t + MOE_BM, d), F32),
        compiler_params=_cparams(("arbitrary", "arbitrary"), 56),
        name="moe_ffn",
    )(tok, dst, block_e, n_real, x, g, wg, wu, wd)


def _combine_kernel(x_ref, info_ref, g_ref, ya_ref, yb_ref, o_ref, *, final_norm):
    w1 = info_ref[:, 2:3]
    w2 = info_ref[:, 3:4]
    z = x_ref[...] + (ya_ref[...] * w1 + yb_ref[...] * w2)
    o_ref[...] = _rms_norm(z, g_ref[...]) if final_norm else z


def _combine(x, info, g, y, final_norm):
    t, d = x.shape
    nt = t // TM_COMBINE
    return pl.pallas_call(
        functools.partial(_combine_kernel, final_norm=final_norm),
        grid=(nt,),
        in_specs=[
            pl.BlockSpec((TM_COMBINE, d), lambda i: (i, 0)),
            pl.BlockSpec((TM_COMBINE, SUBLANES), lambda i: (i, 0)),
            pl.BlockSpec((1, d), lambda i: (0, 0)),
            pl.BlockSpec((TM_COMBINE, d), lambda i: (i, 0)),
            pl.BlockSpec((TM_COMBINE, d), lambda i: (nt + i, 0)),
        ],
        out_specs=pl.BlockSpec((TM_COMBINE, d), lambda i: (i, 0)),
        out_shape=jax.ShapeDtypeStruct((t, d), F32),
        compiler_params=_cparams(("parallel",), 48),
        name="moe_combine",
    )(x, info, g, y, y)


def _routing_plan(info, n_tok):
    n_asg = n_tok * TOP_K
    n_blocks = n_asg // MOE_BM + N_EXPERTS
    flat_e = info[:, 0:TOP_K].astype(jnp.int32).reshape(n_asg)
    onehot = (flat_e[:, None] == jnp.arange(N_EXPERTS, dtype=jnp.int32)[None, :]).astype(jnp.int32)
    csum = jnp.cumsum(onehot, axis=0)
    counts = csum[-1]
    padded = ((counts + MOE_BM - 1) // MOE_BM) * MOE_BM
    p_ends = jnp.cumsum(padded)
    p_starts = p_ends - padded
    pos = jnp.sum(onehot * (csum - 1 + p_starts[None, :]), axis=1).astype(jnp.int32)
    flat = jnp.arange(n_asg, dtype=jnp.int32)
    dest = (flat % TOP_K) * n_tok + flat // TOP_K
    spare = n_asg + jnp.arange((n_blocks + 1) * MOE_BM, dtype=jnp.int32) % MOE_BM
    dst = spare.at[pos].set(dest, unique_indices=True, mode="promise_in_bounds")
    tok = jnp.where(dst >= n_asg, 0, dst % n_tok)
    block_start = jnp.arange(n_blocks, dtype=jnp.int32) * MOE_BM
    block_e = jnp.minimum(jnp.searchsorted(p_ends, block_start, side="right"), N_EXPERTS - 1).astype(jnp.int32)
    n_real = (p_ends[-1] // MOE_BM).astype(jnp.int32).reshape(1)
    return tok, dst, block_e, n_real


def kernel(x, mix_norm, w_in, conv_w, conv_b, conv_ln_g, conv_ln_b, lru_conv_w, lru_conv_b, lru_wa, lru_ba, lru_wx, lru_bx, lru_lambda, w_out, ffn_norm, dense_wg, dense_wu, dense_wd, w_router, moe_wg, moe_wu, moe_wd, final_norm):
    bsz, seq, d = x.shape
    depth = w_in.shape[0]
    t = bsz * seq
    xt = x.reshape(t, d)
    row = lambda v: v.reshape(1, -1)
    assert depth % 2 == 0, "the final RMSNorm is fused into the last routed layer's combine"
    fold = lambda w: w.reshape(w.shape[0], -1, w.shape[-1])
    w_in_b = w_in[0].astype(BF16)
    for layer in range(0, depth, 2):
        j = layer // 2
        odd = layer + 1
        u, wg_d, wu_d, w_out_b = _in_proj(xt, row(mix_norm[layer]), w_in_b,
                                           [(dense_wg, j), (dense_wu, j), (w_out, layer)])
        y, moe_wg_b, wd_d = _mixer_core(
            u, bsz, seq, conv_w[layer], row(conv_b[layer]), row(conv_ln_g[layer]), row(conv_ln_b[layer]),
            lru_conv_w[layer], row(lru_conv_b[layer]), lru_wa[layer].astype(BF16), row(lru_ba[layer]),
            lru_wx[layer].astype(BF16), row(lru_bx[layer]), row(lru_lambda[layer]),
            [(fold(moe_wg), j), (dense_wd, j)])
        xt, w_in_odd = _out_proj(xt, y, w_out_b, [(w_in, odd)])
        xt, moe_wd_b = _dense_ffn(xt, row(ffn_norm[layer]), wg_d, wu_d, wd_d, [(fold(moe_wd), j)])

        hosted = [(w_out, odd)] + ([(w_in, odd + 1)] if odd + 1 < depth else [])
        u, w_out_b, *nxt = _in_proj(xt, row(mix_norm[odd]), w_in_odd, hosted)
        if nxt:
            w_in_b = nxt[0]
        y, moe_wu_b = _mixer_core(
            u, bsz, seq, conv_w[odd], row(conv_b[odd]), row(conv_ln_g[odd]), row(conv_ln_b[odd]),
            lru_conv_w[odd], row(lru_conv_b[odd]), lru_wa[odd].astype(BF16), row(lru_ba[odd]),
            lru_wx[odd].astype(BF16), row(lru_bx[odd]), row(lru_lambda[odd]), [(fold(moe_wu), j)])
        wr = jnp.zeros((d, LANES), BF16).at[:, 0:N_EXPERTS].set(w_router[j].astype(BF16))
        g = row(ffn_norm[odd])
        xt, info = _out_proj(xt, y, w_out_b, [], router=(g, wr))
        tok, dst, block_e, n_real = _routing_plan(info, t)
        y = _moe_ffn(tok, dst, block_e, n_real, xt, g, moe_wg_b.reshape(moe_wg[j].shape),
                     moe_wu_b.reshape(moe_wu[j].shape), moe_wd_b.reshape(moe_wd[j].shape))
        xt = _combine(xt, info, row(final_norm), y, odd == depth - 1)
    return xt.reshape(bsz, seq, d)
```

```python
import functools

import jax
import jax.numpy as jnp
from jax import lax
from jax.experimental import pallas as pl
from jax.experimental.pallas import tpu as pltpu

F32 = jnp.float32
BF16 = jnp.bfloat16

D_MODEL = 2048
CONV_WIDTH = 1024
LRU_WIDTH = 1024
LRU_HEADS = 8
LRU_HEAD_DIM = LRU_WIDTH // LRU_HEADS
D_IN = 2 * CONV_WIDTH + 2 * LRU_WIDTH
CONV_K = 31
LRU_CONV_K = 4
LRU_C = 8.0
N_EXPERTS = 8
TOP_K = 2
EPS = 1e-6

LANES = 128
SUBLANES = 8
BF16_ROWS = 16
MIB = 1024 * 1024

TM_PROJ = 512
TN_IN = 1024
TS_MIX = 256
CONV_HALO = 32
LRU_HALO = 8
CONV_ROWS = 64
TM_FFN = 512
TF_FFN = 1024
TF_MOE = 768
MOE_BM = 512
MOE_XG_SLOTS = 3
TM_COMBINE = 512


def _cparams(semantics, vmem_mib):
    return pltpu.CompilerParams(dimension_semantics=semantics, vmem_limit_bytes=vmem_mib * MIB)


def _sigmoid(x):
    return 0.5 * jnp.tanh(0.5 * x) + 0.5


def _rms_norm(x, g):
    ms = jnp.mean(x * x, axis=-1, keepdims=True)
    return x * lax.rsqrt(ms + EPS) * g


def _cast_specs(casts, n_steps, index_map):
    in_specs, out_specs = [], []
    for w, layer in casts:
        _, rows, cols = w.shape
        assert rows % n_steps == 0 and (rows // n_steps) % BF16_ROWS == 0, (w.shape, n_steps)
        in_specs.append(pl.BlockSpec((None, rows // n_steps, cols),
                                     lambda *idx, layer=layer: (layer, *index_map(*idx))))
        out_specs.append(pl.BlockSpec((rows // n_steps, cols), index_map))
    return in_specs, out_specs


def _cast_shapes(casts):
    return [jax.ShapeDtypeStruct(w.shape[1:], BF16) for w, _ in casts]


def _cast_args(casts):
    return [w for w, _ in casts]


def _run_casts(in_refs, out_refs):
    for src, dst in zip(in_refs, out_refs, strict=True):
        dst[...] = src[...].astype(dst.dtype)


def _in_proj_kernel(x_ref, g_ref, w_ref, *refs, n_cast):
    cast_in, o_ref, cast_out = refs[:n_cast], refs[n_cast], refs[n_cast + 1:]
    _run_casts(cast_in, cast_out)
    h = _rms_norm(x_ref[...], g_ref[...]).astype(BF16)
    for j in range(o_ref.shape[1] // TN_IN):
        cols = slice(j * TN_IN, (j + 1) * TN_IN)
        o_ref[:, cols] = jnp.dot(h, w_ref[:, cols], preferred_element_type=F32)


def _in_proj(x, g, w, casts):
    t, d = x.shape
    n = w.shape[1]
    steps = t // TM_PROJ
    cast_in, cast_out = _cast_specs(casts, steps, lambda i: (i, 0))
    return pl.pallas_call(
        functools.partial(_in_proj_kernel, n_cast=len(casts)),
        grid=(steps,),
        in_specs=[
            pl.BlockSpec((TM_PROJ, d), lambda i: (i, 0)),
            pl.BlockSpec((1, d), lambda i: (0, 0)),
            pl.BlockSpec((d, n), lambda i: (0, 0), pipeline_mode=pl.Buffered(1)),
            *cast_in,
        ],
        out_specs=[pl.BlockSpec((TM_PROJ, n), lambda i: (i, 0)), *cast_out],
        out_shape=[jax.ShapeDtypeStruct((t, n), F32), *_cast_shapes(casts)],
        compiler_params=_cparams(("parallel",), 60),
        name="in_proj",
    )(x, g, w, *_cast_args(casts))


def _causal_depthwise_conv(buf, w_ref, b_ref, out_ref, ts, halo, n_taps):
    base = halo - (n_taps - 1)
    nt = CONV_ROWS // SUBLANES
    nq = (base + n_taps - 1) // SUBLANES + 1
    row8 = lax.broadcasted_iota(jnp.int32, (SUBLANES, LANES), 0)
    for rc in range(ts // CONV_ROWS):
        t0 = rc * CONV_ROWS
        for lc in range(buf.shape[1] // LANES):
            cols = slice(lc * LANES, (lc + 1) * LANES)
            xs = [buf[t0 + SUBLANES * j:t0 + SUBLANES * (j + 1), cols] for j in range(nt + nq - 1)]
            bias = jnp.broadcast_to(b_ref[:, cols], (SUBLANES, LANES))
            out = [bias] * nt
            for p in range(SUBLANES):
                taps = [(q, SUBLANES * q + p - base) for q in range(nq)
                        if 0 <= SUBLANES * q + p - base < n_taps]
                if not taps:
                    continue
                ws = [jnp.broadcast_to(w_ref[k:k + 1, cols], (SUBLANES, LANES)) for _, k in taps]
                zs = []
                for j in range(nt + (1 if p else 0)):
                    z = ws[0] * xs[j + taps[0][0]]
                    for w, (q, _) in zip(ws[1:], taps[1:]):
                        z = z + w * xs[j + q]
                    zs.append(z)
                if p == 0:
                    out = [o + z for o, z in zip(out, zs)]
                else:
                    rs = [pltpu.roll(z, SUBLANES - p, axis=0) for z in zs]
                    keep = row8 < SUBLANES - p
                    out = [o + jnp.where(keep, rs[g], rs[g + 1]) for g, o in enumerate(out)]
            for g in range(nt):
                out_ref[t0 + SUBLANES * g:t0 + SUBLANES * (g + 1), cols] = out[g]


def _mixer_kernel(u_ref, cw_ref, cb_ref, lg_ref, lb_ref, rw_ref, rb_ref, wa_ref, ba_ref,
                  wx_ref, bx_ref, lam_ref, *refs, n_cast):
    cast_in, o_ref, cast_out = refs[:n_cast], refs[n_cast], refs[n_cast + 1:2 * n_cast + 1]
    cbuf, rbuf, abuf, bbuf, hc_ref = refs[2 * n_cast + 1:]
    s = pl.program_id(1)
    ts = TS_MIX
    _run_casts(cast_in, cast_out)

    @pl.when(s == 0)
    def _():
        cbuf[0:CONV_HALO, :] = jnp.zeros((CONV_HALO, CONV_WIDTH), F32)
        rbuf[0:LRU_HALO, :] = jnp.zeros((LRU_HALO, LRU_WIDTH), F32)
        hc_ref[...] = jnp.zeros((SUBLANES, LRU_WIDTH), F32)

    val = u_ref[:, 0:CONV_WIDTH]
    gate = u_ref[:, CONV_WIDTH:2 * CONV_WIDTH]
    cbuf[CONV_HALO:CONV_HALO + ts, :] = val * _sigmoid(gate)
    _causal_depthwise_conv(cbuf, cw_ref, cb_ref, abuf, ts, CONV_HALO, CONV_K)
    cbuf[0:CONV_HALO, :] = cbuf[ts:ts + CONV_HALO, :]
    c = abuf[...]
    mu = jnp.mean(c, axis=-1, keepdims=True)
    cc = c - mu
    var = jnp.mean(cc * cc, axis=-1, keepdims=True)
    cn = cc * lax.rsqrt(var + EPS) * lg_ref[...] + lb_ref[...]
    o_ref[:, 0:CONV_WIDTH] = (cn * _sigmoid(cn)).astype(o_ref.dtype)

    rbuf[LRU_HALO:LRU_HALO + ts, :] = u_ref[:, 2 * CONV_WIDTH:2 * CONV_WIDTH + LRU_WIDTH]
    _causal_depthwise_conv(rbuf, rw_ref, rb_ref, bbuf, ts, LRU_HALO, LRU_CONV_K)
    rbuf[0:LRU_HALO, :] = rbuf[ts:ts + LRU_HALO, :]
    for h in range(LRU_HEADS):
        cols = slice(h * LRU_HEAD_DIM, (h + 1) * LRU_HEAD_DIM)
        xh = bbuf[:, cols]
        xh_b = xh.astype(BF16)
        ga = jnp.dot(xh_b, wa_ref[h], preferred_element_type=F32) + ba_ref[:, cols]
        gx = jnp.dot(xh_b, wx_ref[h], preferred_element_type=F32) + bx_ref[:, cols]
        lam = lam_ref[:, cols]
        e = jnp.exp(-jnp.abs(lam))
        e1 = 1.0 + e
        log1p_e = jnp.where(e1 == 1.0, e, jnp.log(e1) * (e / (e1 - 1.0)))
        sp = jnp.maximum(-lam, 0.0) + log1p_e
        log_a = (-LRU_C) * _sigmoid(ga) * sp
        a = jnp.exp(log_a)
        mult = jnp.sqrt(-jnp.tanh(log_a) * (1.0 + a * a))
        row = lax.broadcasted_iota(jnp.int32, (ts, LRU_HEAD_DIM), 0)
        mult = jnp.where((row == 0) & (s == 0), 1.0, mult)
        abuf[:, cols] = a
        bbuf[:, cols] = mult * (_sigmoid(gx) * xh)

    row8 = lax.broadcasted_iota(jnp.int32, (SUBLANES, LRU_WIDTH), 0)
    h_prev = hc_ref[...]
    for g in range(ts // SUBLANES):
        rows = slice(g * SUBLANES, (g + 1) * SUBLANES)
        a = abuf[rows, :]
        b = bbuf[rows, :]
        for dsh in (1, 2, 4):
            a_s = pltpu.roll(a, dsh, axis=0)
            b_s = pltpu.roll(b, dsh, axis=0)
            m = row8 >= dsh
            b = jnp.where(m, a * b_s + b, b)
            a = jnp.where(m, a * a_s, a)
        hg = a * h_prev + b
        abuf[rows, :] = hg
        h_prev = jnp.broadcast_to(hg[SUBLANES - 1:SUBLANES, :], (SUBLANES, LRU_WIDTH))
    hc_ref[...] = h_prev

    rg = u_ref[:, 2 * CONV_WIDTH + LRU_WIDTH:D_IN]
    gelu = 0.5 * rg * (1.0 + jnp.tanh(0.7978845608028654 * (rg + 0.044715 * (rg * rg * rg))))
    o_ref[:, CONV_WIDTH:CONV_WIDTH + LRU_WIDTH] = (abuf[...] * gelu).astype(o_ref.dtype)


def _mixer_core(u, bsz, seq, cw, cb, lg, lb, rw, rb, wa, ba, wx, bx, lam, casts):
    t = u.shape[0]
    nsb = seq // TS_MIX
    row = lambda n: pl.BlockSpec((1, n), lambda b, s: (0, 0))
    cast_in, cast_out = _cast_specs(casts, bsz * nsb, lambda b, s: (b * nsb + s, 0))
    return pl.pallas_call(
        functools.partial(_mixer_kernel, n_cast=len(casts)),
        grid=(bsz, nsb),
        in_specs=[
            pl.BlockSpec((TS_MIX, D_IN), lambda b, s: (b * nsb + s, 0)),
            pl.BlockSpec((CONV_K, CONV_WIDTH), lambda b, s: (0, 0)),
            row(CONV_WIDTH), row(CONV_WIDTH), row(CONV_WIDTH),
            pl.BlockSpec((LRU_CONV_K, LRU_WIDTH), lambda b, s: (0, 0)),
            row(LRU_WIDTH),
            pl.BlockSpec((LRU_HEADS, LRU_HEAD_DIM, LRU_HEAD_DIM), lambda b, s: (0, 0, 0)),
            row(LRU_WIDTH),
            pl.BlockSpec((LRU_HEADS, LRU_HEAD_DIM, LRU_HEAD_DIM), lambda b, s: (0, 0, 0)),
            row(LRU_WIDTH), row(LRU_WIDTH),
            *cast_in,
        ],
        out_specs=[pl.BlockSpec((TS_MIX, CONV_WIDTH + LRU_WIDTH), lambda b, s: (b * nsb + s, 0)), *cast_out],
        out_shape=[jax.ShapeDtypeStruct((t, CONV_WIDTH + LRU_WIDTH), BF16), *_cast_shapes(casts)],
        scratch_shapes=[
            pltpu.VMEM((CONV_HALO + TS_MIX, CONV_WIDTH), F32),
            pltpu.VMEM((LRU_HALO + TS_MIX, LRU_WIDTH), F32),
            pltpu.VMEM((TS_MIX, LRU_WIDTH), F32),
            pltpu.VMEM((TS_MIX, LRU_WIDTH), F32),
            pltpu.VMEM((SUBLANES, LRU_WIDTH), F32),
        ],
        compiler_params=_cparams(("parallel", "arbitrary"), 56),
        name="mixer_core",
    )(u, cw, cb, lg, lb, rw, rb, wa, ba, wx, bx, lam, *_cast_args(casts))


def _top2_info(logits, width):
    lane = lax.broadcasted_iota(jnp.int32, logits.shape, 1)
    neg = jnp.float32(-jnp.inf)
    l1 = jnp.where(lane < N_EXPERTS, logits, neg)
    m1 = jnp.max(l1, axis=-1, keepdims=True)
    i1 = jnp.min(jnp.where(l1 == m1, lane, LANES), axis=-1, keepdims=True)
    l2 = jnp.where(lane == i1, neg, l1)
    m2 = jnp.max(l2, axis=-1, keepdims=True)
    i2 = jnp.min(jnp.where(l2 == m2, lane, LANES), axis=-1, keepdims=True)
    dlt = jnp.exp(m2 - m1)
    w1 = 1.0 / (1.0 + dlt)
    w2 = dlt / (1.0 + dlt)
    out = jnp.where(lane == 0, i1.astype(F32),
                    jnp.where(lane == 1, i2.astype(F32),
                              jnp.where(lane == 2, w1, jnp.where(lane == 3, w2, 0.0))))
    return out[:, 0:width]


def _out_proj_kernel(x_ref, y_ref, w_ref, *refs, n_cast, route):
    refs = list(refs)
    if route:
        gf_ref, wr_ref = refs[:2]
        refs = refs[2:]
    cast_in, refs = refs[:n_cast], refs[n_cast:]
    o_ref, refs = refs[0], refs[1:]
    if route:
        info_ref, refs = refs[0], refs[1:]
    cast_out = refs
    _run_casts(cast_in, cast_out)
    xn = x_ref[...] + jnp.dot(y_ref[...], w_ref[...], preferred_element_type=F32)
    o_ref[...] = xn
    if route:
        h = _rms_norm(xn, gf_ref[...]).astype(BF16)
        logits = jnp.dot(h, wr_ref[...], preferred_element_type=F32)
        info_ref[...] = _top2_info(logits, info_ref.shape[1])


def _out_proj(x, y, w, casts, router=None):
    t, d = x.shape
    k = y.shape[1]
    steps = t // TM_PROJ
    cast_in, cast_out = _cast_specs(casts, steps, lambda i: (i, 0))
    const = lambda shape: pl.BlockSpec(shape, lambda i: (0, 0))
    route_in = [] if router is None else [const((1, d)), const((d, LANES))]
    route_out = [] if router is None else [pl.BlockSpec((TM_PROJ, SUBLANES), lambda i: (i, 0))]
    route_shape = [] if router is None else [jax.ShapeDtypeStruct((t, SUBLANES), F32)]
    return pl.pallas_call(
        functools.partial(_out_proj_kernel, n_cast=len(casts), route=router is not None),
        grid=(steps,),
        in_specs=[
            pl.BlockSpec((TM_PROJ, d), lambda i: (i, 0)),
            pl.BlockSpec((TM_PROJ, k), lambda i: (i, 0)),
            const((k, d)),
            *route_in,
            *cast_in,
        ],
        out_specs=[pl.BlockSpec((TM_PROJ, d), lambda i: (i, 0)), *route_out, *cast_out],
        out_shape=[jax.ShapeDtypeStruct((t, d), F32), *route_shape, *_cast_shapes(casts)],
        compiler_params=_cparams(("parallel",), 56),
        name="out_proj",
    )(x, y, w, *(router or ()), *_cast_args(casts))


def _swiglu_step(h, wg, wu, wd):
    g = jnp.dot(h, wg, preferred_element_type=F32)
    v = jnp.dot(h, wu, preferred_element_type=F32)
    act = (g * _sigmoid(g) * v).astype(BF16)
    return jnp.dot(act, wd, preferred_element_type=F32)


def _dense_ffn_kernel(x_ref, g_ref, wg_ref, wu_ref, wd_ref, *refs, n_cast):
    cast_in, o_ref, cast_out, h_ref = refs[:n_cast], refs[n_cast], refs[n_cast + 1:2 * n_cast + 1], refs[-1]
    _run_casts(cast_in, cast_out)

    @pl.when(pl.program_id(1) == 0)
    def _():
        h_ref[...] = _rms_norm(x_ref[...], g_ref[...]).astype(BF16)
        o_ref[...] = x_ref[...]

    o_ref[...] += _swiglu_step(h_ref[...], wg_ref[...], wu_ref[...], wd_ref[...])


def _dense_ffn(x, g, wg, wu, wd, casts):
    t, d = x.shape
    ff = wg.shape[1]
    nf = ff // TF_FFN
    cast_in, cast_out = _cast_specs(casts, t // TM_FFN * nf, lambda i, f: (i * nf + f, 0))
    return pl.pallas_call(
        functools.partial(_dense_ffn_kernel, n_cast=len(casts)),
        grid=(t // TM_FFN, nf),
        in_specs=[
            pl.BlockSpec((TM_FFN, d), lambda i, f: (i, 0)),
            pl.BlockSpec((1, d), lambda i, f: (0, 0)),
            pl.BlockSpec((d, TF_FFN), lambda i, f: (0, f)),
            pl.BlockSpec((d, TF_FFN), lambda i, f: (0, f)),
            pl.BlockSpec((TF_FFN, d), lambda i, f: (f, 0)),
            *cast_in,
        ],
        out_specs=[pl.BlockSpec((TM_FFN, d), lambda i, f: (i, 0)), *cast_out],
        out_shape=[jax.ShapeDtypeStruct((t, d), F32), *_cast_shapes(casts)],
        scratch_shapes=[pltpu.VMEM((TM_FFN, d), BF16)],
        compiler_params=_cparams(("parallel", "arbitrary"), 60),
        name="dense_ffn",
    )(x, g, wg, wu, wd, *_cast_args(casts))


def _moe_ffn_kernel(tok_ref, dst_ref, be_ref, nr_ref, x_hbm, g_ref, wg_ref, wu_ref, wd_ref, y_hbm,
                    xg_ref, h_ref, acc_ref, gsem, ssem, *, n_tok):
    b = pl.program_id(0)
    f = pl.program_id(1)
    nb = pl.num_programs(0)
    nf = pl.num_programs(1)
    n_real = nr_ref[0]
    real = b < n_real
    slot = b % 2
    gslot = b % MOE_XG_SLOTS
    rows_per_step = MOE_BM // nf

    def gather_start(tok, r, slt):
        pltpu.make_async_copy(x_hbm.at[pl.ds(tok, 1), :], xg_ref.at[slt, pl.ds(r, 1), :],
                              gsem.at[slt]).start()

    def scatter_start(a, r, slt):
        pltpu.make_async_copy(acc_ref.at[slt, pl.ds(r, 1), :], y_hbm.at[pl.ds(a, 1), :], ssem).start()

    def gather_wait(slt):
        pltpu.make_async_copy(x_hbm.at[pl.ds(0, MOE_BM), :], xg_ref.at[slt], gsem.at[slt]).wait()

    def scatter_wait():
        pltpu.make_async_copy(acc_ref.at[0], y_hbm.at[pl.ds(0, MOE_BM), :], ssem).wait()

    def drain(blk):
        gather_wait((blk + 1) % MOE_XG_SLOTS)
        gather_wait((blk + 2) % MOE_XG_SLOTS)
        scatter_wait()

        def body(r, carry):
            scatter_start(dst_ref[blk * MOE_BM + r], r, blk % 2)
            return carry

        lax.fori_loop(0, MOE_BM, body, 0, unroll=8)
        scatter_wait()

    @pl.when(real & (f == 0))
    def _():
        @pl.when(b == 0)
        def _():
            def body(r, carry):
                gather_start(tok_ref[r], r, 0)
                gather_start(tok_ref[MOE_BM + r], r, 1)
                return carry

            lax.fori_loop(0, MOE_BM, body, 0, unroll=8)
            acc_ref[1] = jnp.zeros(acc_ref.shape[1:], acc_ref.dtype)

        gather_wait(gslot)
        h_ref[...] = _rms_norm(xg_ref[gslot], g_ref[...]).astype(BF16)

        @pl.when(b > 0)
        def _():
            scatter_wait()

        acc_ref[slot] = jnp.zeros(acc_ref.shape[1:], acc_ref.dtype)

    @pl.when(real)
    def _():
        prev_base = jnp.maximum(b - 1, 0) * MOE_BM
        for i in range(rows_per_step):
            r = f * rows_per_step + i
            gather_start(tok_ref[(b + 2) * MOE_BM + r], r, (b + 2) % MOE_XG_SLOTS)
            a_prev = jnp.where(b == 0, TOP_K * n_tok + r, dst_ref[prev_base + r])
            scatter_start(a_prev, r, 1 - slot)
        acc_ref[slot] += _swiglu_step(h_ref[...], wg_ref[0], wu_ref[0], wd_ref[0])

    @pl.when((b == n_real) & (f == 0))
    def _():
        drain(b - 1)

    @pl.when(real & (b == nb - 1) & (f == nf - 1))
    def _():
        drain(b)


def _moe_ffn(tok, dst, block_e, n_real, x, g, wg, wu, wd):
    t, d = x.shape
    ff = wg.shape[2]
    nf = ff // TF_MOE
    n_blocks = block_e.shape[0]
    assert MOE_BM % nf == 0
    assert tok.shape[0] == (n_blocks + MOE_XG_SLOTS - 1) * MOE_BM == dst.shape[0]

    def f_eff(b, f, nr):
        return jnp.where(b < nr[0], f, nf - 1)

    grid_spec = pltpu.PrefetchScalarGridSpec(
        num_scalar_prefetch=4,
        grid=(n_blocks, nf),
        in_specs=[
            pl.BlockSpec(memory_space=pl.ANY),
            pl.BlockSpec((1, d), lambda b, f, tok, dst, be, nr: (0, 0)),
            pl.BlockSpec((1, d, TF_MOE), lambda b, f, tok, dst, be, nr: (be[b], 0, f_eff(b, f, nr))),
            pl.BlockSpec((1, d, TF_MOE), lambda b, f, tok, dst, be, nr: (be[b], 0, f_eff(b, f, nr))),
            pl.BlockSpec((1, TF_MOE, d), lambda b, f, tok, dst, be, nr: (be[b], f_eff(b, f, nr), 0)),
        ],
        out_specs=pl.BlockSpec(memory_space=pl.ANY),
        scratch_shapes=[
            pltpu.VMEM((MOE_XG_SLOTS, MOE_BM, d), F32),
            pltpu.VMEM((MOE_BM, d), BF16),
            pltpu.VMEM((2, MOE_BM, d), F32),
            pltpu.SemaphoreType.DMA((MOE_XG_SLOTS,)),
            pltpu.SemaphoreType.DMA(()),
        ],
    )
    return pl.pallas_call(
        functools.partial(_moe_ffn_kernel, n_tok=t),
        grid_spec=grid_spec,
        out_shape=jax.ShapeDtypeStruct((TOP_K * t + MOE_BM, d), F32),
        compiler_params=_cparams(("arbitrary", "arbitrary"), 56),
        name="moe_ffn",
    )(tok, dst, block_e, n_real, x, g, wg, wu, wd)


def _combine_kernel(x_ref, info_ref, g_ref, ya_ref, yb_ref, o_ref, *, final_norm):
    w1 = info_ref[:, 2:3]
    w2 = info_ref[:, 3:4]
    z = x_ref[...] + (ya_ref[...] * w1 + yb_ref[...] * w2)
    o_ref[...] = _rms_norm(z, g_ref[...]) if final_norm else z


def _combine(x, info, g, y, final_norm):
    t, d = x.shape
    nt = t // TM_COMBINE
    return pl.pallas_call(
        functools.partial(_combine_kernel, final_norm=final_norm),
        grid=(nt,),
        in_specs=[
            pl.BlockSpec((TM_COMBINE, d), lambda i: (i, 0)),
            pl.BlockSpec((TM_COMBINE, SUBLANES), lambda i: (i, 0)),
            pl.BlockSpec((1, d), lambda i: (0, 0)),
            pl.BlockSpec((TM_COMBINE, d), lambda i: (i, 0)),
            pl.BlockSpec((TM_COMBINE, d), lambda i: (nt + i, 0)),
        ],
        out_specs=pl.BlockSpec((TM_COMBINE, d), lambda i: (i, 0)),
        out_shape=jax.ShapeDtypeStruct((t, d), F32),
        compiler_params=_cparams(("parallel",), 48),
        name="moe_combine",
    )(x, info, g, y, y)


def _routing_plan(info, n_tok):
    n_asg = n_tok * TOP_K
    n_blocks = n_asg // MOE_BM + N_EXPERTS
    flat_e = info[:, 0:TOP_K].astype(jnp.int32).reshape(n_asg)
    onehot = (flat_e[:, None] == jnp.arange(N_EXPERTS, dtype=jnp.int32)[None, :]).astype(jnp.int32)
    csum = jnp.cumsum(onehot, axis=0)
    counts = csum[-1]
    padded = ((counts + MOE_BM - 1) // MOE_BM) * MOE_BM
    p_ends = jnp.cumsum(padded)
    p_starts = p_ends - padded
    pos = jnp.sum(onehot * (csum - 1 + p_starts[None, :]), axis=1).astype(jnp.int32)
    flat = jnp.arange(n_asg, dtype=jnp.int32)
    dest = (flat % TOP_K) * n_tok + flat // TOP_K
    spare = n_asg + jnp.arange((n_blocks + MOE_XG_SLOTS - 1) * MOE_BM, dtype=jnp.int32) % MOE_BM
    dst = spare.at[pos].set(dest, unique_indices=True, mode="promise_in_bounds")
    tok = jnp.where(dst >= n_asg, 0, dst % n_tok)
    block_start = jnp.arange(n_blocks, dtype=jnp.int32) * MOE_BM
    block_e = jnp.minimum(jnp.searchsorted(p_ends, block_start, side="right"), N_EXPERTS - 1).astype(jnp.int32)
    n_real = (p_ends[-1] // MOE_BM).astype(jnp.int32).reshape(1)
    return tok, dst, block_e, n_real


def kernel(x, mix_norm, w_in, conv_w, conv_b, conv_ln_g, conv_ln_b, lru_conv_w, lru_conv_b, lru_wa, lru_ba, lru_wx, lru_bx, lru_lambda, w_out, ffn_norm, dense_wg, dense_wu, dense_wd, w_router, moe_wg, moe_wu, moe_wd, final_norm):
    bsz, seq, d = x.shape
    depth = w_in.shape[0]
    t = bsz * seq
    xt = x.reshape(t, d)
    row = lambda v: v.reshape(1, -1)
    assert depth % 2 == 0, "the final RMSNorm is fused into the last routed layer's combine"
    fold = lambda w: w.reshape(w.shape[0], -1, w.shape[-1])
    w_in_b = w_in[0].astype(BF16)
    for layer in range(0, depth, 2):
        j = layer // 2
        odd = layer + 1
        u, wg_d, wu_d, w_out_b = _in_proj(xt, row(mix_norm[layer]), w_in_b,
                                           [(dense_wg, j), (dense_wu, j), (w_out, layer)])
        y, moe_wg_b, wd_d = _mixer_core(
            u, bsz, seq, conv_w[layer], row(conv_b[layer]), row(conv_ln_g[layer]), row(conv_ln_b[layer]),
            lru_conv_w[layer], row(lru_conv_b[layer]), lru_wa[layer].astype(BF16), row(lru_ba[layer]),
            lru_wx[layer].astype(BF16), row(lru_bx[layer]), row(lru_lambda[layer]),
            [(fold(moe_wg), j), (dense_wd, j)])
        xt, w_in_odd = _out_proj(xt, y, w_out_b, [(w_in, odd)])
        xt, moe_wd_b = _dense_ffn(xt, row(ffn_norm[layer]), wg_d, wu_d, wd_d, [(fold(moe_wd), j)])

        hosted = [(w_out, odd)] + ([(w_in, odd + 1)] if odd + 1 < depth else [])
        u, w_out_b, *nxt = _in_proj(xt, row(mix_norm[odd]), w_in_odd, hosted)
        if nxt:
            w_in_b = nxt[0]
        y, moe_wu_b = _mixer_core(
            u, bsz, seq, conv_w[odd], row(conv_b[odd]), row(conv_ln_g[odd]), row(conv_ln_b[odd]),
            lru_conv_w[odd], row(lru_conv_b[odd]), lru_wa[odd].astype(BF16), row(lru_ba[odd]),
            lru_wx[odd].astype(BF16), row(lru_bx[odd]), row(lru_lambda[odd]), [(fold(moe_wu), j)])
        wr = jnp.zeros((d, LANES), BF16).at[:, 0:N_EXPERTS].set(w_router[j].astype(BF16))
        g = row(ffn_norm[odd])
        xt, info = _out_proj(xt, y, w_out_b, [], router=(g, wr))
        tok, dst, block_e, n_real = _routing_plan(info, t)
        y = _moe_ffn(tok, dst, block_e, n_real, xt, g, moe_wg_b.reshape(moe_wg[j].shape),
                     moe_wu_b.reshape(moe_wu[j].shape), moe_wd_b.reshape(moe_wd[j].shape))
        xt = _combine(xt, info, row(final_norm), y, odd == depth - 1)
    return xt.reshape(bsz, seq, d)
```

```python
import functools

import jax
import jax.numpy as jnp
from jax import lax
from jax.experimental import pallas as pl
from jax.experimental.pallas import tpu as pltpu

F32 = jnp.float32
BF16 = jnp.bfloat16

D_MODEL = 2048
CONV_WIDTH = 1024
LRU_WIDTH = 1024
LRU_HEADS = 8
LRU_HEAD_DIM = LRU_WIDTH // LRU_HEADS
D_IN = 2 * CONV_WIDTH + 2 * LRU_WIDTH
CONV_K = 31
LRU_CONV_K = 4
LRU_C = 8.0
N_EXPERTS = 8
TOP_K = 2
EPS = 1e-6

LANES = 128
SUBLANES = 8
BF16_ROWS = 16
MIB = 1024 * 1024

TM_PROJ = 512
TN_IN = 1024
TS_MIX = 256
CONV_HALO = 32
LRU_HALO = 8
CONV_ROWS = 64
TM_FFN = 512
TF_FFN = 1024
TF_MOE = 768
MOE_BM = 512
MOE_XG_SLOTS = 3
TM_COMBINE = 512


def _cparams(semantics, vmem_mib):
    return pltpu.CompilerParams(dimension_semantics=semantics, vmem_limit_bytes=vmem_mib * MIB)


def _sigmoid(x):
    return 0.5 * jnp.tanh(0.5 * x) + 0.5


def _rms_norm(x, g):
    ms = jnp.mean(x * x, axis=-1, keepdims=True)
    return x * lax.rsqrt(ms + EPS) * g


def _cast_specs(casts, n_steps, index_map):
    in_specs, out_specs = [], []
    for w, layer in casts:
        _, rows, cols = w.shape
        assert rows % n_steps == 0 and (rows // n_steps) % BF16_ROWS == 0, (w.shape, n_steps)
        in_specs.append(pl.BlockSpec((None, rows // n_steps, cols),
                                     lambda *idx, layer=layer: (layer, *index_map(*idx))))
        out_specs.append(pl.BlockSpec((rows // n_steps, cols), index_map))
    return in_specs, out_specs


def _cast_shapes(casts):
    return [jax.ShapeDtypeStruct(w.shape[1:], BF16) for w, _ in casts]


def _cast_args(casts):
    return [w for w, _ in casts]


def _run_casts(in_refs, out_refs):
    for src, dst in zip(in_refs, out_refs, strict=True):
        dst[...] = src[...].astype(dst.dtype)


def _in_proj_kernel(x_ref, g_ref, w_ref, *refs, n_cast):
    cast_in, o_ref, cast_out = refs[:n_cast], refs[n_cast], refs[n_cast + 1:]
    _run_casts(cast_in, cast_out)
    h = _rms_norm(x_ref[...], g_ref[...]).astype(BF16)
    for j in range(o_ref.shape[1] // TN_IN):
        cols = slice(j * TN_IN, (j + 1) * TN_IN)
        o_ref[:, cols] = jnp.dot(h, w_ref[:, cols], preferred_element_type=F32)


def _in_proj(x, g, w, casts):
    t, d = x.shape
    n = w.shape[1]
    steps = t // TM_PROJ
    cast_in, cast_out = _cast_specs(casts, steps, lambda i: (i, 0))
    return pl.pallas_call(
        functools.partial(_in_proj_kernel, n_cast=len(casts)),
        grid=(steps,),
        in_specs=[
            pl.BlockSpec((TM_PROJ, d), lambda i: (i, 0)),
            pl.BlockSpec((1, d), lambda i: (0, 0)),
            pl.BlockSpec((d, n), lambda i: (0, 0), pipeline_mode=pl.Buffered(1)),
            *cast_in,
        ],
        out_specs=[pl.BlockSpec((TM_PROJ, n), lambda i: (i, 0)), *cast_out],
        out_shape=[jax.ShapeDtypeStruct((t, n), F32), *_cast_shapes(casts)],
        compiler_params=_cparams(("parallel",), 60),
        name="in_proj",
    )(x, g, w, *_cast_args(casts))


def _causal_depthwise_conv(buf, w_ref, b_ref, out_ref, ts, halo, n_taps):
    base = halo - (n_taps - 1)
    nt = CONV_ROWS // SUBLANES
    nq = (base + n_taps - 1) // SUBLANES + 1
    row8 = lax.broadcasted_iota(jnp.int32, (SUBLANES, LANES), 0)
    for rc in range(ts // CONV_ROWS):
        t0 = rc * CONV_ROWS
        for lc in range(buf.shape[1] // LANES):
            cols = slice(lc * LANES, (lc + 1) * LANES)
            xs = [buf[t0 + SUBLANES * j:t0 + SUBLANES * (j + 1), cols] for j in range(nt + nq - 1)]
            bias = jnp.broadcast_to(b_ref[:, cols], (SUBLANES, LANES))
            out = [bias] * nt
            for p in range(SUBLANES):
                taps = [(q, SUBLANES * q + p - base) for q in range(nq)
                        if 0 <= SUBLANES * q + p - base < n_taps]
                if not taps:
                    continue
                ws = [jnp.broadcast_to(w_ref[k:k + 1, cols], (SUBLANES, LANES)) for _, k in taps]
                zs = []
                for j in range(nt + (1 if p else 0)):
                    z = ws[0] * xs[j + taps[0][0]]
                    for w, (q, _) in zip(ws[1:], taps[1:]):
                        z = z + w * xs[j + q]
                    zs.append(z)
                if p == 0:
                    out = [o + z for o, z in zip(out, zs)]
                else:
                    rs = [pltpu.roll(z, SUBLANES - p, axis=0) for z in zs]
                    keep = row8 < SUBLANES - p
                    out = [o + jnp.where(keep, rs[g], rs[g + 1]) for g, o in enumerate(out)]
            for g in range(nt):
                out_ref[t0 + SUBLANES * g:t0 + SUBLANES * (g + 1), cols] = out[g]


def _mixer_kernel(u_ref, cw_ref, cb_ref, lg_ref, lb_ref, rw_ref, rb_ref, wa_ref, ba_ref,
                  wx_ref, bx_ref, lam_ref, *refs, n_cast):
    cast_in, o_ref, cast_out = refs[:n_cast], refs[n_cast], refs[n_cast + 1:2 * n_cast + 1]
    cbuf, rbuf, abuf, bbuf, hc_ref = refs[2 * n_cast + 1:]
    s = pl.program_id(1)
    ts = TS_MIX
    _run_casts(cast_in, cast_out)

    @pl.when(s == 0)
    def _():
        cbuf[0:CONV_HALO, :] = jnp.zeros((CONV_HALO, CONV_WIDTH), F32)
        rbuf[0:LRU_HALO, :] = jnp.zeros((LRU_HALO, LRU_WIDTH), F32)
        hc_ref[...] = jnp.zeros((SUBLANES, LRU_WIDTH), F32)

    val = u_ref[:, 0:CONV_WIDTH]
    gate = u_ref[:, CONV_WIDTH:2 * CONV_WIDTH]
    cbuf[CONV_HALO:CONV_HALO + ts, :] = val * _sigmoid(gate)
    _causal_depthwise_conv(cbuf, cw_ref, cb_ref, abuf, ts, CONV_HALO, CONV_K)
    cbuf[0:CONV_HALO, :] = cbuf[ts:ts + CONV_HALO, :]
    c = abuf[...]
    mu = jnp.mean(c, axis=-1, keepdims=True)
    cc = c - mu
    var = jnp.mean(cc * cc, axis=-1, keepdims=True)
    cn = cc * lax.rsqrt(var + EPS) * lg_ref[...] + lb_ref[...]
    o_ref[:, 0:CONV_WIDTH] = (cn * _sigmoid(cn)).astype(o_ref.dtype)

    rbuf[LRU_HALO:LRU_HALO + ts, :] = u_ref[:, 2 * CONV_WIDTH:2 * CONV_WIDTH + LRU_WIDTH]
    _causal_depthwise_conv(rbuf, rw_ref, rb_ref, bbuf, ts, LRU_HALO, LRU_CONV_K)
    rbuf[0:LRU_HALO, :] = rbuf[ts:ts + LRU_HALO, :]
    for h in range(LRU_HEADS):
        cols = slice(h * LRU_HEAD_DIM, (h + 1) * LRU_HEAD_DIM)
        xh = bbuf[:, cols]
        xh_b = xh.astype(BF16)
        ga = jnp.dot(xh_b, wa_ref[h], preferred_element_type=F32) + ba_ref[:, cols]
        gx = jnp.dot(xh_b, wx_ref[h], preferred_element_type=F32) + bx_ref[:, cols]
        lam = lam_ref[:, cols]
        e = jnp.exp(-jnp.abs(lam))
        e1 = 1.0 + e
        log1p_e = jnp.where(e1 == 1.0, e, jnp.log(e1) * (e / (e1 - 1.0)))
        sp = jnp.maximum(-lam, 0.0) + log1p_e
        log_a = (-LRU_C) * _sigmoid(ga) * sp
        a = jnp.exp(log_a)
        mult = jnp.sqrt(-jnp.tanh(log_a) * (1.0 + a * a))
        row = lax.broadcasted_iota(jnp.int32, (ts, LRU_HEAD_DIM), 0)
        mult = jnp.where((row == 0) & (s == 0), 1.0, mult)
        abuf[:, cols] = a
        bbuf[:, cols] = mult * (_sigmoid(gx) * xh)

    row8 = lax.broadcasted_iota(jnp.int32, (SUBLANES, LRU_WIDTH), 0)
    h_prev = hc_ref[...]
    for g in range(ts // SUBLANES):
        rows = slice(g * SUBLANES, (g + 1) * SUBLANES)
        a = abuf[rows, :]
        b = bbuf[rows, :]
        for dsh in (1, 2, 4):
            a_s = pltpu.roll(a, dsh, axis=0)
            b_s = pltpu.roll(b, dsh, axis=0)
            m = row8 >= dsh
            b = jnp.where(m, a * b_s + b, b)
            a = jnp.where(m, a * a_s, a)
        hg = a * h_prev + b
        abuf[rows, :] = hg
        h_prev = jnp.broadcast_to(hg[SUBLANES - 1:SUBLANES, :], (SUBLANES, LRU_WIDTH))
    hc_ref[...] = h_prev

    rg = u_ref[:, 2 * CONV_WIDTH + LRU_WIDTH:D_IN]
    gelu = 0.5 * rg * (1.0 + jnp.tanh(0.7978845608028654 * (rg + 0.044715 * (rg * rg * rg))))
    o_ref[:, CONV_WIDTH:CONV_WIDTH + LRU_WIDTH] = (abuf[...] * gelu).astype(o_ref.dtype)


def _mixer_core(u, bsz, seq, cw, cb, lg, lb, rw, rb, wa, ba, wx, bx, lam, casts):
    t = u.shape[0]
    nsb = seq // TS_MIX
    row = lambda n: pl.BlockSpec((1, n), lambda b, s: (0, 0))
    cast_in, cast_out = _cast_specs(casts, bsz * nsb, lambda b, s: (b * nsb + s, 0))
    return pl.pallas_call(
        functools.partial(_mixer_kernel, n_cast=len(casts)),
        grid=(bsz, nsb),
        in_specs=[
            pl.BlockSpec((TS_MIX, D_IN), lambda b, s: (b * nsb + s, 0)),
            pl.BlockSpec((CONV_K, CONV_WIDTH), lambda b, s: (0, 0)),
            row(CONV_WIDTH), row(CONV_WIDTH), row(CONV_WIDTH),
            pl.BlockSpec((LRU_CONV_K, LRU_WIDTH), lambda b, s: (0, 0)),
            row(LRU_WIDTH),
            pl.BlockSpec((LRU_HEADS, LRU_HEAD_DIM, LRU_HEAD_DIM), lambda b, s: (0, 0, 0)),
            row(LRU_WIDTH),
            pl.BlockSpec((LRU_HEADS, LRU_HEAD_DIM, LRU_HEAD_DIM), lambda b, s: (0, 0, 0)),
            row(LRU_WIDTH), row(LRU_WIDTH),
            *cast_in,
        ],
        out_specs=[pl.BlockSpec((TS_MIX, CONV_WIDTH + LRU_WIDTH), lambda b, s: (b * nsb + s, 0)), *cast_out],
        out_shape=[jax.ShapeDtypeStruct((t, CONV_WIDTH + LRU_WIDTH), BF16), *_cast_shapes(casts)],
        scratch_shapes=[
            pltpu.VMEM((CONV_HALO + TS_MIX, CONV_WIDTH), F32),
            pltpu.VMEM((LRU_HALO + TS_MIX, LRU_WIDTH), F32),
            pltpu.VMEM((TS_MIX, LRU_WIDTH), F32),
            pltpu.VMEM((TS_MIX, LRU_WIDTH), F32),
            pltpu.VMEM((SUBLANES, LRU_WIDTH), F32),
        ],
        compiler_params=_cparams(("parallel", "arbitrary"), 56),
        name="mixer_core",
    )(u, cw, cb, lg, lb, rw, rb, wa, ba, wx, bx, lam, *_cast_args(casts))


def _top2_info(logits, width):
    lane = lax.broadcasted_iota(jnp.int32, logits.shape, 1)
    neg = jnp.float32(-jnp.inf)
    l1 = jnp.where(lane < N_EXPERTS, logits, neg)
    m1 = jnp.max(l1, axis=-1, keepdims=True)
    i1 = jnp.min(jnp.where(l1 == m1, lane, LANES), axis=-1, keepdims=True)
    l2 = jnp.where(lane == i1, neg, l1)
    m2 = jnp.max(l2, axis=-1, keepdims=True)
    i2 = jnp.min(jnp.where(l2 == m2, lane, LANES), axis=-1, keepdims=True)
    dlt = jnp.exp(m2 - m1)
    w1 = 1.0 / (1.0 + dlt)
    w2 = dlt / (1.0 + dlt)
    out = jnp.where(lane == 0, i1.astype(F32),
                    jnp.where(lane == 1, i2.astype(F32),
                              jnp.where(lane == 2, w1, jnp.where(lane == 3, w2, 0.0))))
    return out[:, 0:width]


def _out_proj_kernel(x_ref, y_ref, w_ref, *refs, n_cast, route):
    refs = list(refs)
    if route:
        gf_ref, wr_ref = refs[:2]
        refs = refs[2:]
    cast_in, refs = refs[:n_cast], refs[n_cast:]
    o_ref, refs = refs[0], refs[1:]
    if route:
        info_ref, refs = refs[0], refs[1:]
    cast_out = refs
    _run_casts(cast_in, cast_out)
    xn = x_ref[...] + jnp.dot(y_ref[...], w_ref[...], preferred_element_type=F32)
    o_ref[...] = xn
    if route:
        h = _rms_norm(xn, gf_ref[...]).astype(BF16)
        logits = jnp.dot(h, wr_ref[...], preferred_element_type=F32)
        info_ref[...] = _top2_info(logits, info_ref.shape[1])


def _out_proj(x, y, w, casts, router=None):
    t, d = x.shape
    k = y.shape[1]
    steps = t // TM_PROJ
    cast_in, cast_out = _cast_specs(casts, steps, lambda i: (i, 0))
    const = lambda shape: pl.BlockSpec(shape, lambda i: (0, 0))
    route_in = [] if router is None else [const((1, d)), const((d, LANES))]
    route_out = [] if router is None else [pl.BlockSpec((TM_PROJ, SUBLANES), lambda i: (i, 0))]
    route_shape = [] if router is None else [jax.ShapeDtypeStruct((t, SUBLANES), F32)]
    return pl.pallas_call(
        functools.partial(_out_proj_kernel, n_cast=len(casts), route=router is not None),
        grid=(steps,),
        in_specs=[
            pl.BlockSpec((TM_PROJ, d), lambda i: (i, 0)),
            pl.BlockSpec((TM_PROJ, k), lambda i: (i, 0)),
            const((k, d)),
            *route_in,
            *cast_in,
        ],
        out_specs=[pl.BlockSpec((TM_PROJ, d), lambda i: (i, 0)), *route_out, *cast_out],
        out_shape=[jax.ShapeDtypeStruct((t, d), F32), *route_shape, *_cast_shapes(casts)],
        compiler_params=_cparams(("parallel",), 56),
        name="out_proj",
    )(x, y, w, *(router or ()), *_cast_args(casts))


def _swiglu_step(h, wg, wu, wd):
    g = jnp.dot(h, wg, preferred_element_type=F32)
    v = jnp.dot(h, wu, preferred_element_type=F32)
    act = (g * _sigmoid(g) * v).astype(BF16)
    return jnp.dot(act, wd, preferred_element_type=F32)


def _dense_ffn_kernel(x_ref, g_ref, wg_ref, wu_ref, wd_ref, *refs, n_cast):
    cast_in, o_ref, cast_out, h_ref = refs[:n_cast], refs[n_cast], refs[n_cast + 1:2 * n_cast + 1], refs[-1]
    _run_casts(cast_in, cast_out)

    @pl.when(pl.program_id(1) == 0)
    def _():
        h_ref[...] = _rms_norm(x_ref[...], g_ref[...]).astype(BF16)
        o_ref[...] = x_ref[...]

    o_ref[...] += _swiglu_step(h_ref[...], wg_ref[...], wu_ref[...], wd_ref[...])


def _dense_ffn(x, g, wg, wu, wd, casts):
    t, d = x.shape
    ff = wg.shape[1]
    nf = ff // TF_FFN
    cast_in, cast_out = _cast_specs(casts, t // TM_FFN * nf, lambda i, f: (i * nf + f, 0))
    return pl.pallas_call(
        functools.partial(_dense_ffn_kernel, n_cast=len(casts)),
        grid=(t // TM_FFN, nf),
        in_specs=[
            pl.BlockSpec((TM_FFN, d), lambda i, f: (i, 0)),
            pl.BlockSpec((1, d), lambda i, f: (0, 0)),
            pl.BlockSpec((d, TF_FFN), lambda i, f: (0, f)),
            pl.BlockSpec((d, TF_FFN), lambda i, f: (0, f)),
            pl.BlockSpec((TF_FFN, d), lambda i, f: (f, 0)),
            *cast_in,
        ],
        out_specs=[pl.BlockSpec((TM_FFN, d), lambda i, f: (i, 0)), *cast_out],
        out_shape=[jax.ShapeDtypeStruct((t, d), F32), *_cast_shapes(casts)],
        scratch_shapes=[pltpu.VMEM((TM_FFN, d), BF16)],
        compiler_params=_cparams(("parallel", "arbitrary"), 60),
        name="dense_ffn",
    )(x, g, wg, wu, wd, *_cast_args(casts))


def _moe_ffn_kernel(tok_ref, dst_ref, be_ref, nr_ref, x_hbm, g_ref, wg_ref, wu_ref, wd_ref, y_hbm,
                    xg_ref, h_ref, acc_ref, gsem, ssem, *, n_tok):
    b = pl.program_id(0)
    f = pl.program_id(1)
    nb = pl.num_programs(0)
    nf = pl.num_programs(1)
    n_real = nr_ref[0]
    real = b < n_real
    slot = b % 2
    gslot = b % MOE_XG_SLOTS
    rows_per_step = MOE_BM // nf

    def gather_start(tok, r, slt):
        pltpu.make_async_copy(x_hbm.at[pl.ds(tok, 1), :], xg_ref.at[slt, pl.ds(r, 1), :],
                              gsem.at[slt]).start()

    def scatter_start(a, r, slt):
        pltpu.make_async_copy(acc_ref.at[slt, pl.ds(r, 1), :], y_hbm.at[pl.ds(a, 1), :], ssem).start()

    def gather_wait(slt):
        pltpu.make_async_copy(x_hbm.at[pl.ds(0, MOE_BM), :], xg_ref.at[slt], gsem.at[slt]).wait()

    def scatter_wait():
        pltpu.make_async_copy(acc_ref.at[0], y_hbm.at[pl.ds(0, MOE_BM), :], ssem).wait()

    def drain(blk):
        gather_wait((blk + 1) % MOE_XG_SLOTS)
        gather_wait((blk + 2) % MOE_XG_SLOTS)
        scatter_wait()

        def body(r, carry):
            scatter_start(dst_ref[blk * MOE_BM + r], r, blk % 2)
            return carry

        lax.fori_loop(0, MOE_BM, body, 0, unroll=8)
        scatter_wait()

    @pl.when(real & (f == 0))
    def _():
        @pl.when(b == 0)
        def _():
            def body(r, carry):
                gather_start(tok_ref[r], r, 0)
                gather_start(tok_ref[MOE_BM + r], r, 1)
                return carry

            lax.fori_loop(0, MOE_BM, body, 0, unroll=8)
            acc_ref[1] = jnp.zeros(acc_ref.shape[1:], acc_ref.dtype)

        gather_wait(gslot)
        h_ref[...] = _rms_norm(xg_ref[gslot], g_ref[...]).astype(BF16)

        @pl.when(b > 0)
        def _():
            scatter_wait()

        acc_ref[slot] = jnp.zeros(acc_ref.shape[1:], acc_ref.dtype)

    @pl.when(real)
    def _():
        prev_base = jnp.maximum(b - 1, 0) * MOE_BM
        for i in range(rows_per_step):
            r = f * rows_per_step + i
            gather_start(tok_ref[(b + 2) * MOE_BM + r], r, (b + 2) % MOE_XG_SLOTS)
            a_prev = jnp.where(b == 0, TOP_K * n_tok + r, dst_ref[prev_base + r])
            scatter_start(a_prev, r, 1 - slot)
        acc_ref[slot] += _swiglu_step(h_ref[...], wg_ref[0], wu_ref[0], wd_ref[0])

    @pl.when((b == n_real) & (f == 0))
    def _():
        drain(b - 1)

    @pl.when(real & (b == nb - 1) & (f == nf - 1))
    def _():
        drain(b)


def _moe_ffn(tok, dst, block_e, n_real, x, g, wg, wu, wd):
    t, d = x.shape
    ff = wg.shape[2]
    nf = ff // TF_MOE
    n_blocks = block_e.shape[0]
    assert MOE_BM % nf == 0
    assert tok.shape[0] == (n_blocks + MOE_XG_SLOTS - 1) * MOE_BM == dst.shape[0]

    def f_eff(b, f, nr):
        return jnp.where(b < nr[0], f, nf - 1)

    grid_spec = pltpu.PrefetchScalarGridSpec(
        num_scalar_prefetch=4,
        grid=(n_blocks, nf),
        in_specs=[
            pl.BlockSpec(memory_space=pl.ANY),
            pl.BlockSpec((1, d), lambda b, f, tok, dst, be, nr: (0, 0)),
            pl.BlockSpec((1, d, TF_MOE), lambda b, f, tok, dst, be, nr: (be[b], 0, f_eff(b, f, nr))),
            pl.BlockSpec((1, d, TF_MOE), lambda b, f, tok, dst, be, nr: (be[b], 0, f_eff(b, f, nr))),
            pl.BlockSpec((1, TF_MOE, d), lambda b, f, tok, dst, be, nr: (be[b], f_eff(b, f, nr), 0)),
        ],
        out_specs=pl.BlockSpec(memory_space=pl.ANY),
        scratch_shapes=[
            pltpu.VMEM((MOE_XG_SLOTS, MOE_BM, d), F32),
            pltpu.VMEM((MOE_BM, d), BF16),
            pltpu.VMEM((2, MOE_BM, d), F32),
            pltpu.SemaphoreType.DMA((MOE_XG_SLOTS,)),
            pltpu.SemaphoreType.DMA(()),
        ],
    )
    return pl.pallas_call(
        functools.partial(_moe_ffn_kernel, n_tok=t),
        grid_spec=grid_spec,
        out_shape=jax.ShapeDtypeStruct((TOP_K * t + MOE_BM, d), F32),
        compiler_params=_cparams(("arbitrary", "arbitrary"), 56),
        name="moe_ffn",
    )(tok, dst, block_e, n_real, x, g, wg, wu, wd)


def _combine_kernel(x_ref, info_ref, g_ref, ya_ref, yb_ref, o_ref, *, final_norm):
    w1 = info_ref[:, 2:3]
    w2 = info_ref[:, 3:4]
    z = x_ref[...] + (ya_ref[...] * w1 + yb_ref[...] * w2)
    o_ref[...] = _rms_norm(z, g_ref[...]) if final_norm else z


def _combine(x, info, g, y, final_norm):
    t, d = x.shape
    nt = t // TM_COMBINE
    return pl.pallas_call(
        functools.partial(_combine_kernel, final_norm=final_norm),
        grid=(nt,),
        in_specs=[
            pl.BlockSpec((TM_COMBINE, d), lambda i: (i, 0)),
            pl.BlockSpec((TM_COMBINE, SUBLANES), lambda i: (i, 0)),
            pl.BlockSpec((1, d), lambda i: (0, 0)),
            pl.BlockSpec((TM_COMBINE, d), lambda i: (i, 0)),
            pl.BlockSpec((TM_COMBINE, d), lambda i: (nt + i, 0)),
        ],
        out_specs=pl.BlockSpec((TM_COMBINE, d), lambda i: (i, 0)),
        out_shape=jax.ShapeDtypeStruct((t, d), F32),
        compiler_params=_cparams(("parallel",), 48),
        name="moe_combine",
    )(x, info, g, y, y)


def _routing_plan(info, n_tok):
    n_asg = n_tok * TOP_K
    n_blocks = n_asg // MOE_BM + N_EXPERTS
    flat_e = info[:, 0:TOP_K].astype(jnp.int32).reshape(n_asg)
    onehot = (flat_e[None, :] == jnp.arange(N_EXPERTS, dtype=jnp.int32)[:, None]).astype(jnp.int32)
    csum = jnp.cumsum(onehot, axis=1)
    counts = csum[:, -1]
    padded = ((counts + MOE_BM - 1) // MOE_BM) * MOE_BM
    p_ends = jnp.cumsum(padded)
    p_starts = p_ends - padded
    pos = jnp.sum(onehot * (csum - 1 + p_starts[:, None]), axis=0).astype(jnp.int32)
    flat = jnp.arange(n_asg, dtype=jnp.int32)
    dest = (flat % TOP_K) * n_tok + flat // TOP_K
    spare = n_asg + jnp.arange((n_blocks + MOE_XG_SLOTS - 1) * MOE_BM, dtype=jnp.int32) % MOE_BM
    dst = spare.at[pos].set(dest, unique_indices=True, mode="promise_in_bounds")
    tok = jnp.where(dst >= n_asg, 0, dst % n_tok)
    block_start = jnp.arange(n_blocks, dtype=jnp.int32) * MOE_BM
    block_e = jnp.minimum(jnp.searchsorted(p_ends, block_start, side="right"), N_EXPERTS - 1).astype(jnp.int32)
    n_real = (p_ends[-1] // MOE_BM).astype(jnp.int32).reshape(1)
    return tok, dst, block_e, n_real


def kernel(x, mix_norm, w_in, conv_w, conv_b, conv_ln_g, conv_ln_b, lru_conv_w, lru_conv_b, lru_wa, lru_ba, lru_wx, lru_bx, lru_lambda, w_out, ffn_norm, dense_wg, dense_wu, dense_wd, w_router, moe_wg, moe_wu, moe_wd, final_norm):
    bsz, seq, d = x.shape
    depth = w_in.shape[0]
    t = bsz * seq
    xt = x.reshape(t, d)
    row = lambda v: v.reshape(1, -1)
    assert depth % 2 == 0, "the final RMSNorm is fused into the last routed layer's combine"
    fold = lambda w: w.reshape(w.shape[0], -1, w.shape[-1])
    w_in_b = w_in[0].astype(BF16)
    for layer in range(0, depth, 2):
        j = layer // 2
        odd = layer + 1
        u, wg_d, wu_d, w_out_b = _in_proj(xt, row(mix_norm[layer]), w_in_b,
                                           [(dense_wg, j), (dense_wu, j), (w_out, layer)])
        y, moe_wg_b, wd_d = _mixer_core(
            u, bsz, seq, conv_w[layer], row(conv_b[layer]), row(conv_ln_g[layer]), row(conv_ln_b[layer]),
            lru_conv_w[layer], row(lru_conv_b[layer]), lru_wa[layer].astype(BF16), row(lru_ba[layer]),
            lru_wx[layer].astype(BF16), row(lru_bx[layer]), row(lru_lambda[layer]),
            [(fold(moe_wg), j), (dense_wd, j)])
        xt, w_in_odd = _out_proj(xt, y, w_out_b, [(w_in, odd)])
        xt, moe_wd_b = _dense_ffn(xt, row(ffn_norm[layer]), wg_d, wu_d, wd_d, [(fold(moe_wd), j)])

        hosted = [(w_out, odd)] + ([(w_in, odd + 1)] if odd + 1 < depth else [])
        u, w_out_b, *nxt = _in_proj(xt, row(mix_norm[odd]), w_in_odd, hosted)
        if nxt:
            w_in_b = nxt[0]
        y, moe_wu_b = _mixer_core(
            u, bsz, seq, conv_w[odd], row(conv_b[odd]), row(conv_ln_g[odd]), row(conv_ln_b[odd]),
            lru_conv_w[odd], row(lru_conv_b[odd]), lru_wa[odd].astype(BF16), row(lru_ba[odd]),
            lru_wx[odd].astype(BF16), row(lru_bx[odd]), row(lru_lambda[odd]), [(fold(moe_wu), j)])
        wr = jnp.zeros((d, LANES), BF16).at[:, 0:N_EXPERTS].set(w_router[j].astype(BF16))
        g = row(ffn_norm[odd])
        xt, info = _out_proj(xt, y, w_out_b, [], router=(g, wr))
        tok, dst, block_e, n_real = _routing_plan(info, t)
        y = _moe_ffn(tok, dst, block_e, n_real, xt, g, moe_wg_b.reshape(moe_wg[j].shape),
                     moe_wu_b.reshape(moe_wu[j].shape), moe_wd_b.reshape(moe_wd[j].shape))
        xt = _combine(xt, info, row(final_norm), y, odd == depth - 1)
    return xt.reshape(bsz, seq, d)
```

```python
import functools

import jax
import jax.numpy as jnp
from jax import lax
from jax.experimental import pallas as pl
from jax.experimental.pallas import tpu as pltpu

F32 = jnp.float32
BF16 = jnp.bfloat16

D_MODEL = 2048
CONV_WIDTH = 1024
LRU_WIDTH = 1024
LRU_HEADS = 8
LRU_HEAD_DIM = LRU_WIDTH // LRU_HEADS
D_IN = 2 * CONV_WIDTH + 2 * LRU_WIDTH
CONV_K = 31
LRU_CONV_K = 4
LRU_C = 8.0
N_EXPERTS = 8
TOP_K = 2
EPS = 1e-6

LANES = 128
SUBLANES = 8
BF16_ROWS = 16
MIB = 1024 * 1024

TM_PROJ = 512
TN_IN = 1024
TS_MIX = 256
CONV_HALO = 32
LRU_HALO = 8
CONV_ROWS = 64
TM_FFN = 512
TF_FFN = 1024
TF_MOE = 768
MOE_BM = 512
MOE_XG_SLOTS = 3
TM_COMBINE = 512


def _cparams(semantics, vmem_mib):
    return pltpu.CompilerParams(dimension_semantics=semantics, vmem_limit_bytes=vmem_mib * MIB)


def _sigmoid(x):
    return 0.5 * jnp.tanh(0.5 * x) + 0.5


def _rms_norm(x, g):
    ms = jnp.mean(x * x, axis=-1, keepdims=True)
    return x * lax.rsqrt(ms + EPS) * g


def _cast_specs(casts, n_steps, index_map):
    in_specs, out_specs = [], []
    for w, layer, *tiling in casts:
        _, rows, cols = w.shape
        rb = rows // n_steps
        assert rows % n_steps == 0 and rb % BF16_ROWS == 0, (w.shape, n_steps)
        in_specs.append(pl.BlockSpec((None, rb, cols), lambda *idx, layer=layer: (layer, *index_map(*idx))))
        if tiling:
            group_rows, tile_cols = tiling
            per_group = group_rows // rb
            assert group_rows % rb == 0 and cols % tile_cols == 0

            def tiled_map(*idx, per_group=per_group):
                i = index_map(*idx)[0]
                return (i // per_group, i % per_group, 0)

            out_specs.append(pl.BlockSpec((cols // tile_cols, rb, tile_cols), tiled_map))
        else:
            out_specs.append(pl.BlockSpec((rb, cols), index_map))
    return in_specs, out_specs


def _cast_shapes(casts):
    shapes = []
    for w, _, *tiling in casts:
        _, rows, cols = w.shape
        if tiling:
            group_rows, tile_cols = tiling
            shapes.append(jax.ShapeDtypeStruct((rows // group_rows * (cols // tile_cols), group_rows, tile_cols), BF16))
        else:
            shapes.append(jax.ShapeDtypeStruct((rows, cols), BF16))
    return shapes


def _cast_args(casts):
    return [entry[0] for entry in casts]


def _run_casts(in_refs, out_refs):
    for src, dst in zip(in_refs, out_refs, strict=True):
        if len(dst.shape) == 3:
            tc = dst.shape[2]
            for f in range(dst.shape[0]):
                dst[f] = src[:, f * tc:(f + 1) * tc].astype(dst.dtype)
        else:
            dst[...] = src[...].astype(dst.dtype)


def _in_proj_kernel(x_ref, g_ref, w_ref, *refs, n_cast):
    cast_in, o_ref, cast_out = refs[:n_cast], refs[n_cast], refs[n_cast + 1:]
    _run_casts(cast_in, cast_out)
    h = _rms_norm(x_ref[...], g_ref[...]).astype(BF16)
    for j in range(o_ref.shape[1] // TN_IN):
        cols = slice(j * TN_IN, (j + 1) * TN_IN)
        o_ref[:, cols] = jnp.dot(h, w_ref[:, cols], preferred_element_type=F32)


def _in_proj(x, g, w, casts):
    t, d = x.shape
    n = w.shape[1]
    steps = t // TM_PROJ
    cast_in, cast_out = _cast_specs(casts, steps, lambda i: (i, 0))
    return pl.pallas_call(
        functools.partial(_in_proj_kernel, n_cast=len(casts)),
        grid=(steps,),
        in_specs=[
            pl.BlockSpec((TM_PROJ, d), lambda i: (i, 0)),
            pl.BlockSpec((1, d), lambda i: (0, 0)),
            pl.BlockSpec((d, n), lambda i: (0, 0), pipeline_mode=pl.Buffered(1)),
            *cast_in,
        ],
        out_specs=[pl.BlockSpec((TM_PROJ, n), lambda i: (i, 0)), *cast_out],
        out_shape=[jax.ShapeDtypeStruct((t, n), F32), *_cast_shapes(casts)],
        compiler_params=_cparams(("parallel",), 60),
        name="in_proj",
    )(x, g, w, *_cast_args(casts))


def _causal_depthwise_conv(buf, w_ref, b_ref, out_ref, ts, halo, n_taps):
    base = halo - (n_taps - 1)
    nt = CONV_ROWS // SUBLANES
    nq = (base + n_taps - 1) // SUBLANES + 1
    row8 = lax.broadcasted_iota(jnp.int32, (SUBLANES, LANES), 0)
    for rc in range(ts // CONV_ROWS):
        t0 = rc * CONV_ROWS
        for lc in range(buf.shape[1] // LANES):
            cols = slice(lc * LANES, (lc + 1) * LANES)
            xs = [buf[t0 + SUBLANES * j:t0 + SUBLANES * (j + 1), cols] for j in range(nt + nq - 1)]
            bias = jnp.broadcast_to(b_ref[:, cols], (SUBLANES, LANES))
            out = [bias] * nt
            for p in range(SUBLANES):
                taps = [(q, SUBLANES * q + p - base) for q in range(nq)
                        if 0 <= SUBLANES * q + p - base < n_taps]
                if not taps:
                    continue
                ws = [jnp.broadcast_to(w_ref[k:k + 1, cols], (SUBLANES, LANES)) for _, k in taps]
                zs = []
                for j in range(nt + (1 if p else 0)):
                    z = ws[0] * xs[j + taps[0][0]]
                    for w, (q, _) in zip(ws[1:], taps[1:]):
                        z = z + w * xs[j + q]
                    zs.append(z)
                if p == 0:
                    out = [o + z for o, z in zip(out, zs)]
                else:
                    rs = [pltpu.roll(z, SUBLANES - p, axis=0) for z in zs]
                    keep = row8 < SUBLANES - p
                    out = [o + jnp.where(keep, rs[g], rs[g + 1]) for g, o in enumerate(out)]
            for g in range(nt):
                out_ref[t0 + SUBLANES * g:t0 + SUBLANES * (g + 1), cols] = out[g]


def _mixer_kernel(u_ref, cw_ref, cb_ref, lg_ref, lb_ref, rw_ref, rb_ref, wa_ref, ba_ref,
                  wx_ref, bx_ref, lam_ref, *refs, n_cast):
    cast_in, o_ref, cast_out = refs[:n_cast], refs[n_cast], refs[n_cast + 1:2 * n_cast + 1]
    cbuf, rbuf, abuf, bbuf, hc_ref = refs[2 * n_cast + 1:]
    s = pl.program_id(1)
    ts = TS_MIX
    _run_casts(cast_in, cast_out)

    @pl.when(s == 0)
    def _():
        cbuf[0:CONV_HALO, :] = jnp.zeros((CONV_HALO, CONV_WIDTH), F32)
        rbuf[0:LRU_HALO, :] = jnp.zeros((LRU_HALO, LRU_WIDTH), F32)
        hc_ref[...] = jnp.zeros((SUBLANES, LRU_WIDTH), F32)

    val = u_ref[:, 0:CONV_WIDTH]
    gate = u_ref[:, CONV_WIDTH:2 * CONV_WIDTH]
    cbuf[CONV_HALO:CONV_HALO + ts, :] = val * _sigmoid(gate)
    _causal_depthwise_conv(cbuf, cw_ref, cb_ref, abuf, ts, CONV_HALO, CONV_K)
    cbuf[0:CONV_HALO, :] = cbuf[ts:ts + CONV_HALO, :]
    c = abuf[...]
    mu = jnp.mean(c, axis=-1, keepdims=True)
    cc = c - mu
    var = jnp.mean(cc * cc, axis=-1, keepdims=True)
    cn = cc * lax.rsqrt(var + EPS) * lg_ref[...] + lb_ref[...]
    o_ref[:, 0:CONV_WIDTH] = (cn * _sigmoid(cn)).astype(o_ref.dtype)

    rbuf[LRU_HALO:LRU_HALO + ts, :] = u_ref[:, 2 * CONV_WIDTH:2 * CONV_WIDTH + LRU_WIDTH]
    _causal_depthwise_conv(rbuf, rw_ref, rb_ref, bbuf, ts, LRU_HALO, LRU_CONV_K)
    rbuf[0:LRU_HALO, :] = rbuf[ts:ts + LRU_HALO, :]
    for h in range(LRU_HEADS):
        cols = slice(h * LRU_HEAD_DIM, (h + 1) * LRU_HEAD_DIM)
        xh = bbuf[:, cols]
        xh_b = xh.astype(BF16)
        ga = jnp.dot(xh_b, wa_ref[h], preferred_element_type=F32) + ba_ref[:, cols]
        gx = jnp.dot(xh_b, wx_ref[h], preferred_element_type=F32) + bx_ref[:, cols]
        lam = lam_ref[:, cols]
        e = jnp.exp(-jnp.abs(lam))
        e1 = 1.0 + e
        log1p_e = jnp.where(e1 == 1.0, e, jnp.log(e1) * (e / (e1 - 1.0)))
        sp = jnp.maximum(-lam, 0.0) + log1p_e
        log_a = (-LRU_C) * _sigmoid(ga) * sp
        a = jnp.exp(log_a)
        mult = jnp.sqrt(-jnp.tanh(log_a) * (1.0 + a * a))
        row = lax.broadcasted_iota(jnp.int32, (ts, LRU_HEAD_DIM), 0)
        mult = jnp.where((row == 0) & (s == 0), 1.0, mult)
        abuf[:, cols] = a
        bbuf[:, cols] = mult * (_sigmoid(gx) * xh)

    row8 = lax.broadcasted_iota(jnp.int32, (SUBLANES, LRU_WIDTH), 0)
    h_prev = hc_ref[...]
    for g in range(ts // SUBLANES):
        rows = slice(g * SUBLANES, (g + 1) * SUBLANES)
        a = abuf[rows, :]
        b = bbuf[rows, :]
        for dsh in (1, 2, 4):
            a_s = pltpu.roll(a, dsh, axis=0)
            b_s = pltpu.roll(b, dsh, axis=0)
            m = row8 >= dsh
            b = jnp.where(m, a * b_s + b, b)
            a = jnp.where(m, a * a_s, a)
        hg = a * h_prev + b
        abuf[rows, :] = hg
        h_prev = jnp.broadcast_to(hg[SUBLANES - 1:SUBLANES, :], (SUBLANES, LRU_WIDTH))
    hc_ref[...] = h_prev

    rg = u_ref[:, 2 * CONV_WIDTH + LRU_WIDTH:D_IN]
    gelu = 0.5 * rg * (1.0 + jnp.tanh(0.7978845608028654 * (rg + 0.044715 * (rg * rg * rg))))
    o_ref[:, CONV_WIDTH:CONV_WIDTH + LRU_WIDTH] = (abuf[...] * gelu).astype(o_ref.dtype)


def _mixer_core(u, bsz, seq, cw, cb, lg, lb, rw, rb, wa, ba, wx, bx, lam, casts):
    t = u.shape[0]
    nsb = seq // TS_MIX
    row = lambda n: pl.BlockSpec((1, n), lambda b, s: (0, 0))
    cast_in, cast_out = _cast_specs(casts, bsz * nsb, lambda b, s: (b * nsb + s, 0))
    return pl.pallas_call(
        functools.partial(_mixer_kernel, n_cast=len(casts)),
        grid=(bsz, nsb),
        in_specs=[
            pl.BlockSpec((TS_MIX, D_IN), lambda b, s: (b * nsb + s, 0)),
            pl.BlockSpec((CONV_K, CONV_WIDTH), lambda b, s: (0, 0)),
            row(CONV_WIDTH), row(CONV_WIDTH), row(CONV_WIDTH),
            pl.BlockSpec((LRU_CONV_K, LRU_WIDTH), lambda b, s: (0, 0)),
            row(LRU_WIDTH),
            pl.BlockSpec((LRU_HEADS, LRU_HEAD_DIM, LRU_HEAD_DIM), lambda b, s: (0, 0, 0)),
            row(LRU_WIDTH),
            pl.BlockSpec((LRU_HEADS, LRU_HEAD_DIM, LRU_HEAD_DIM), lambda b, s: (0, 0, 0)),
            row(LRU_WIDTH), row(LRU_WIDTH),
            *cast_in,
        ],
        out_specs=[pl.BlockSpec((TS_MIX, CONV_WIDTH + LRU_WIDTH), lambda b, s: (b * nsb + s, 0)), *cast_out],
        out_shape=[jax.ShapeDtypeStruct((t, CONV_WIDTH + LRU_WIDTH), BF16), *_cast_shapes(casts)],
        scratch_shapes=[
            pltpu.VMEM((CONV_HALO + TS_MIX, CONV_WIDTH), F32),
            pltpu.VMEM((LRU_HALO + TS_MIX, LRU_WIDTH), F32),
            pltpu.VMEM((TS_MIX, LRU_WIDTH), F32),
            pltpu.VMEM((TS_MIX, LRU_WIDTH), F32),
            pltpu.VMEM((SUBLANES, LRU_WIDTH), F32),
        ],
        compiler_params=_cparams(("parallel", "arbitrary"), 56),
        name="mixer_core",
    )(u, cw, cb, lg, lb, rw, rb, wa, ba, wx, bx, lam, *_cast_args(casts))


def _top2_info(logits, width):
    lane = lax.broadcasted_iota(jnp.int32, logits.shape, 1)
    neg = jnp.float32(-jnp.inf)
    l1 = jnp.where(lane < N_EXPERTS, logits, neg)
    m1 = jnp.max(l1, axis=-1, keepdims=True)
    i1 = jnp.min(jnp.where(l1 == m1, lane, LANES), axis=-1, keepdims=True)
    l2 = jnp.where(lane == i1, neg, l1)
    m2 = jnp.max(l2, axis=-1, keepdims=True)
    i2 = jnp.min(jnp.where(l2 == m2, lane, LANES), axis=-1, keepdims=True)
    dlt = jnp.exp(m2 - m1)
    w1 = 1.0 / (1.0 + dlt)
    w2 = dlt / (1.0 + dlt)
    out = jnp.where(lane == 0, i1.astype(F32),
                    jnp.where(lane == 1, i2.astype(F32),
                              jnp.where(lane == 2, w1, jnp.where(lane == 3, w2, 0.0))))
    return out[:, 0:width]


def _out_proj_kernel(x_ref, y_ref, w_ref, *refs, n_cast, route):
    refs = list(refs)
    if route:
        gf_ref, wr_ref = refs[:2]
        refs = refs[2:]
    cast_in, refs = refs[:n_cast], refs[n_cast:]
    o_ref, refs = refs[0], refs[1:]
    if route:
        info_ref, refs = refs[0], refs[1:]
    cast_out = refs
    _run_casts(cast_in, cast_out)
    xn = x_ref[...] + jnp.dot(y_ref[...], w_ref[...], preferred_element_type=F32)
    o_ref[...] = xn
    if route:
        h = _rms_norm(xn, gf_ref[...]).astype(BF16)
        logits = jnp.dot(h, wr_ref[...], preferred_element_type=F32)
        info_ref[...] = _top2_info(logits, info_ref.shape[1])


def _out_proj(x, y, w, casts, router=None):
    t, d = x.shape
    k = y.shape[1]
    steps = t // TM_PROJ
    cast_in, cast_out = _cast_specs(casts, steps, lambda i: (i, 0))
    const = lambda shape: pl.BlockSpec(shape, lambda i: (0, 0))
    route_in = [] if router is None else [const((1, d)), const((d, LANES))]
    route_out = [] if router is None else [pl.BlockSpec((TM_PROJ, SUBLANES), lambda i: (i, 0))]
    route_shape = [] if router is None else [jax.ShapeDtypeStruct((t, SUBLANES), F32)]
    return pl.pallas_call(
        functools.partial(_out_proj_kernel, n_cast=len(casts), route=router is not None),
        grid=(steps,),
        in_specs=[
            pl.BlockSpec((TM_PROJ, d), lambda i: (i, 0)),
            pl.BlockSpec((TM_PROJ, k), lambda i: (i, 0)),
            const((k, d)),
            *route_in,
            *cast_in,
        ],
        out_specs=[pl.BlockSpec((TM_PROJ, d), lambda i: (i, 0)), *route_out, *cast_out],
        out_shape=[jax.ShapeDtypeStruct((t, d), F32), *route_shape, *_cast_shapes(casts)],
        compiler_params=_cparams(("parallel",), 56),
        name="out_proj",
    )(x, y, w, *(router or ()), *_cast_args(casts))


def _swiglu_step(h, wg, wu, wd):
    g = jnp.dot(h, wg, preferred_element_type=F32)
    v = jnp.dot(h, wu, preferred_element_type=F32)
    act = (g * _sigmoid(g) * v).astype(BF16)
    return jnp.dot(act, wd, preferred_element_type=F32)


def _dense_ffn_kernel(x_ref, g_ref, wg_ref, wu_ref, wd_ref, *refs, n_cast):
    cast_in, o_ref, cast_out, h_ref = refs[:n_cast], refs[n_cast], refs[n_cast + 1:2 * n_cast + 1], refs[-1]
    _run_casts(cast_in, cast_out)

    @pl.when(pl.program_id(1) == 0)
    def _():
        h_ref[...] = _rms_norm(x_ref[...], g_ref[...]).astype(BF16)
        o_ref[...] = x_ref[...]

    o_ref[...] += _swiglu_step(h_ref[...], wg_ref[...], wu_ref[...], wd_ref[...])


def _dense_ffn(x, g, wg, wu, wd, casts):
    t, d = x.shape
    ff = wg.shape[1]
    nf = ff // TF_FFN
    cast_in, cast_out = _cast_specs(casts, t // TM_FFN * nf, lambda i, f: (i * nf + f, 0))
    return pl.pallas_call(
        functools.partial(_dense_ffn_kernel, n_cast=len(casts)),
        grid=(t // TM_FFN, nf),
        in_specs=[
            pl.BlockSpec((TM_FFN, d), lambda i, f: (i, 0)),
            pl.BlockSpec((1, d), lambda i, f: (0, 0)),
            pl.BlockSpec((d, TF_FFN), lambda i, f: (0, f)),
            pl.BlockSpec((d, TF_FFN), lambda i, f: (0, f)),
            pl.BlockSpec((TF_FFN, d), lambda i, f: (f, 0)),
            *cast_in,
        ],
        out_specs=[pl.BlockSpec((TM_FFN, d), lambda i, f: (i, 0)), *cast_out],
        out_shape=[jax.ShapeDtypeStruct((t, d), F32), *_cast_shapes(casts)],
        scratch_shapes=[pltpu.VMEM((TM_FFN, d), BF16)],
        compiler_params=_cparams(("parallel", "arbitrary"), 60),
        name="dense_ffn",
    )(x, g, wg, wu, wd, *_cast_args(casts))


def _moe_ffn_kernel(tok_ref, dst_ref, be_ref, nr_ref, x_hbm, g_ref, wg_ref, wu_ref, wd_ref, y_hbm,
                    xg_ref, h_ref, acc_ref, gsem, ssem, *, n_tok):
    b = pl.program_id(0)
    f = pl.program_id(1)
    nb = pl.num_programs(0)
    nf = pl.num_programs(1)
    n_real = nr_ref[0]
    real = b < n_real
    slot = b % 2
    gslot = b % MOE_XG_SLOTS
    rows_per_step = MOE_BM // nf

    def gather_start(tok, r, slt):
        pltpu.make_async_copy(x_hbm.at[pl.ds(tok, 1), :], xg_ref.at[slt, pl.ds(r, 1), :],
                              gsem.at[slt]).start()

    def scatter_start(a, r, slt):
        pltpu.make_async_copy(acc_ref.at[slt, pl.ds(r, 1), :], y_hbm.at[pl.ds(a, 1), :], ssem).start()

    def gather_wait(slt):
        pltpu.make_async_copy(x_hbm.at[pl.ds(0, MOE_BM), :], xg_ref.at[slt], gsem.at[slt]).wait()

    def scatter_wait():
        pltpu.make_async_copy(acc_ref.at[0], y_hbm.at[pl.ds(0, MOE_BM), :], ssem).wait()

    def drain(blk):
        gather_wait((blk + 1) % MOE_XG_SLOTS)
        gather_wait((blk + 2) % MOE_XG_SLOTS)
        scatter_wait()

        def body(r, carry):
            scatter_start(dst_ref[blk * MOE_BM + r], r, blk % 2)
            return carry

        lax.fori_loop(0, MOE_BM, body, 0, unroll=8)
        scatter_wait()

    @pl.when(real & (f == 0))
    def _():
        @pl.when(b == 0)
        def _():
            def body(r, carry):
                gather_start(tok_ref[r], r, 0)
                gather_start(tok_ref[MOE_BM + r], r, 1)
                return carry

            lax.fori_loop(0, MOE_BM, body, 0, unroll=8)
            acc_ref[1] = jnp.zeros(acc_ref.shape[1:], acc_ref.dtype)

        gather_wait(gslot)
        h_ref[...] = _rms_norm(xg_ref[gslot], g_ref[...]).astype(BF16)

        @pl.when(b > 0)
        def _():
            scatter_wait()

        acc_ref[slot] = jnp.zeros(acc_ref.shape[1:], acc_ref.dtype)

    @pl.when(real)
    def _():
        prev_base = jnp.maximum(b - 1, 0) * MOE_BM
        for i in range(rows_per_step):
            r = f * rows_per_step + i
            gather_start(tok_ref[(b + 2) * MOE_BM + r], r, (b + 2) % MOE_XG_SLOTS)
            a_prev = jnp.where(b == 0, TOP_K * n_tok + r, dst_ref[prev_base + r])
            scatter_start(a_prev, r, 1 - slot)
        acc_ref[slot] += _swiglu_step(h_ref[...], wg_ref[...], wu_ref[...], wd_ref[0])

    @pl.when((b == n_real) & (f == 0))
    def _():
        drain(b - 1)

    @pl.when(real & (b == nb - 1) & (f == nf - 1))
    def _():
        drain(b)


def _moe_ffn(tok, dst, block_e, n_real, x, g, wg, wu, wd):
    t, d = x.shape
    ff = wd.shape[1]
    nf = ff // TF_MOE
    n_blocks = block_e.shape[0]
    assert wg.shape == wu.shape == (N_EXPERTS * nf, d, TF_MOE)
    assert MOE_BM % nf == 0
    assert tok.shape[0] == (n_blocks + MOE_XG_SLOTS - 1) * MOE_BM == dst.shape[0]

    def f_eff(b, f, nr):
        return jnp.where(b < nr[0], f, nf - 1)

    grid_spec = pltpu.PrefetchScalarGridSpec(
        num_scalar_prefetch=4,
        grid=(n_blocks, nf),
        in_specs=[
            pl.BlockSpec(memory_space=pl.ANY),
            pl.BlockSpec((1, d), lambda b, f, tok, dst, be, nr: (0, 0)),
            pl.BlockSpec((None, d, TF_MOE), lambda b, f, tok, dst, be, nr: (be[b] * nf + f_eff(b, f, nr), 0, 0)),
            pl.BlockSpec((None, d, TF_MOE), lambda b, f, tok, dst, be, nr: (be[b] * nf + f_eff(b, f, nr), 0, 0)),
            pl.BlockSpec((1, TF_MOE, d), lambda b, f, tok, dst, be, nr: (be[b], f_eff(b, f, nr), 0)),
        ],
        out_specs=pl.BlockSpec(memory_space=pl.ANY),
        scratch_shapes=[
            pltpu.VMEM((MOE_XG_SLOTS, MOE_BM, d), F32),
            pltpu.VMEM((MOE_BM, d), BF16),
            pltpu.VMEM((2, MOE_BM, d), F32),
            pltpu.SemaphoreType.DMA((MOE_XG_SLOTS,)),
            pltpu.SemaphoreType.DMA(()),
        ],
    )
    return pl.pallas_call(
        functools.partial(_moe_ffn_kernel, n_tok=t),
        grid_spec=grid_spec,
        out_shape=jax.ShapeDtypeStruct((TOP_K * t + MOE_BM, d), F32),
        compiler_params=_cparams(("arbitrary", "arbitrary"), 56),
        name="moe_ffn",
    )(tok, dst, block_e, n_real, x, g, wg, wu, wd)


def _combine_kernel(x_ref, info_ref, g_ref, ya_ref, yb_ref, o_ref, *, final_norm):
    w1 = info_ref[:, 2:3]
    w2 = info_ref[:, 3:4]
    z = x_ref[...] + (ya_ref[...] * w1 + yb_ref[...] * w2)
    o_ref[...] = _rms_norm(z, g_ref[...]) if final_norm else z


def _combine(x, info, g, y, final_norm):
    t, d = x.shape
    nt = t // TM_COMBINE
    return pl.pallas_call(
        functools.partial(_combine_kernel, final_norm=final_norm),
        grid=(nt,),
        in_specs=[
            pl.BlockSpec((TM_COMBINE, d), lambda i: (i, 0)),
            pl.BlockSpec((TM_COMBINE, SUBLANES), lambda i: (i, 0)),
            pl.BlockSpec((1, d), lambda i: (0, 0)),
            pl.BlockSpec((TM_COMBINE, d), lambda i: (i, 0)),
            pl.BlockSpec((TM_COMBINE, d), lambda i: (nt + i, 0)),
        ],
        out_specs=pl.BlockSpec((TM_COMBINE, d), lambda i: (i, 0)),
        out_shape=jax.ShapeDtypeStruct((t, d), F32),
        compiler_params=_cparams(("parallel",), 48),
        name="moe_combine",
    )(x, info, g, y, y)


def _routing_plan(info, n_tok):
    n_asg = n_tok * TOP_K
    n_blocks = n_asg // MOE_BM + N_EXPERTS
    flat_e = info[:, 0:TOP_K].astype(jnp.int32).reshape(n_asg)
    onehot = (flat_e[None, :] == jnp.arange(N_EXPERTS, dtype=jnp.int32)[:, None]).astype(jnp.int32)
    csum = jnp.cumsum(onehot, axis=1)
    counts = csum[:, -1]
    padded = ((counts + MOE_BM - 1) // MOE_BM) * MOE_BM
    p_ends = jnp.cumsum(padded)
    p_starts = p_ends - padded
    pos = jnp.sum(onehot * (csum - 1 + p_starts[:, None]), axis=0).astype(jnp.int32)
    flat = jnp.arange(n_asg, dtype=jnp.int32)
    dest = (flat % TOP_K) * n_tok + flat // TOP_K
    spare = n_asg + jnp.arange((n_blocks + MOE_XG_SLOTS - 1) * MOE_BM, dtype=jnp.int32) % MOE_BM
    dst = spare.at[pos].set(dest, unique_indices=True, mode="promise_in_bounds")
    tok = jnp.where(dst >= n_asg, 0, dst % n_tok)
    block_start = jnp.arange(n_blocks, dtype=jnp.int32) * MOE_BM
    block_e = jnp.minimum(jnp.searchsorted(p_ends, block_start, side="right"), N_EXPERTS - 1).astype(jnp.int32)
    n_real = (p_ends[-1] // MOE_BM).astype(jnp.int32).reshape(1)
    return tok, dst, block_e, n_real


def kernel(x, mix_norm, w_in, conv_w, conv_b, conv_ln_g, conv_ln_b, lru_conv_w, lru_conv_b, lru_wa, lru_ba, lru_wx, lru_bx, lru_lambda, w_out, ffn_norm, dense_wg, dense_wu, dense_wd, w_router, moe_wg, moe_wu, moe_wd, final_norm):
    bsz, seq, d = x.shape
    depth = w_in.shape[0]
    t = bsz * seq
    xt = x.reshape(t, d)
    row = lambda v: v.reshape(1, -1)
    assert depth % 2 == 0, "the final RMSNorm is fused into the last routed layer's combine"
    fold = lambda w: w.reshape(w.shape[0], -1, w.shape[-1])
    w_in_b = w_in[0].astype(BF16)
    for layer in range(0, depth, 2):
        j = layer // 2
        odd = layer + 1
        u, wg_d, wu_d, w_out_b = _in_proj(xt, row(mix_norm[layer]), w_in_b,
                                           [(dense_wg, j), (dense_wu, j), (w_out, layer)])
        y, moe_wg_b, wd_d = _mixer_core(
            u, bsz, seq, conv_w[layer], row(conv_b[layer]), row(conv_ln_g[layer]), row(conv_ln_b[layer]),
            lru_conv_w[layer], row(lru_conv_b[layer]), lru_wa[layer].astype(BF16), row(lru_ba[layer]),
            lru_wx[layer].astype(BF16), row(lru_bx[layer]), row(lru_lambda[layer]),
            [(fold(moe_wg), j, d, TF_MOE), (dense_wd, j)])
        xt, w_in_odd = _out_proj(xt, y, w_out_b, [(w_in, odd)])
        xt, moe_wd_b = _dense_ffn(xt, row(ffn_norm[layer]), wg_d, wu_d, wd_d, [(fold(moe_wd), j)])

        hosted = [(w_out, odd)] + ([(w_in, odd + 1)] if odd + 1 < depth else [])
        u, w_out_b, *nxt = _in_proj(xt, row(mix_norm[odd]), w_in_odd, hosted)
        if nxt:
            w_in_b = nxt[0]
        y, moe_wu_b = _mixer_core(
            u, bsz, seq, conv_w[odd], row(conv_b[odd]), row(conv_ln_g[odd]), row(conv_ln_b[odd]),
            lru_conv_w[odd], row(lru_conv_b[odd]), lru_wa[odd].astype(BF16), row(lru_ba[odd]),
            lru_wx[odd].astype(BF16), row(lru_bx[odd]), row(lru_lambda[odd]), [(fold(moe_wu), j, d, TF_MOE)])
        wr = jnp.zeros((d, LANES), BF16).at[:, 0:N_EXPERTS].set(w_router[j].astype(BF16))
        g = row(ffn_norm[odd])
        xt, info = _out_proj(xt, y, w_out_b, [], router=(g, wr))
        tok, dst, block_e, n_real = _routing_plan(info, t)
        y = _moe_ffn(tok, dst, block_e, n_real, xt, g, moe_wg_b, moe_wu_b, moe_wd_b.reshape(moe_wd[j].shape))
        xt = _combine(xt, info, row(final_norm), y, odd == depth - 1)
    return xt.reshape(bsz, seq, d)
```

```python
import functools

import jax
import jax.numpy as jnp
from jax import lax
from jax.experimental import pallas as pl
from jax.experimental.pallas import tpu as pltpu

F32 = jnp.float32
BF16 = jnp.bfloat16

D_MODEL = 2048
CONV_WIDTH = 1024
LRU_WIDTH = 1024
LRU_HEADS = 8
LRU_HEAD_DIM = LRU_WIDTH // LRU_HEADS
D_IN = 2 * CONV_WIDTH + 2 * LRU_WIDTH
CONV_K = 31
LRU_CONV_K = 4
LRU_C = 8.0
N_EXPERTS = 8
TOP_K = 2
EPS = 1e-6

LANES = 128
SUBLANES = 8
BF16_ROWS = 16
MIB = 1024 * 1024

TM_PROJ = 512
TN_IN = 1024
TS_MIX = 256
CONV_HALO = 32
LRU_HALO = 8
CONV_ROWS = 64
TM_FFN = 512
TF_FFN = 1024
TF_MOE = 768
MOE_BM = 512
MOE_XG_SLOTS = 3
TM_COMBINE = 512


def _cparams(semantics, vmem_mib):
    return pltpu.CompilerParams(dimension_semantics=semantics, vmem_limit_bytes=vmem_mib * MIB)


def _sigmoid(x):
    return 0.5 * jnp.tanh(0.5 * x) + 0.5


def _rms_norm(x, g):
    ms = jnp.mean(x * x, axis=-1, keepdims=True)
    return x * lax.rsqrt(ms + EPS) * g


def _cast_specs(casts, n_steps, index_map):
    in_specs, out_specs = [], []
    for w, layer in casts:
        _, rows, cols = w.shape
        assert rows % n_steps == 0 and (rows // n_steps) % BF16_ROWS == 0, (w.shape, n_steps)
        in_specs.append(pl.BlockSpec((None, rows // n_steps, cols),
                                     lambda *idx, layer=layer: (layer, *index_map(*idx))))
        out_specs.append(pl.BlockSpec((rows // n_steps, cols), index_map))
    return in_specs, out_specs


def _cast_shapes(casts):
    return [jax.ShapeDtypeStruct(w.shape[1:], BF16) for w, _ in casts]


def _cast_args(casts):
    return [w for w, _ in casts]


def _run_casts(in_refs, out_refs):
    for src, dst in zip(in_refs, out_refs, strict=True):
        dst[...] = src[...].astype(dst.dtype)


def _in_proj_kernel(x_ref, g_ref, w_ref, *refs, n_cast):
    cast_in, o_ref, cast_out = refs[:n_cast], refs[n_cast], refs[n_cast + 1:]
    _run_casts(cast_in, cast_out)
    h = _rms_norm(x_ref[...], g_ref[...]).astype(BF16)
    for j in range(o_ref.shape[1] // TN_IN):
        cols = slice(j * TN_IN, (j + 1) * TN_IN)
        o_ref[:, cols] = jnp.dot(h, w_ref[:, cols], preferred_element_type=F32)


def _in_proj(x, g, w, casts):
    t, d = x.shape
    n = w.shape[1]
    steps = t // TM_PROJ
    cast_in, cast_out = _cast_specs(casts, steps, lambda i: (i, 0))
    return pl.pallas_call(
        functools.partial(_in_proj_kernel, n_cast=len(casts)),
        grid=(steps,),
        in_specs=[
            pl.BlockSpec((TM_PROJ, d), lambda i: (i, 0)),
            pl.BlockSpec((1, d), lambda i: (0, 0)),
            pl.BlockSpec((d, n), lambda i: (0, 0), pipeline_mode=pl.Buffered(1)),
            *cast_in,
        ],
        out_specs=[pl.BlockSpec((TM_PROJ, n), lambda i: (i, 0)), *cast_out],
        out_shape=[jax.ShapeDtypeStruct((t, n), F32), *_cast_shapes(casts)],
        compiler_params=_cparams(("parallel",), 60),
        name="in_proj",
    )(x, g, w, *_cast_args(casts))


def _causal_depthwise_conv(buf, w_ref, b_ref, out_ref, ts, halo, n_taps):
    base = halo - (n_taps - 1)
    nt = CONV_ROWS // SUBLANES
    nq = (base + n_taps - 1) // SUBLANES + 1
    row8 = lax.broadcasted_iota(jnp.int32, (SUBLANES, LANES), 0)
    for rc in range(ts // CONV_ROWS):
        t0 = rc * CONV_ROWS
        for lc in range(buf.shape[1] // LANES):
            cols = slice(lc * LANES, (lc + 1) * LANES)
            xs = [buf[t0 + SUBLANES * j:t0 + SUBLANES * (j + 1), cols] for j in range(nt + nq - 1)]
            bias = jnp.broadcast_to(b_ref[:, cols], (SUBLANES, LANES))
            out = [bias] * nt
            for p in range(SUBLANES):
                taps = [(q, SUBLANES * q + p - base) for q in range(nq)
                        if 0 <= SUBLANES * q + p - base < n_taps]
                if not taps:
                    continue
                ws = [jnp.broadcast_to(w_ref[k:k + 1, cols], (SUBLANES, LANES)) for _, k in taps]
                zs = []
                for j in range(nt + (1 if p else 0)):
                    z = ws[0] * xs[j + taps[0][0]]
                    for w, (q, _) in zip(ws[1:], taps[1:]):
                        z = z + w * xs[j + q]
                    zs.append(z)
                if p == 0:
                    out = [o + z for o, z in zip(out, zs)]
                else:
                    rs = [pltpu.roll(z, SUBLANES - p, axis=0) for z in zs]
                    keep = row8 < SUBLANES - p
                    out = [o + jnp.where(keep, rs[g], rs[g + 1]) for g, o in enumerate(out)]
            for g in range(nt):
                out_ref[t0 + SUBLANES * g:t0 + SUBLANES * (g + 1), cols] = out[g]


def _mixer_kernel(u_ref, cw_ref, cb_ref, lg_ref, lb_ref, rw_ref, rb_ref, wa_ref, ba_ref,
                  wx_ref, bx_ref, lam_ref, *refs, n_cast):
    cast_in, o_ref, cast_out = refs[:n_cast], refs[n_cast], refs[n_cast + 1:2 * n_cast + 1]
    cbuf, rbuf, abuf, bbuf, hc_ref = refs[2 * n_cast + 1:]
    s = pl.program_id(1)
    ts = TS_MIX
    _run_casts(cast_in, cast_out)

    @pl.when(s == 0)
    def _():
        cbuf[0:CONV_HALO, :] = jnp.zeros((CONV_HALO, CONV_WIDTH), F32)
        rbuf[0:LRU_HALO, :] = jnp.zeros((LRU_HALO, LRU_WIDTH), F32)
        hc_ref[...] = jnp.zeros((SUBLANES, LRU_WIDTH), F32)

    val = u_ref[:, 0:CONV_WIDTH]
    gate = u_ref[:, CONV_WIDTH:2 * CONV_WIDTH]
    cbuf[CONV_HALO:CONV_HALO + ts, :] = val * _sigmoid(gate)
    _causal_depthwise_conv(cbuf, cw_ref, cb_ref, abuf, ts, CONV_HALO, CONV_K)
    cbuf[0:CONV_HALO, :] = cbuf[ts:ts + CONV_HALO, :]
    c = abuf[...]
    mu = jnp.mean(c, axis=-1, keepdims=True)
    cc = c - mu
    var = jnp.mean(cc * cc, axis=-1, keepdims=True)
    cn = cc * lax.rsqrt(var + EPS) * lg_ref[...] + lb_ref[...]
    o_ref[:, 0:CONV_WIDTH] = (cn * _sigmoid(cn)).astype(o_ref.dtype)

    rbuf[LRU_HALO:LRU_HALO + ts, :] = u_ref[:, 2 * CONV_WIDTH:2 * CONV_WIDTH + LRU_WIDTH]
    _causal_depthwise_conv(rbuf, rw_ref, rb_ref, bbuf, ts, LRU_HALO, LRU_CONV_K)
    rbuf[0:LRU_HALO, :] = rbuf[ts:ts + LRU_HALO, :]
    for h in range(LRU_HEADS):
        cols = slice(h * LRU_HEAD_DIM, (h + 1) * LRU_HEAD_DIM)
        xh = bbuf[:, cols]
        xh_b = xh.astype(BF16)
        ga = jnp.dot(xh_b, wa_ref[h], preferred_element_type=F32) + ba_ref[:, cols]
        gx = jnp.dot(xh_b, wx_ref[h], preferred_element_type=F32) + bx_ref[:, cols]
        lam = lam_ref[:, cols]
        e = jnp.exp(-jnp.abs(lam))
        e1 = 1.0 + e
        log1p_e = jnp.where(e1 == 1.0, e, jnp.log(e1) * (e / (e1 - 1.0)))
        sp = jnp.maximum(-lam, 0.0) + log1p_e
        log_a = (-LRU_C) * _sigmoid(ga) * sp
        a = jnp.exp(log_a)
        mult = jnp.sqrt(-jnp.tanh(log_a) * (1.0 + a * a))
        row = lax.broadcasted_iota(jnp.int32, (ts, LRU_HEAD_DIM), 0)
        mult = jnp.where((row == 0) & (s == 0), 1.0, mult)
        abuf[:, cols] = a
        bbuf[:, cols] = mult * (_sigmoid(gx) * xh)

    row8 = lax.broadcasted_iota(jnp.int32, (SUBLANES, LRU_WIDTH), 0)
    h_prev = hc_ref[...]
    for g in range(ts // SUBLANES):
        rows = slice(g * SUBLANES, (g + 1) * SUBLANES)
        a = abuf[rows, :]
        b = bbuf[rows, :]
        for dsh in (1, 2, 4):
            a_s = pltpu.roll(a, dsh, axis=0)
            b_s = pltpu.roll(b, dsh, axis=0)
            m = row8 >= dsh
            b = jnp.where(m, a * b_s + b, b)
            a = jnp.where(m, a * a_s, a)
        hg = a * h_prev + b
        abuf[rows, :] = hg
        h_prev = jnp.broadcast_to(hg[SUBLANES - 1:SUBLANES, :], (SUBLANES, LRU_WIDTH))
    hc_ref[...] = h_prev

    rg = u_ref[:, 2 * CONV_WIDTH + LRU_WIDTH:D_IN]
    gelu = 0.5 * rg * (1.0 + jnp.tanh(0.7978845608028654 * (rg + 0.044715 * (rg * rg * rg))))
    o_ref[:, CONV_WIDTH:CONV_WIDTH + LRU_WIDTH] = (abuf[...] * gelu).astype(o_ref.dtype)


def _mixer_core(u, bsz, seq, cw, cb, lg, lb, rw, rb, wa, ba, wx, bx, lam, casts):
    t = u.shape[0]
    nsb = seq // TS_MIX
    row = lambda n: pl.BlockSpec((1, n), lambda b, s: (0, 0))
    cast_in, cast_out = _cast_specs(casts, bsz * nsb, lambda b, s: (b * nsb + s, 0))
    return pl.pallas_call(
        functools.partial(_mixer_kernel, n_cast=len(casts)),
        grid=(bsz, nsb),
        in_specs=[
            pl.BlockSpec((TS_MIX, D_IN), lambda b, s: (b * nsb + s, 0)),
            pl.BlockSpec((CONV_K, CONV_WIDTH), lambda b, s: (0, 0)),
            row(CONV_WIDTH), row(CONV_WIDTH), row(CONV_WIDTH),
            pl.BlockSpec((LRU_CONV_K, LRU_WIDTH), lambda b, s: (0, 0)),
            row(LRU_WIDTH),
            pl.BlockSpec((LRU_HEADS, LRU_HEAD_DIM, LRU_HEAD_DIM), lambda b, s: (0, 0, 0)),
            row(LRU_WIDTH),
            pl.BlockSpec((LRU_HEADS, LRU_HEAD_DIM, LRU_HEAD_DIM), lambda b, s: (0, 0, 0)),
            row(LRU_WIDTH), row(LRU_WIDTH),
            *cast_in,
        ],
        out_specs=[pl.BlockSpec((TS_MIX, CONV_WIDTH + LRU_WIDTH), lambda b, s: (b * nsb + s, 0)), *cast_out],
        out_shape=[jax.ShapeDtypeStruct((t, CONV_WIDTH + LRU_WIDTH), BF16), *_cast_shapes(casts)],
        scratch_shapes=[
            pltpu.VMEM((CONV_HALO + TS_MIX, CONV_WIDTH), F32),
            pltpu.VMEM((LRU_HALO + TS_MIX, LRU_WIDTH), F32),
            pltpu.VMEM((TS_MIX, LRU_WIDTH), F32),
            pltpu.VMEM((TS_MIX, LRU_WIDTH), F32),
            pltpu.VMEM((SUBLANES, LRU_WIDTH), F32),
        ],
        compiler_params=_cparams(("parallel", "arbitrary"), 56),
        name="mixer_core",
    )(u, cw, cb, lg, lb, rw, rb, wa, ba, wx, bx, lam, *_cast_args(casts))


def _top2_info(logits, width):
    lane = lax.broadcasted_iota(jnp.int32, logits.shape, 1)
    neg = jnp.float32(-jnp.inf)
    l1 = jnp.where(lane < N_EXPERTS, logits, neg)
    m1 = jnp.max(l1, axis=-1, keepdims=True)
    i1 = jnp.min(jnp.where(l1 == m1, lane, LANES), axis=-1, keepdims=True)
    l2 = jnp.where(lane == i1, neg, l1)
    m2 = jnp.max(l2, axis=-1, keepdims=True)
    i2 = jnp.min(jnp.where(l2 == m2, lane, LANES), axis=-1, keepdims=True)
    dlt = jnp.exp(m2 - m1)
    w1 = 1.0 / (1.0 + dlt)
    w2 = dlt / (1.0 + dlt)
    out = jnp.where(lane == 0, i1.astype(F32),
                    jnp.where(lane == 1, i2.astype(F32),
                              jnp.where(lane == 2, w1, jnp.where(lane == 3, w2, 0.0))))
    return out[:, 0:width]


def _out_proj_kernel(x_ref, y_ref, w_ref, *refs, n_cast, route):
    refs = list(refs)
    if route:
        gf_ref, wr_ref = refs[:2]
        refs = refs[2:]
    cast_in, refs = refs[:n_cast], refs[n_cast:]
    o_ref, refs = refs[0], refs[1:]
    if route:
        info_ref, refs = refs[0], refs[1:]
    cast_out = refs
    _run_casts(cast_in, cast_out)
    xn = x_ref[...] + jnp.dot(y_ref[...], w_ref[...], preferred_element_type=F32)
    o_ref[...] = xn
    if route:
        h = _rms_norm(xn, gf_ref[...]).astype(BF16)
        logits = jnp.dot(h, wr_ref[...], preferred_element_type=F32)
        info_ref[...] = _top2_info(logits, info_ref.shape[1])


def _out_proj(x, y, w, casts, router=None):
    t, d = x.shape
    k = y.shape[1]
    steps = t // TM_PROJ
    cast_in, cast_out = _cast_specs(casts, steps, lambda i: (i, 0))
    const = lambda shape: pl.BlockSpec(shape, lambda i: (0, 0))
    route_in = [] if router is None else [const((1, d)), const((d, LANES))]
    route_out = [] if router is None else [pl.BlockSpec((TM_PROJ, SUBLANES), lambda i: (i, 0))]
    route_shape = [] if router is None else [jax.ShapeDtypeStruct((t, SUBLANES), F32)]
    return pl.pallas_call(
        functools.partial(_out_proj_kernel, n_cast=len(casts), route=router is not None),
        grid=(steps,),
        in_specs=[
            pl.BlockSpec((TM_PROJ, d), lambda i: (i, 0)),
            pl.BlockSpec((TM_PROJ, k), lambda i: (i, 0)),
            const((k, d)),
            *route_in,
            *cast_in,
        ],
        out_specs=[pl.BlockSpec((TM_PROJ, d), lambda i: (i, 0)), *route_out, *cast_out],
        out_shape=[jax.ShapeDtypeStruct((t, d), F32), *route_shape, *_cast_shapes(casts)],
        compiler_params=_cparams(("parallel",), 56),
        name="out_proj",
    )(x, y, w, *(router or ()), *_cast_args(casts))


def _swiglu_step(h, wg, wu, wd):
    g = jnp.dot(h, wg, preferred_element_type=F32)
    v = jnp.dot(h, wu, preferred_element_type=F32)
    act = (g * _sigmoid(g) * v).astype(BF16)
    return jnp.dot(act, wd, preferred_element_type=F32)


def _dense_ffn_kernel(x_ref, g_ref, wg_ref, wu_ref, wd_ref, *refs, n_cast):
    cast_in, o_ref, cast_out, h_ref = refs[:n_cast], refs[n_cast], refs[n_cast + 1:2 * n_cast + 1], refs[-1]
    _run_casts(cast_in, cast_out)

    @pl.when(pl.program_id(1) == 0)
    def _():
        h_ref[...] = _rms_norm(x_ref[...], g_ref[...]).astype(BF16)
        o_ref[...] = x_ref[...]

    o_ref[...] += _swiglu_step(h_ref[...], wg_ref[...], wu_ref[...], wd_ref[...])


def _dense_ffn(x, g, wg, wu, wd, casts):
    t, d = x.shape
    ff = wg.shape[1]
    nf = ff // TF_FFN
    cast_in, cast_out = _cast_specs(casts, t // TM_FFN * nf, lambda i, f: (i * nf + f, 0))
    return pl.pallas_call(
        functools.partial(_dense_ffn_kernel, n_cast=len(casts)),
        grid=(t // TM_FFN, nf),
        in_specs=[
            pl.BlockSpec((TM_FFN, d), lambda i, f: (i, 0)),
            pl.BlockSpec((1, d), lambda i, f: (0, 0)),
            pl.BlockSpec((d, TF_FFN), lambda i, f: (0, f)),
            pl.BlockSpec((d, TF_FFN), lambda i, f: (0, f)),
            pl.BlockSpec((TF_FFN, d), lambda i, f: (f, 0)),
            *cast_in,
        ],
        out_specs=[pl.BlockSpec((TM_FFN, d), lambda i, f: (i, 0)), *cast_out],
        out_shape=[jax.ShapeDtypeStruct((t, d), F32), *_cast_shapes(casts)],
        scratch_shapes=[pltpu.VMEM((TM_FFN, d), BF16)],
        compiler_params=_cparams(("parallel", "arbitrary"), 60),
        name="dense_ffn",
    )(x, g, wg, wu, wd, *_cast_args(casts))


def _moe_ffn_kernel(tok_ref, dst_ref, be_ref, nr_ref, x_hbm, g_ref, wg_ref, wu_ref, wd_ref, y_hbm,
                    xg_ref, h_ref, acc_ref, gsem, ssem, *, n_tok):
    b = pl.program_id(0)
    f = pl.program_id(1)
    nb = pl.num_programs(0)
    nf = pl.num_programs(1)
    n_real = nr_ref[0]
    real = b < n_real
    slot = b % 2
    gslot = b % MOE_XG_SLOTS
    rows_per_step = MOE_BM // nf

    def gather_start(tok, r, slt):
        pltpu.make_async_copy(x_hbm.at[pl.ds(tok, 1), :], xg_ref.at[slt, pl.ds(r, 1), :],
                              gsem.at[slt]).start()

    def scatter_start(a, r, slt, priority=0):
        pltpu.make_async_copy(acc_ref.at[slt, pl.ds(r, 1), :], y_hbm.at[pl.ds(a, 1), :],
                              ssem).start(priority=priority)

    def gather_wait(slt):
        pltpu.make_async_copy(x_hbm.at[pl.ds(0, MOE_BM), :], xg_ref.at[slt], gsem.at[slt]).wait()

    def scatter_wait():
        pltpu.make_async_copy(acc_ref.at[0], y_hbm.at[pl.ds(0, MOE_BM), :], ssem).wait()

    def drain(blk):
        gather_wait((blk + 1) % MOE_XG_SLOTS)
        gather_wait((blk + 2) % MOE_XG_SLOTS)
        scatter_wait()

        def body(r, carry):
            scatter_start(dst_ref[blk * MOE_BM + r], r, blk % 2)
            return carry

        lax.fori_loop(0, MOE_BM, body, 0, unroll=8)
        scatter_wait()

    @pl.when(real & (f == 0))
    def _():
        @pl.when(b == 0)
        def _():
            def body(r, carry):
                gather_start(tok_ref[r], r, 0)
                gather_start(tok_ref[MOE_BM + r], r, 1)
                return carry

            lax.fori_loop(0, MOE_BM, body, 0, unroll=8)
            acc_ref[1] = jnp.zeros(acc_ref.shape[1:], acc_ref.dtype)

        gather_wait(gslot)
        h_ref[...] = _rms_norm(xg_ref[gslot], g_ref[...]).astype(BF16)

        @pl.when(b > 0)
        def _():
            scatter_wait()

        acc_ref[slot] = jnp.zeros(acc_ref.shape[1:], acc_ref.dtype)

    @pl.when(real)
    def _():
        prev_base = jnp.maximum(b - 1, 0) * MOE_BM
        for i in range(rows_per_step):
            r = f * rows_per_step + i
            gather_start(tok_ref[(b + 2) * MOE_BM + r], r, (b + 2) % MOE_XG_SLOTS)
            a_prev = jnp.where(b == 0, TOP_K * n_tok + r, dst_ref[prev_base + r])
            scatter_start(a_prev, r, 1 - slot, priority=i % 2)
        acc_ref[slot] += _swiglu_step(h_ref[...], wg_ref[0], wu_ref[0], wd_ref[0])

    @pl.when((b == n_real) & (f == 0))
    def _():
        drain(b - 1)

    @pl.when(real & (b == nb - 1) & (f == nf - 1))
    def _():
        drain(b)


def _moe_ffn(tok, dst, block_e, n_real, x, g, wg, wu, wd):
    t, d = x.shape
    ff = wg.shape[2]
    nf = ff // TF_MOE
    n_blocks = block_e.shape[0]
    assert MOE_BM % nf == 0
    assert tok.shape[0] == (n_blocks + MOE_XG_SLOTS - 1) * MOE_BM == dst.shape[0]

    def f_eff(b, f, nr):
        return jnp.where(b < nr[0], f, nf - 1)

    grid_spec = pltpu.PrefetchScalarGridSpec(
        num_scalar_prefetch=4,
        grid=(n_blocks, nf),
        in_specs=[
            pl.BlockSpec(memory_space=pl.ANY),
            pl.BlockSpec((1, d), lambda b, f, tok, dst, be, nr: (0, 0)),
            pl.BlockSpec((1, d, TF_MOE), lambda b, f, tok, dst, be, nr: (be[b], 0, f_eff(b, f, nr))),
            pl.BlockSpec((1, d, TF_MOE), lambda b, f, tok, dst, be, nr: (be[b], 0, f_eff(b, f, nr))),
            pl.BlockSpec((1, TF_MOE, d), lambda b, f, tok, dst, be, nr: (be[b], f_eff(b, f, nr), 0)),
        ],
        out_specs=pl.BlockSpec(memory_space=pl.ANY),
        scratch_shapes=[
            pltpu.VMEM((MOE_XG_SLOTS, MOE_BM, d), F32),
            pltpu.VMEM((MOE_BM, d), BF16),
            pltpu.VMEM((2, MOE_BM, d), F32),
            pltpu.SemaphoreType.DMA((MOE_XG_SLOTS,)),
            pltpu.SemaphoreType.DMA(()),
        ],
    )
    return pl.pallas_call(
        functools.partial(_moe_ffn_kernel, n_tok=t),
        grid_spec=grid_spec,
        out_shape=jax.ShapeDtypeStruct((TOP_K * t + MOE_BM, d), F32),
        compiler_params=_cparams(("arbitrary", "arbitrary"), 56),
        name="moe_ffn",
    )(tok, dst, block_e, n_real, x, g, wg, wu, wd)


def _combine_kernel(x_ref, info_ref, g_ref, ya_ref, yb_ref, o_ref, *, final_norm):
    w1 = info_ref[:, 2:3]
    w2 = info_ref[:, 3:4]
    z = x_ref[...] + (ya_ref[...] * w1 + yb_ref[...] * w2)
    o_ref[...] = _rms_norm(z, g_ref[...]) if final_norm else z


def _combine(x, info, g, y, final_norm):
    t, d = x.shape
    nt = t // TM_COMBINE
    return pl.pallas_call(
        functools.partial(_combine_kernel, final_norm=final_norm),
        grid=(nt,),
        in_specs=[
            pl.BlockSpec((TM_COMBINE, d), lambda i: (i, 0)),
            pl.BlockSpec((TM_COMBINE, SUBLANES), lambda i: (i, 0)),
            pl.BlockSpec((1, d), lambda i: (0, 0)),
            pl.BlockSpec((TM_COMBINE, d), lambda i: (i, 0)),
            pl.BlockSpec((TM_COMBINE, d), lambda i: (nt + i, 0)),
        ],
        out_specs=pl.BlockSpec((TM_COMBINE, d), lambda i: (i, 0)),
        out_shape=jax.ShapeDtypeStruct((t, d), F32),
        compiler_params=_cparams(("parallel",), 48),
        name="moe_combine",
    )(x, info, g, y, y)


def _routing_plan(info, n_tok):
    n_asg = n_tok * TOP_K
    n_blocks = n_asg // MOE_BM + N_EXPERTS
    flat_e = info[:, 0:TOP_K].astype(jnp.int32).reshape(n_asg)
    onehot = (flat_e[None, :] == jnp.arange(N_EXPERTS, dtype=jnp.int32)[:, None]).astype(jnp.int32)
    csum = jnp.cumsum(onehot, axis=1)
    counts = csum[:, -1]
    padded = ((counts + MOE_BM - 1) // MOE_BM) * MOE_BM
    p_ends = jnp.cumsum(padded)
    p_starts = p_ends - padded
    pos = jnp.sum(onehot * (csum - 1 + p_starts[:, None]), axis=0).astype(jnp.int32)
    flat = jnp.arange(n_asg, dtype=jnp.int32)
    dest = (flat % TOP_K) * n_tok + flat // TOP_K
    spare = n_asg + jnp.arange((n_blocks + MOE_XG_SLOTS - 1) * MOE_BM, dtype=jnp.int32) % MOE_BM
    dst = spare.at[pos].set(dest, unique_indices=True, mode="promise_in_bounds")
    tok = jnp.where(dst >= n_asg, 0, dst % n_tok)
    block_start = jnp.arange(n_blocks, dtype=jnp.int32) * MOE_BM
    block_e = jnp.minimum(jnp.searchsorted(p_ends, block_start, side="right"), N_EXPERTS - 1).astype(jnp.int32)
    n_real = (p_ends[-1] // MOE_BM).astype(jnp.int32).reshape(1)
    return tok, dst, block_e, n_real


def kernel(x, mix_norm, w_in, conv_w, conv_b, conv_ln_g, conv_ln_b, lru_conv_w, lru_conv_b, lru_wa, lru_ba, lru_wx, lru_bx, lru_lambda, w_out, ffn_norm, dense_wg, dense_wu, dense_wd, w_router, moe_wg, moe_wu, moe_wd, final_norm):
    bsz, seq, d = x.shape
    depth = w_in.shape[0]
    t = bsz * seq
    xt = x.reshape(t, d)
    row = lambda v: v.reshape(1, -1)
    assert depth % 2 == 0, "the final RMSNorm is fused into the last routed layer's combine"
    fold = lambda w: w.reshape(w.shape[0], -1, w.shape[-1])
    w_in_b = w_in[0].astype(BF16)
    for layer in range(0, depth, 2):
        j = layer // 2
        odd = layer + 1
        u, wg_d, wu_d, w_out_b = _in_proj(xt, row(mix_norm[layer]), w_in_b,
                                           [(dense_wg, j), (dense_wu, j), (w_out, layer)])
        y, moe_wg_b, wd_d = _mixer_core(
            u, bsz, seq, conv_w[layer], row(conv_b[layer]), row(conv_ln_g[layer]), row(conv_ln_b[layer]),
            lru_conv_w[layer], row(lru_conv_b[layer]), lru_wa[layer].astype(BF16), row(lru_ba[layer]),
            lru_wx[layer].astype(BF16), row(lru_bx[layer]), row(lru_lambda[layer]),
            [(fold(moe_wg), j), (dense_wd, j)])
        xt, w_in_odd = _out_proj(xt, y, w_out_b, [(w_in, odd)])
        xt, moe_wd_b = _dense_ffn(xt, row(ffn_norm[layer]), wg_d, wu_d, wd_d, [(fold(moe_wd), j)])

        hosted = [(w_out, odd)] + ([(w_in, odd + 1)] if odd + 1 < depth else [])
        u, w_out_b, *nxt = _in_proj(xt, row(mix_norm[odd]), w_in_odd, hosted)
        if nxt:
            w_in_b = nxt[0]
        y, moe_wu_b = _mixer_core(
            u, bsz, seq, conv_w[odd], row(conv_b[odd]), row(conv_ln_g[odd]), row(conv_ln_b[odd]),
            lru_conv_w[odd], row(lru_conv_b[odd]), lru_wa[odd].astype(BF16), row(lru_ba[odd]),
            lru_wx[odd].astype(BF16), row(lru_bx[odd]), row(lru_lambda[odd]), [(fold(moe_wu), j)])
        wr = jnp.zeros((d, LANES), BF16).at[:, 0:N_EXPERTS].set(w_router[j].astype(BF16))
        g = row(ffn_norm[odd])
        xt, info = _out_proj(xt, y, w_out_b, [], router=(g, wr))
        tok, dst, block_e, n_real = _routing_plan(info, t)
        y = _moe_ffn(tok, dst, block_e, n_real, xt, g, moe_wg_b.reshape(moe_wg[j].shape),
                     moe_wu_b.reshape(moe_wu[j].shape), moe_wd_b.reshape(moe_wd[j].shape))
        xt = _combine(xt, info, row(final_norm), y, odd == depth - 1)
    return xt.reshape(bsz, seq, d)
```
